```python
import math
import jax, jax.numpy as jnp
from jax import lax
import numpy as np

D_MODEL = 1024
BATCH = 16
SEQ = 256
DEPTH = 4
DEC_BATCH = 2
DEC_SEQ = 4096
PAST_LEN = 256

GRID_W = 64
BLOCK = 128
EPS = 1e-6
N_HEADS = 16
N_KV_HEADS = 4
HEAD_DIM = 64
GROUP = N_HEADS // N_KV_HEADS
WINDOW = 128
ROPE_BASE = 10000.0
RET_HEADS = D_MODEL // 256
RET_DK = D_MODEL // RET_HEADS
RET_DV = 2 * D_MODEL // RET_HEADS
RET_HK = RET_HEADS * RET_DK
RET_HV = RET_HEADS * RET_DV
D_FF = 2816
N_EXPERTS = 8
TOP_K = 2
D_FF_EXPERT = 3584
N_EVEN = (DEPTH + 1) // 2
N_ODD = DEPTH // 2

kernel_name = "hybrid_dit_window_gqa_retention_step"

F32 = jnp.float32
NEG = -1e30


def rmsnorm(x, g):
    xf = x.astype(F32)
    y = xf * lax.rsqrt(jnp.mean(xf * xf, axis=-1, keepdims=True) + EPS)
    return (y * g.astype(F32)).astype(x.dtype)


def head_norm(x):
    xf = x.astype(F32)
    return (xf * lax.rsqrt(jnp.mean(xf * xf, axis=-1, keepdims=True) + EPS)).astype(x.dtype)


def rope_2d(x):
    n_tok, dh = x.shape[1], x.shape[-1]
    nf = dh // 4
    t = jnp.arange(n_tok)
    freqs = ROPE_BASE ** (-jnp.arange(nf, dtype=F32) / nf)
    row = (t // GRID_W).astype(F32)
    col = (t % GRID_W).astype(F32)
    ang = jnp.concatenate([row[:, None] * freqs, col[:, None] * freqs], axis=-1)
    bshape = (1, n_tok) + (1,) * (x.ndim - 3) + (dh // 2,)
    cos = jnp.cos(ang).reshape(bshape)
    sin = jnp.sin(ang).reshape(bshape)
    x1, x2 = jnp.split(x.astype(F32), 2, axis=-1)
    return jnp.concatenate([x1 * cos - x2 * sin, x1 * sin + x2 * cos], axis=-1).astype(x.dtype)


def modulations(cond, w_mod, b_mod):
    m = jax.nn.silu(cond) @ w_mod + b_mod
    return [t[:, None, :] for t in jnp.split(m, 6, axis=-1)]


def pre(x, g, shift, scale):
    return rmsnorm(x, g) * (1 + scale) + shift


def sink_attention(q, k, v, sink, mask):
    s = jnp.einsum('bqhgd,bkhd->bhgqk', q, k).astype(F32) * (HEAD_DIM ** -0.5)
    if mask is not None:
        s = jnp.where(mask, s, NEG)
    sk = sink.astype(F32)[None, :, :, None, None]
    m = jnp.maximum(jnp.max(s, axis=-1, keepdims=True), sk)
    p = jnp.exp(s - m)
    denom = jnp.sum(p, axis=-1, keepdims=True) + jnp.exp(sk - m)
    p = (p / denom).astype(v.dtype)
    return jnp.einsum('bhgqk,bkhd->bqhgd', p, v)


def attn_qkv(h, w_qkv):
    b, l, _ = h.shape
    q, k, v = jnp.split(h @ w_qkv, [N_HEADS * HEAD_DIM, (N_HEADS + N_KV_HEADS) * HEAD_DIM], axis=-1)
    return (q.reshape(b, l, N_KV_HEADS, GROUP, HEAD_DIM),
            k.reshape(b, l, N_KV_HEADS, HEAD_DIM),
            v.reshape(b, l, N_KV_HEADS, HEAD_DIM))


def attn_context(h, w_qkv, w_o, sink):
    b, l, _ = h.shape
    q, k, v = attn_qkv(h, w_qkv)
    nb = l // BLOCK
    qb = q.reshape(b, nb, BLOCK, N_KV_HEADS, GROUP, HEAD_DIM).swapaxes(0, 1)
    o = lax.map(lambda qi: sink_attention(qi, k, v, sink, None), qb)
    o = o.swapaxes(0, 1).reshape(b, l, N_HEADS * HEAD_DIM)
    return o @ w_o, k, v


def attn_latent(h, k_ctx, v_ctx, w_qkv, w_o, sink):
    b, l, _ = h.shape
    q, k, v = attn_qkv(h, w_qkv)
    q = rope_2d(q)
    k = rope_2d(k)
    nb = l // BLOCK
    pad = ((0, 0), (BLOCK, BLOCK), (0, 0), (0, 0))
    kp = jnp.pad(k, pad)
    vp = jnp.pad(v, pad)
    qi = jnp.arange(BLOCK)[:, None]
    kj = jnp.arange(3 * BLOCK)[None, :]
    band = jnp.abs(kj - BLOCK - qi) <= WINDOW
    ctx_mask = jnp.ones((BLOCK, k_ctx.shape[1]), dtype=bool)
    k_ctx = k_ctx.astype(k.dtype)
    v_ctx = v_ctx.astype(v.dtype)

    def one_block(blk):
        start = blk * BLOCK
        qb = lax.dynamic_slice_in_dim(q, start, BLOCK, axis=1)
        kb = lax.dynamic_slice_in_dim(kp, start, 3 * BLOCK, axis=1)
        vb = lax.dynamic_slice_in_dim(vp, start, 3 * BLOCK, axis=1)
        key_pos = start - BLOCK + jnp.arange(3 * BLOCK)
        valid = band & ((key_pos >= 0) & (key_pos < l))[None, :]
        mask = jnp.concatenate([ctx_mask, valid], axis=1)
        return sink_attention(qb, jnp.concatenate([k_ctx, kb], axis=1),
                              jnp.concatenate([v_ctx, vb], axis=1), sink, mask)

    o = lax.map(one_block, jnp.arange(nb))
    o = o.swapaxes(0, 1).reshape(b, l, N_HEADS * HEAD_DIM)
    return o @ w_o


def ret_project(h, w_in):
    b, l, _ = h.shape
    q, k, v, gf, gb = jnp.split(h @ w_in, [RET_HK, 2 * RET_HK, 2 * RET_HK + RET_HV,
                                           2 * RET_HK + 2 * RET_HV], axis=-1)
    q = q.reshape(b, l, RET_HEADS, RET_DK)
    k = k.reshape(b, l, RET_HEADS, RET_DK)
    v = v.reshape(b, l, RET_HEADS, RET_DV)
    return q, k, v, gf.reshape(b, l, RET_HEADS, RET_DV), gb.reshape(b, l, RET_HEADS, RET_DV)


def retention_scan(q, k, v, log_gamma, s0):
    b, l, nh, _ = q.shape
    dt = q.dtype
    nc = l // BLOCK
    idx = jnp.arange(BLOCK, dtype=F32)
    diff = idx[:, None] - idx[None, :]
    intra = jnp.where(diff >= 0, jnp.exp(log_gamma[:, None, None] * jnp.maximum(diff, 0.0)), 0.0).astype(dt)
    q_dec = jnp.exp(log_gamma[:, None] * (idx + 1.0)).astype(dt)[..., None]
    k_dec = jnp.exp(log_gamma[:, None] * (BLOCK - 1.0 - idx)).astype(dt)[..., None]
    c_dec = jnp.exp(log_gamma * BLOCK).astype(dt)[:, None, None]

    def to_chunks(x):
        return x.reshape(b, nc, BLOCK, nh, x.shape[-1]).transpose(1, 0, 3, 2, 4)

    def step(s, qkv):
        qc, kc, vc = qkv
        a = jnp.einsum('bhid,bhjd->bhij', qc, kc) * intra
        o = jnp.einsum('bhij,bhjv->bhiv', a, vc) + jnp.einsum('bhid,bhdv->bhiv', qc * q_dec, s)
        s = c_dec * s + jnp.einsum('bhjd,bhjv->bhdv', kc * k_dec, vc)
        return s, o

    s, o = lax.scan(step, s0.astype(dt), (to_chunks(q), to_chunks(k), to_chunks(v)))
    o = o.transpose(1, 0, 3, 2, 4).reshape(b, l, nh, v.shape[-1])
    return o, s


def retention_mix(q, k, v, gf, gb, decay_logit, s0_f, s0_b, w_out):
    log_g = jax.nn.log_sigmoid(decay_logit.astype(F32))
    k = k * (RET_DK ** -0.5)
    of, sf = retention_scan(q, k, v, log_g[0], s0_f)
    ob, sb = retention_scan(q[:, ::-1], k[:, ::-1], v[:, ::-1], log_g[1], s0_b)
    ob = ob[:, ::-1]
    o = head_norm(of) * jax.nn.silu(gf) + head_norm(ob) * jax.nn.silu(gb)
    b, l = q.shape[:2]
    return o.reshape(b, l, RET_HV) @ w_out, sf, sb


def ret_context(h, w_in, decay_logit, w_out):
    q, k, v, gf, gb = ret_project(h, w_in)
    zeros = jnp.zeros((h.shape[0], RET_HEADS, RET_DK, RET_DV), dtype=q.dtype)
    return retention_mix(q, k, v, gf, gb, decay_logit, zeros, zeros, w_out)


def ret_latent(h, s_f, s_b, w_in, decay_logit, w_out):
    q, k, v, gf, gb = ret_project(h, w_in)
    q = rope_2d(q)
    k = rope_2d(k)
    o, _, _ = retention_mix(q, k, v, gf, gb, decay_logit, s_f, s_b, w_out)
    return o


def swiglu(h, w_gu, w_down):
    g, u = jnp.split(h @ w_gu, 2, axis=-1)
    return (jax.nn.silu(g) * u) @ w_down


def moe(h, w_router, w_gu, w_down):
    logits = (h @ w_router).astype(F32)
    top_val, top_idx = lax.top_k(logits, TOP_K)
    probs = jax.nn.softmax(top_val, axis=-1)
    gates = jnp.sum(jax.nn.one_hot(top_idx, N_EXPERTS, dtype=F32) * probs[..., None], axis=-2).astype(h.dtype)
    out = jnp.zeros_like(h)
    for e in range(N_EXPERTS):
        out = out + gates[..., e:e + 1] * swiglu(h, w_gu[e], w_down[e])
    return out


def setup_inputs(seed: int = 0) -> dict:
    key = jax.random.key(seed)
    ks = jax.random.split(key, 32)
    nrm = jax.random.normal
    d = D_MODEL
    base = 1.0 - 2.0 ** (-5.0 - jnp.arange(RET_HEADS, dtype=F32))
    base_logit = jnp.log(base / (1.0 - base))
    return {
        "x_prompt": nrm(ks[0], (BATCH, SEQ, d), F32),
        "x_sample": nrm(ks[1], (DEC_BATCH, DEC_SEQ, d), F32),
        "cache_k": nrm(ks[2], (DEC_BATCH, N_EVEN, PAST_LEN, N_KV_HEADS, HEAD_DIM), F32),
        "cache_v": nrm(ks[3], (DEC_BATCH, N_EVEN, PAST_LEN, N_KV_HEADS, HEAD_DIM), F32),
        "state_fwd": 0.5 * nrm(ks[4], (DEC_BATCH, N_ODD, RET_HEADS, RET_DK, RET_DV), F32),
        "state_bwd": 0.5 * nrm(ks[5], (DEC_BATCH, N_ODD, RET_HEADS, RET_DK, RET_DV), F32),
        "c": nrm(ks[6], (DEC_BATCH, d), F32),
        "c_ctx": nrm(ks[7], (d,), F32),
        "norm_mix": 1.0 + 0.02 * nrm(ks[8], (DEPTH, d), F32),
        "norm_ffn": 1.0 + 0.02 * nrm(ks[9], (DEPTH, d), F32),
        "w_mod": 0.5 * d ** -0.5 * nrm(ks[10], (DEPTH, d, 6 * d), F32),
        "b_mod": 0.02 * nrm(ks[11], (DEPTH, 6 * d), F32),
        "w_qkv": d ** -0.5 * nrm(ks[12], (N_EVEN, d, (N_HEADS + 2 * N_KV_HEADS) * HEAD_DIM), F32),
        "w_attn_o": (N_HEADS * HEAD_DIM) ** -0.5 * nrm(ks[13], (N_EVEN, N_HEADS * HEAD_DIM, d), F32),
        "attn_sink": 0.5 * nrm(ks[14], (N_EVEN, N_HEADS), F32),
        "w_ret_in": d ** -0.5 * nrm(ks[15], (N_ODD, d, 2 * RET_HK + 3 * RET_HV), F32),
        "ret_decay": base_logit + 0.1 * nrm(ks[16], (N_ODD, 2, RET_HEADS), F32),
        "w_ret_out": RET_HV ** -0.5 * nrm(ks[17], (N_ODD, RET_HV, d), F32),
        "w_ffn_gu": d ** -0.5 * nrm(ks[18], (N_EVEN, d, 2 * D_FF), F32),
        "w_ffn_down": D_FF ** -0.5 * nrm(ks[19], (N_EVEN, D_FF, d), F32),
        "w_router": d ** -0.5 * nrm(ks[20], (N_ODD, d, N_EXPERTS), F32),
        "w_exp_gu": d ** -0.5 * nrm(ks[21], (N_ODD, N_EXPERTS, d, 2 * D_FF_EXPERT), F32),
        "w_exp_down": D_FF_EXPERT ** -0.5 * nrm(ks[22], (N_ODD, N_EXPERTS, D_FF_EXPERT, d), F32),
        "final_norm": 1.0 + 0.02 * nrm(ks[23], (d,), F32),
    }


def reference(x_prompt, x_sample, cache_k, cache_v, state_fwd, state_bwd, c, c_ctx,
              norm_mix, norm_ffn, w_mod, b_mod,
              w_qkv, w_attn_o, attn_sink,
              w_ret_in, ret_decay, w_ret_out,
              w_ffn_gu, w_ffn_down,
              w_router, w_exp_gu, w_exp_down, final_norm):
    xp = x_prompt
    xs = x_sample
    new_k, new_v, new_sf, new_sb = [], [], [], []
    for i in range(DEPTH):
        j = i // 2
        mp = modulations(c_ctx[None, :], w_mod[i], b_mod[i])
        ms = modulations(c, w_mod[i], b_mod[i])
        hp = pre(xp, norm_mix[i], mp[0], mp[1])
        hs = pre(xs, norm_mix[i], ms[0], ms[1])
        if i % 2 == 0:
            sink = attn_sink[j].reshape(N_KV_HEADS, GROUP)
            op, kp, vp = attn_context(hp, w_qkv[j], w_attn_o[j], sink)
            os_ = attn_latent(hs, cache_k[:, j], cache_v[:, j], w_qkv[j], w_attn_o[j], sink)
            new_k.append(kp)
            new_v.append(vp)
        else:
            op, sf, sb = ret_context(hp, w_ret_in[j], ret_decay[j], w_ret_out[j])
            os_ = ret_latent(hs, state_fwd[:, j], state_bwd[:, j], w_ret_in[j], ret_decay[j], w_ret_out[j])
            new_sf.append(sf)
            new_sb.append(sb)
        xp = xp + mp[2] * op
        xs = xs + ms[2] * os_
        hp = pre(xp, norm_ffn[i], mp[3], mp[4])
        hs = pre(xs, norm_ffn[i], ms[3], ms[4])
        if i % 2 == 0:
            fp = swiglu(hp, w_ffn_gu[j], w_ffn_down[j])
            fs = swiglu(hs, w_ffn_gu[j], w_ffn_down[j])
        else:
            fp = moe(hp, w_router[j], w_exp_gu[j], w_exp_down[j])
            fs = moe(hs, w_router[j], w_exp_gu[j], w_exp_down[j])
        xp = xp + mp[5] * fp
        xs = xs + ms[5] * fs
    y_prompt = rmsnorm(xp, final_norm)
    y_sample = rmsnorm(xs, final_norm)
    return (y_prompt, y_sample, jnp.stack(new_k, axis=1), jnp.stack(new_v, axis=1),
            jnp.stack(new_sf, axis=1), jnp.stack(new_sb, axis=1))
```

```python
import functools

import jax
import jax.numpy as jnp
from jax import lax
from jax.experimental import pallas as pl
from jax.experimental.pallas import tpu as pltpu

F32 = jnp.float32
BF16 = jnp.bfloat16
I32 = jnp.int32

D_MODEL = 1024
BATCH = 16
SEQ = 256
DEPTH = 4
DEC_BATCH = 2
DEC_SEQ = 4096
PAST_LEN = 256
GRID_W = 64
BLOCK = 128
EPS = 1e-6
N_HEADS = 16
N_KV_HEADS = 4
HEAD_DIM = 64
GROUP = N_HEADS // N_KV_HEADS
WINDOW = 128
ROPE_BASE = 10000.0
RET_HEADS = 4
RET_DK = 256
RET_DV = 512
RET_HK = RET_HEADS * RET_DK
RET_HV = RET_HEADS * RET_DV
D_FF = 2816
N_EXPERTS = 8
TOP_K = 2
D_FF_EXPERT = 3584
N_EVEN = 2
N_ODD = 2
NEG = -1e30

GROUP_TOKENS = 4096
N_PROMPT = BATCH * SEQ
N_SAMPLE = DEC_BATCH * DEC_SEQ
N_TOK = N_PROMPT + N_SAMPLE
N_GROUPS = N_TOK // GROUP_TOKENS

R_SHIFT_MIX, R_SCALE_MIX, R_GATE_MIX, R_SHIFT_FFN, R_SCALE_FFN, R_GATE_FFN, R_G_MIX, R_G_FFN = range(8)

VMEM_LIMIT_BYTES = 56 * 1024 * 1024
LANES = 128

TM = 1024
TM_DOWN = 512
MOE_TILE = 512
MOE_SLOTS = TOP_K * N_TOK + N_EXPERTS * MOE_TILE
MOE_NT = MOE_SLOTS // MOE_TILE
TC_COMBINE = 256
GATHER_ROWS = 512


def _cparams(sem):
    return pltpu.CompilerParams(dimension_semantics=sem, vmem_limit_bytes=VMEM_LIMIT_BYTES)


def _normmod(x, g, scale, shift):
    ms = jnp.mean(x * x, axis=-1, keepdims=True)
    return (x * lax.rsqrt(ms + EPS) * g) * (1.0 + scale) + shift


def _mod_kernel(cond_ref, w_ref, b_ref, o_ref):
    c = cond_ref[...]
    s = c * jax.nn.sigmoid(c)
    o_ref[0] = jnp.dot(s.astype(BF16), w_ref[0].astype(BF16), preferred_element_type=F32) + b_ref[0]


def _modulations(cond8, w_mod, b_mod):
    tn = 2048
    n6 = 6 * D_MODEL
    return pl.pallas_call(
        _mod_kernel,
        grid=(DEPTH, n6 // tn),
        in_specs=[pl.BlockSpec((8, D_MODEL), lambda l, n: (0, 0)),
                  pl.BlockSpec((1, D_MODEL, tn), lambda l, n: (l, 0, n)),
                  pl.BlockSpec((1, 1, tn), lambda l, n: (l, 0, n))],
        out_specs=pl.BlockSpec((1, 8, tn), lambda l, n: (l, 0, n)),
        out_shape=jax.ShapeDtypeStruct((DEPTH, 8, n6), F32),
        compiler_params=_cparams(("arbitrary", "arbitrary")),
        name="modulations",
    )(cond8, w_mod, b_mod.reshape(DEPTH, 1, n6))


def _pre_kernel(x_ref, pk_ref, h_ref):
    pk = pk_ref[0, 0]
    h = _normmod(x_ref[...], pk[R_G_MIX:R_G_MIX + 1], pk[R_SCALE_MIX:R_SCALE_MIX + 1],
                 pk[R_SHIFT_MIX:R_SHIFT_MIX + 1])
    h_ref[...] = h.astype(BF16)


def _pre(x, pack):
    tm = TM
    return pl.pallas_call(
        _pre_kernel,
        grid=(N_TOK // tm,),
        in_specs=[pl.BlockSpec((tm, D_MODEL), lambda m: (m, 0)),
                  pl.BlockSpec((1, 1, 8, D_MODEL), lambda m: (0, (m * tm) // GROUP_TOKENS, 0, 0))],
        out_specs=pl.BlockSpec((tm, D_MODEL), lambda m: (m, 0)),
        out_shape=jax.ShapeDtypeStruct((N_TOK, D_MODEL), BF16),
        compiler_params=_cparams(("arbitrary",)),
        name="pre_norm",
    )(x, pack)


def _rope64(y, cos, sin_signed):
    width = y.shape[-1]
    lane = lax.broadcasted_iota(I32, y.shape, 1)
    first = (lane % HEAD_DIM) < (HEAD_DIM // 2)
    swapped = jnp.where(first, pltpu.roll(y, width - HEAD_DIM // 2, 1), pltpu.roll(y, HEAD_DIM // 2, 1))
    reps = width // LANES
    c = jnp.concatenate([cos] * reps, axis=1) if reps > 1 else cos
    s = jnp.concatenate([sin_signed] * reps, axis=1) if reps > 1 else sin_signed
    return y * c + swapped * s


def _attn_proj_kernel(h_ref, w_ref, cos_ref, sin_ref, o_ref, wb_ref, *, scale, rope_tiles, first_sample_tile):
    n = pl.program_id(0)
    m = pl.program_id(1)

    @pl.when(m == 0)
    def _():
        wb_ref[...] = w_ref[0].astype(BF16)

    y = jnp.dot(h_ref[...], wb_ref[...], preferred_element_type=F32)
    if scale != 1.0:
        y = y * scale
    do_rope = jnp.logical_and(m >= first_sample_tile, n < rope_tiles)

    @pl.when(do_rope)
    def _():
        o_ref[...] = _rope64(y, cos_ref[...], sin_ref[...]).astype(o_ref.dtype)

    @pl.when(jnp.logical_not(do_rope))
    def _():
        o_ref[...] = y.astype(o_ref.dtype)


def _attn_proj(h, w_qkv, layer, cos, sin, *, col0, ncols, tn, rope_tiles, scale, out_dtype):
    tm = TM
    first_sample_tile = N_PROMPT // tm
    tiles_per_seq = DEC_SEQ // tm
    cb0 = col0 // tn

    def tab_map(n, m):
        return (jnp.maximum(m - first_sample_tile, 0) % tiles_per_seq, 0)

    kern = functools.partial(_attn_proj_kernel, scale=scale, rope_tiles=rope_tiles,
                             first_sample_tile=first_sample_tile)
    return pl.pallas_call(
        kern,
        grid=(ncols // tn, N_TOK // tm),
        in_specs=[pl.BlockSpec((tm, D_MODEL), lambda n, m: (m, 0)),
                  pl.BlockSpec((1, D_MODEL, tn), lambda n, m: (layer, 0, cb0 + n)),
                  pl.BlockSpec((tm, LANES), tab_map),
                  pl.BlockSpec((tm, LANES), tab_map)],
        out_specs=pl.BlockSpec((tm, tn), lambda n, m: (m, n)),
        out_shape=jax.ShapeDtypeStruct((N_TOK, ncols), out_dtype),
        scratch_shapes=[pltpu.VMEM((D_MODEL, tn), BF16)],
        compiler_params=_cparams(("arbitrary", "arbitrary")),
        name="attn_proj",
    )(h, w_qkv, cos, sin)


def _ret_proj_kernel(h_ref, w_ref, cos_ref, sin_ref, o_ref, wb_ref, *, first_sample_tile, tn):
    n = pl.program_id(0)
    m = pl.program_id(1)

    @pl.when(m == 0)
    def _():
        wb_ref[...] = w_ref[0].astype(BF16)

    y = jnp.dot(h_ref[...], wb_ref[...], preferred_element_type=F32)
    k_tiles_lo = RET_HK // tn
    qk_tiles = 2 * RET_HK // tn
    is_k = jnp.logical_and(n >= k_tiles_lo, n < qk_tiles)
    y = y * jnp.where(is_k, RET_DK ** -0.5, 1.0).astype(F32)
    do_rope = jnp.logical_and(m >= first_sample_tile, n < qk_tiles)

    @pl.when(do_rope)
    def _():
        c = cos_ref[...]
        s = sin_ref[...]
        half = RET_DK // 2
        parts = []
        for hh in range(tn // RET_DK):
            x1 = y[:, hh * RET_DK:hh * RET_DK + half]
            x2 = y[:, hh * RET_DK + half:(hh + 1) * RET_DK]
            parts.append(x1 * c - x2 * s)
            parts.append(x1 * s + x2 * c)
        o_ref[...] = jnp.concatenate(parts, axis=1).astype(BF16)

    @pl.when(jnp.logical_not(do_rope))
    def _():
        o_ref[...] = y.astype(BF16)


def _ret_proj(h, w_ret_in, layer, cos, sin):
    tm, tn = TM, 512
    ncols = 2 * RET_HK + 3 * RET_HV
    first_sample_tile = N_PROMPT // tm
    tiles_per_seq = DEC_SEQ // tm

    def tab_map(n, m):
        return (jnp.maximum(m - first_sample_tile, 0) % tiles_per_seq, 0)

    kern = functools.partial(_ret_proj_kernel, first_sample_tile=first_sample_tile, tn=tn)
    return pl.pallas_call(
        kern,
        grid=(ncols // tn, N_TOK // tm),
        in_specs=[pl.BlockSpec((tm, D_MODEL), lambda n, m: (m, 0)),
                  pl.BlockSpec((1, D_MODEL, tn), lambda n, m: (layer, 0, n)),
                  pl.BlockSpec((tm, LANES), tab_map),
                  pl.BlockSpec((tm, LANES), tab_map)],
        out_specs=pl.BlockSpec((tm, tn), lambda n, m: (m, n)),
        out_shape=jax.ShapeDtypeStruct((N_TOK, ncols), BF16),
        scratch_shapes=[pltpu.VMEM((D_MODEL, tn), BF16)],
        compiler_params=_cparams(("arbitrary", "arbitrary")),
        name="ret_proj",
    )(h, w_ret_in, cos, sin)


def _swiglu_kernel(h_ref, wg_ref, wu_ref, o_ref, wgb_ref, wub_ref):
    m = pl.program_id(1)

    @pl.when(m == 0)
    def _():
        wgb_ref[...] = wg_ref[0].astype(BF16)
        wub_ref[...] = wu_ref[0].astype(BF16)

    h = h_ref[...]
    g = jnp.dot(h, wgb_ref[...], preferred_element_type=F32)
    u = jnp.dot(h, wub_ref[...], preferred_element_type=F32)
    o_ref[...] = (g * jax.nn.sigmoid(g) * u).astype(BF16)


def _swiglu_up(h, w_gu, layer):
    tm, tf = TM, 256
    nf = D_FF // tf
    return pl.pallas_call(
        _swiglu_kernel,
        grid=(nf, N_TOK // tm),
        in_specs=[pl.BlockSpec((tm, D_MODEL), lambda f, m: (m, 0)),
                  pl.BlockSpec((1, D_MODEL, tf), lambda f, m: (layer, 0, f)),
                  pl.BlockSpec((1, D_MODEL, tf), lambda f, m: (layer, 0, nf + f))],
        out_specs=pl.BlockSpec((tm, tf), lambda f, m: (m, f)),
        out_shape=jax.ShapeDtypeStruct((N_TOK, D_FF), BF16),
        scratch_shapes=[pltpu.VMEM((D_MODEL, tf), BF16), pltpu.VMEM((D_MODEL, tf), BF16)],
        compiler_params=_cparams(("arbitrary", "arbitrary")),
        name="swiglu_up",
    )(h, w_gu, w_gu)


def _route(logits):
    lane = lax.broadcasted_iota(I32, logits.shape, 1).astype(F32)
    lg = jnp.where(lane < N_EXPERTS, logits, -jnp.inf)
    m1 = jnp.max(lg, axis=-1, keepdims=True)
    i1 = jnp.min(jnp.where(lg == m1, lane, float(LANES)), axis=-1, keepdims=True)
    lg2 = jnp.where(lane == i1, -jnp.inf, lg)
    m2 = jnp.max(lg2, axis=-1, keepdims=True)
    i2 = jnp.min(jnp.where(lg2 == m2, lane, float(LANES)), axis=-1, keepdims=True)
    e2 = jnp.exp(m2 - m1)
    p1 = 1.0 / (1.0 + e2)
    p2 = e2 / (1.0 + e2)
    out = jnp.where(lane == 0, i1, 0.0)
    out = jnp.where(lane == 1, i2, out)
    out = jnp.where(lane == 2, p1, out)
    out = jnp.where(lane == 3, p2, out)
    return out


def _epilogue(x_new, pk_next, rows, fin_ref, final):
    if final:
        ms = jnp.mean(x_new * x_new, axis=-1, keepdims=True)
        return x_new * lax.rsqrt(ms + EPS) * fin_ref[...]
    g_row, sc_row, sh_row = rows
    return _normmod(x_new, pk_next[g_row:g_row + 1], pk_next[sc_row:sc_row + 1], pk_next[sh_row:sh_row + 1])


def _down_kernel(*refs, n_a, first_sample_tile, gate_row, next_rows, router):
    a_refs = refs[:n_a]
    w_ref, x_ref, pk_ref, pkn_ref = refs[n_a:n_a + 4]
    pos = n_a + 4
    if router:
        wr_ref = refs[pos]
        pos += 1
        xo_ref, h_ref, hf_ref, rt_ref, wb_ref = refs[pos:pos + 5]
    else:
        xo_ref, h_ref, wb_ref = refs[pos:pos + 3]
    m = pl.program_id(0)

    @pl.when(m == 0)
    def _():
        wb_ref[...] = w_ref[0].astype(BF16)

    def finish(a):
        y = jnp.dot(a, wb_ref[...], preferred_element_type=F32)
        pk = pk_ref[0, 0]
        x_new = x_ref[...] + pk[gate_row:gate_row + 1] * y
        xo_ref[...] = x_new
        hn = _epilogue(x_new, pkn_ref[0, 0], next_rows, None, False)
        h_ref[...] = hn.astype(BF16)
        if router:
            hf_ref[...] = hn
            logits = jnp.dot(hn, wr_ref[0], preferred_element_type=F32, precision=lax.Precision.HIGHEST)
            rt_ref[...] = _route(logits)

    if n_a == 1:
        finish(a_refs[0][...])
    else:
        @pl.when(m < first_sample_tile)
        def _():
            finish(a_refs[0][...])

        @pl.when(m >= first_sample_tile)
        def _():
            finish(a_refs[1][...])


def _down(a_list, w, layer, x, pack, pack_layer, gate_row, next_layer, next_rows, w_router=None, router_layer=0):
    tm = TM_DOWN
    kd = w.shape[1]
    n_a = len(a_list)
    fst = N_PROMPT // tm
    router = w_router is not None
    if n_a == 1:
        a_specs = [pl.BlockSpec((tm, kd), lambda m: (m, 0))]
    else:
        a_specs = [pl.BlockSpec((tm, kd), lambda m: (jnp.minimum(m, fst - 1), 0)),
                   pl.BlockSpec((tm, kd), lambda m: (jnp.maximum(m - fst, 0), 0))]
    in_specs = a_specs + [
        pl.BlockSpec((1, kd, D_MODEL), lambda m: (layer, 0, 0)),
        pl.BlockSpec((tm, D_MODEL), lambda m: (m, 0)),
        pl.BlockSpec((1, 1, 8, D_MODEL), lambda m: (pack_layer, (m * tm) // GROUP_TOKENS, 0, 0)),
        pl.BlockSpec((1, 1, 8, D_MODEL), lambda m: (next_layer, (m * tm) // GROUP_TOKENS, 0, 0)),
    ]
    args = list(a_list) + [w, x, pack, pack]
    out_specs = [pl.BlockSpec((tm, D_MODEL), lambda m: (m, 0)),
                 pl.BlockSpec((tm, D_MODEL), lambda m: (m, 0))]
    out_shape = [jax.ShapeDtypeStruct((N_TOK, D_MODEL), F32),
                 jax.ShapeDtypeStruct((N_TOK, D_MODEL), BF16)]
    if router:
        in_specs.append(pl.BlockSpec((1, D_MODEL, LANES), lambda m: (router_layer, 0, 0)))
        args.append(w_router)
        out_specs += [pl.BlockSpec((tm, D_MODEL), lambda m: (m, 0)),
                      pl.BlockSpec((tm, LANES), lambda m: (m, 0))]
        out_shape += [jax.ShapeDtypeStruct((N_TOK, D_MODEL), F32),
                      jax.ShapeDtypeStruct((N_TOK, LANES), F32)]
    kern = functools.partial(_down_kernel, n_a=n_a, first_sample_tile=fst, gate_row=gate_row,
                             next_rows=next_rows, router=router)
    return pl.pallas_call(
        kern,
        grid=(N_TOK // tm,),
        in_specs=in_specs,
        out_specs=out_specs,
        out_shape=out_shape,
        scratch_shapes=[pltpu.VMEM((kd, D_MODEL), BF16)],
        compiler_params=_cparams(("arbitrary",)),
        name="down_proj",
    )(*args)


def _softmax_pv(s, sink_col, v):
    m = jnp.maximum(jnp.max(s, axis=-1, keepdims=True), sink_col)
    p = jnp.exp(s - m)
    denom = jnp.sum(p, axis=-1, keepdims=True) + jnp.exp(sink_col - m)
    o = jnp.dot(p.astype(BF16), v, preferred_element_type=F32)
    return o / denom


def _sink_column(sink_ref, layer, kvh, rows):
    cols = [jnp.full((rows, 1), sink_ref[layer * N_HEADS + kvh * GROUP + g], F32) for g in range(GROUP)]
    return jnp.concatenate(cols, axis=0)


def _ctx_attn_kernel(sink_ref, q_ref, kv_ref, o_ref, *, layer):
    kvw = N_KV_HEADS * HEAD_DIM
    for kvh in range(N_KV_HEADS):
        k = kv_ref[:, kvh * HEAD_DIM:(kvh + 1) * HEAD_DIM].astype(BF16)
        v = kv_ref[:, kvw + kvh * HEAD_DIM:kvw + (kvh + 1) * HEAD_DIM].astype(BF16)
        q4 = jnp.concatenate(
            [q_ref[:, (kvh * GROUP + g) * HEAD_DIM:(kvh * GROUP + g + 1) * HEAD_DIM] for g in range(GROUP)], axis=0)
        s = lax.dot_general(q4, k, (((1,), (1,)), ((), ())), preferred_element_type=F32)
        o = _softmax_pv(s, _sink_column(sink_ref, layer, kvh, SEQ), v)
        for g in range(GROUP):
            hd = kvh * GROUP + g
            o_ref[:, hd * HEAD_DIM:(hd + 1) * HEAD_DIM] = o[g * SEQ:(g + 1) * SEQ].astype(BF16)


def _ctx_attention(q, kv, sink_flat, layer):
    kern = functools.partial(_ctx_attn_kernel, layer=layer)
    return pl.pallas_call(
        kern,
        grid_spec=pltpu.PrefetchScalarGridSpec(
            num_scalar_prefetch=1,
            grid=(BATCH,),
            in_specs=[pl.BlockSpec((SEQ, D_MODEL), lambda b, s: (b, 0)),
                      pl.BlockSpec((SEQ, 2 * N_KV_HEADS * HEAD_DIM), lambda b, s: (b, 0))],
            out_specs=pl.BlockSpec((SEQ, D_MODEL), lambda b, s: (b, 0))),
        out_shape=jax.ShapeDtypeStruct((N_PROMPT, D_MODEL), BF16),
        compiler_params=_cparams(("arbitrary",)),
        name="ctx_attention",
    )(sink_flat, q, kv)


def _lat_attn_kernel(sink_ref, q_ref, kvp_ref, kvc_ref, kvn_ref, ck_ref, cv_ref, o_ref, *, layer):
    blk = pl.program_id(1)
    kvw = N_KV_HEADS * HEAD_DIM
    rows = GROUP * BLOCK
    n_lat = 3 * BLOCK
    r = lax.broadcasted_iota(I32, (rows, n_lat), 0) % BLOCK
    j = lax.broadcasted_iota(I32, (rows, n_lat), 1)
    lo = jnp.maximum(r, BLOCK - BLOCK * blk)
    hi = jnp.minimum(r + 2 * WINDOW, DEC_SEQ + BLOCK - 1 - BLOCK * blk)
    lat_mask = jnp.logical_and(j >= lo, j <= hi)
    for kvh in range(N_KV_HEADS):
        ksl = slice(kvh * HEAD_DIM, (kvh + 1) * HEAD_DIM)
        vsl = slice(kvw + kvh * HEAD_DIM, kvw + (kvh + 1) * HEAD_DIM)
        k_ctx = ck_ref[0, 0, :, ksl].astype(BF16)
        v_ctx = cv_ref[0, 0, :, ksl].astype(BF16)
        k_lat = jnp.concatenate([kvp_ref[:, ksl], kvc_ref[:, ksl], kvn_ref[:, ksl]], axis=0).astype(BF16)
        v_lat = jnp.concatenate([kvp_ref[:, vsl], kvc_ref[:, vsl], kvn_ref[:, vsl]], axis=0).astype(BF16)
        q4 = jnp.concatenate(
            [q_ref[:, (kvh * GROUP + g) * HEAD_DIM:(kvh * GROUP + g + 1) * HEAD_DIM] for g in range(GROUP)], axis=0)
        dn = (((1,), (1,)), ((), ()))
        s_ctx = lax.dot_general(q4, k_ctx, dn, preferred_element_type=F32)
        s_lat = lax.dot_general(q4, k_lat, dn, preferred_element_type=F32)
        s_lat = jnp.where(lat_mask, s_lat, NEG)
        sink_col = _sink_column(sink_ref, layer, kvh, BLOCK)
        m = jnp.maximum(jnp.maximum(jnp.max(s_ctx, axis=-1, keepdims=True),
                                    jnp.max(s_lat, axis=-1, keepdims=True)), sink_col)
        p_ctx = jnp.exp(s_ctx - m)
        p_lat = jnp.exp(s_lat - m)
        denom = (jnp.sum(p_ctx, axis=-1, keepdims=True) + jnp.sum(p_lat, axis=-1, keepdims=True)
                 + jnp.exp(sink_col - m))
        o = (jnp.dot(p_ctx.astype(BF16), v_ctx, preferred_element_type=F32)
             + jnp.dot(p_lat.astype(BF16), v_lat, preferred_element_type=F32)) / denom
        for g in range(GROUP):
            hd = kvh * GROUP + g
            o_ref[:, hd * HEAD_DIM:(hd + 1) * HEAD_DIM] = o[g * BLOCK:(g + 1) * BLOCK].astype(BF16)


def _lat_attention(q, kv, cache_k4, cache_v4, sink_flat, layer):
    nb = DEC_SEQ // BLOCK
    base = N_PROMPT // BLOCK
    kvc = 2 * N_KV_HEADS * HEAD_DIM

    def row(b, i):
        return base + b * nb + i

    kern = functools.partial(_lat_attn_kernel, layer=layer)
    return pl.pallas_call(
        kern,
        grid_spec=pltpu.PrefetchScalarGridSpec(
            num_scalar_prefetch=1,
            grid=(DEC_BATCH, nb),
            in_specs=[pl.BlockSpec((BLOCK, D_MODEL), lambda b, i, s: (row(b, i), 0)),
                      pl.BlockSpec((BLOCK, kvc), lambda b, i, s: (row(b, jnp.maximum(i - 1, 0)), 0)),
                      pl.BlockSpec((BLOCK, kvc), lambda b, i, s: (row(b, i), 0)),
                      pl.BlockSpec((BLOCK, kvc), lambda b, i, s: (row(b, jnp.minimum(i + 1, nb - 1)), 0)),
                      pl.BlockSpec((1, 1, PAST_LEN, N_KV_HEADS * HEAD_DIM), lambda b, i, s: (b, layer, 0, 0)),
                      pl.BlockSpec((1, 1, PAST_LEN, N_KV_HEADS * HEAD_DIM), lambda b, i, s: (b, layer, 0, 0))],
            out_specs=pl.BlockSpec((BLOCK, D_MODEL), lambda b, i, s: (b * nb + i, 0))),
        out_shape=jax.ShapeDtypeStruct((N_SAMPLE, D_MODEL), BF16),
        compiler_params=_cparams(("arbitrary", "arbitrary")),
        name="lat_attention",
    )(sink_flat, q, kv, kv, kv, cache_k4, cache_v4)


N_CHUNKS = N_TOK // BLOCK
PROMPT_CHUNKS = N_PROMPT // BLOCK
CHUNKS_PER_PROMPT = SEQ // BLOCK
CHUNKS_PER_SAMPLE = DEC_SEQ // BLOCK


def _ret_kernel(lg_ref, q_ref, k_ref, v_ref, gate_ref, s0_ref, *rest, backward, layer):
    if backward:
        part_ref, o_ref, sout_ref, state_ref = rest
    else:
        o_ref, sout_ref, state_ref = rest
    step = pl.program_id(0)
    c = (N_CHUNKS - 1 - step) if backward else step
    in_prompt = c < PROMPT_CHUNKS
    if backward:
        seq_start = jnp.where(in_prompt, c % CHUNKS_PER_PROMPT == CHUNKS_PER_PROMPT - 1,
                              (c - PROMPT_CHUNKS) % CHUNKS_PER_SAMPLE == CHUNKS_PER_SAMPLE - 1)
        seq_end = jnp.logical_and(in_prompt, c % CHUNKS_PER_PROMPT == 0)
    else:
        seq_start = jnp.where(in_prompt, c % CHUNKS_PER_PROMPT == 0,
                              (c - PROMPT_CHUNKS) % CHUNKS_PER_SAMPLE == 0)
        seq_end = jnp.logical_and(in_prompt, c % CHUNKS_PER_PROMPT == CHUNKS_PER_PROMPT - 1)

    @pl.when(jnp.logical_and(seq_start, in_prompt))
    def _():
        state_ref[...] = jnp.zeros_like(state_ref)

    @pl.when(jnp.logical_and(seq_start, jnp.logical_not(in_prompt)))
    def _():
        state_ref[...] = s0_ref[0, 0]

    ii = lax.broadcasted_iota(I32, (BLOCK, BLOCK), 0).astype(F32)
    jj = lax.broadcasted_iota(I32, (BLOCK, BLOCK), 1).astype(F32)
    dist = (jj - ii) if backward else (ii - jj)
    ti = lax.broadcasted_iota(I32, (BLOCK, 1), 0).astype(F32)
    q_pow = (BLOCK - ti) if backward else (ti + 1.0)
    k_pow = ti if backward else (BLOCK - 1.0 - ti)

    for h in range(RET_HEADS):
        lg = lg_ref[(layer * 2 + (1 if backward else 0)) * RET_HEADS + h]
        intra = jnp.where(dist >= 0, jnp.exp(lg * jnp.maximum(dist, 0.0)), 0.0)
        q_dec = jnp.exp(lg * q_pow)
        k_dec = jnp.exp(lg * k_pow)
        c_dec = jnp.exp(lg * BLOCK)
        q = q_ref[:, h * RET_DK:(h + 1) * RET_DK]
        k = k_ref[:, h * RET_DK:(h + 1) * RET_DK]
        v = v_ref[:, h * RET_DV:(h + 1) * RET_DV]
        s = state_ref[h]
        a = lax.dot_general(q, k, (((1,), (1,)), ((), ())), preferred_element_type=F32) * intra
        o = (jnp.dot(a.astype(BF16), v, preferred_element_type=F32)
             + q_dec * jnp.dot(q, s.astype(BF16), preferred_element_type=F32))
        kd = (k.astype(F32) * k_dec).astype(BF16)
        s_new = c_dec * s + lax.dot_general(kd, v, (((0,), (0,)), ((), ())), preferred_element_type=F32)
        state_ref[h] = s_new
        gate = gate_ref[:, h * RET_DV:(h + 1) * RET_DV].astype(F32)
        on = o * lax.rsqrt(jnp.mean(o * o, axis=-1, keepdims=True) + EPS)
        res = on * (gate * jax.nn.sigmoid(gate))
        if backward:
            res = res + part_ref[:, h * RET_DV:(h + 1) * RET_DV].astype(F32)
        o_ref[:, h * RET_DV:(h + 1) * RET_DV] = res.astype(BF16)

    @pl.when(seq_end)
    def _():
        sout_ref[0] = state_ref[...]


def _retention_pass(proj, s0, log_gamma_flat, layer, backward, partial=None):
    def chunk(i):
        return (N_CHUNKS - 1 - i) if backward else i

    def s0_map(i, lg):
        c = chunk(i)
        return (jnp.clip((c - PROMPT_CHUNKS) // CHUNKS_PER_SAMPLE, 0, DEC_BATCH - 1), layer, 0, 0, 0)

    def sout_map(i, lg):
        c = chunk(i)
        return (jnp.clip(c // CHUNKS_PER_PROMPT, 0, BATCH - 1), 0, 0, 0)

    gate_block = 3 if backward else 2
    in_specs = [pl.BlockSpec((BLOCK, RET_HK), lambda i, lg: (chunk(i), 0)),
                pl.BlockSpec((BLOCK, RET_HK), lambda i, lg: (chunk(i), 1)),
                pl.BlockSpec((BLOCK, RET_HV), lambda i, lg: (chunk(i), 1)),
                pl.BlockSpec((BLOCK, RET_HV), lambda i, lg: (chunk(i), gate_block)),
                pl.BlockSpec((1, 1, RET_HEADS, RET_DK, RET_DV), s0_map)]
    args = [proj, proj, proj, proj, s0]
    if backward:
        in_specs.append(pl.BlockSpec((BLOCK, RET_HV), lambda i, lg: (chunk(i), 0)))
        args.append(partial)
    kern = functools.partial(_ret_kernel, backward=backward, layer=layer)
    return pl.pallas_call(
        kern,
        grid_spec=pltpu.PrefetchScalarGridSpec(
            num_scalar_prefetch=1,
            grid=(N_CHUNKS,),
            in_specs=in_specs,
            out_specs=[pl.BlockSpec((BLOCK, RET_HV), lambda i, lg: (chunk(i), 0)),
                       pl.BlockSpec((1, RET_HEADS, RET_DK, RET_DV), sout_map)],
            scratch_shapes=[pltpu.VMEM((RET_HEADS, RET_DK, RET_DV), F32)]),
        out_shape=[jax.ShapeDtypeStruct((N_TOK, RET_HV), BF16),
                   jax.ShapeDtypeStruct((BATCH, RET_HEADS, RET_DK, RET_DV), F32)],
        compiler_params=_cparams(("arbitrary",)),
        name="retention_bwd" if backward else "retention_fwd",
    )(log_gamma_flat, *args)


def _gather_kernel(tok_ref, h_hbm, o_hbm, sem):
    base = pl.program_id(0) * GATHER_ROWS

    def row_copy(r):
        return pltpu.make_async_copy(h_hbm.at[pl.ds(tok_ref[base + r], 1)], o_hbm.at[pl.ds(base + r, 1)], sem)

    def start(r, carry):
        row_copy(r).start()
        return carry

    def wait(r, carry):
        row_copy(r).wait()
        return carry

    lax.fori_loop(0, GATHER_ROWS, start, 0)
    lax.fori_loop(0, GATHER_ROWS, wait, 0)


def _gather_rows(tok_of_slot, h):
    return pl.pallas_call(
        _gather_kernel,
        grid_spec=pltpu.PrefetchScalarGridSpec(
            num_scalar_prefetch=1,
            grid=(MOE_SLOTS // GATHER_ROWS,),
            in_specs=[pl.BlockSpec(memory_space=pl.ANY)],
            out_specs=pl.BlockSpec(memory_space=pl.ANY),
            scratch_shapes=[pltpu.SemaphoreType.DMA(())]),
        out_shape=jax.ShapeDtypeStruct((MOE_SLOTS, D_MODEL), F32),
        compiler_params=_cparams(("arbitrary",)),
        name="moe_gather",
    )(tok_of_slot, h)


def _moe_up_kernel(te_ref, nv_ref, h_ref, wg_ref, wu_ref, o_ref, wgb_ref, wub_ref):
    m = pl.program_id(1)
    new_w = jnp.logical_or(m == 0, te_ref[m] != te_ref[jnp.maximum(m - 1, 0)])

    @pl.when(new_w)
    def _():
        wgb_ref[...] = wg_ref[0, 0].astype(BF16)
        wub_ref[...] = wu_ref[0, 0].astype(BF16)

    @pl.when(m < nv_ref[0])
    def _():
        h = h_ref[...].astype(BF16)
        g = jnp.dot(h, wgb_ref[...], preferred_element_type=F32)
        u = jnp.dot(h, wub_ref[...], preferred_element_type=F32)
        o_ref[...] = (g * jax.nn.sigmoid(g) * u).astype(BF16)

    @pl.when(m >= nv_ref[0])
    def _():
        o_ref[...] = jnp.zeros_like(o_ref)


def _moe_up(tile_expert, n_valid, hs, w_gu, layer):
    tf = 512
    nf = D_FF_EXPERT // tf
    return pl.pallas_call(
        _moe_up_kernel,
        grid_spec=pltpu.PrefetchScalarGridSpec(
            num_scalar_prefetch=2,
            grid=(nf, MOE_NT),
            in_specs=[pl.BlockSpec((MOE_TILE, D_MODEL), lambda f, m, te, nv: (m, 0)),
                      pl.BlockSpec((1, 1, D_MODEL, tf), lambda f, m, te, nv: (layer, te[m], 0, f)),
                      pl.BlockSpec((1, 1, D_MODEL, tf), lambda f, m, te, nv: (layer, te[m], 0, nf + f))],
            out_specs=pl.BlockSpec((MOE_TILE, tf), lambda f, m, te, nv: (m, f)),
            scratch_shapes=[pltpu.VMEM((D_MODEL, tf), BF16), pltpu.VMEM((D_MODEL, tf), BF16)]),
        out_shape=jax.ShapeDtypeStruct((MOE_SLOTS, D_FF_EXPERT), BF16),
        compiler_params=_cparams(("arbitrary", "arbitrary")),
        name="moe_up",
    )(tile_expert, n_valid, hs, w_gu, w_gu)


def _moe_down_kernel(te_ref, nv_ref, a_ref, w_ref, o_ref, wb_ref):
    m = pl.program_id(1)
    new_w = jnp.logical_or(m == 0, te_ref[m] != te_ref[jnp.maximum(m - 1, 0)])

    @pl.when(new_w)
    def _():
        wb_ref[...] = w_ref[0, 0].astype(BF16)

    @pl.when(m < nv_ref[0])
    def _():
        o_ref[...] = jnp.dot(a_ref[...], wb_ref[...], preferred_element_type=F32)

    @pl.when(m >= nv_ref[0])
    def _():
        o_ref[...] = jnp.zeros_like(o_ref)


def _moe_down(tile_expert, n_valid, act, w_down, layer):
    tn = 512
    return pl.pallas_call(
        _moe_down_kernel,
        grid_spec=pltpu.PrefetchScalarGridSpec(
            num_scalar_prefetch=2,
            grid=(D_MODEL // tn, MOE_NT),
            in_specs=[pl.BlockSpec((MOE_TILE, D_FF_EXPERT), lambda n, m, te, nv: (m, 0)),
                      pl.BlockSpec((1, 1, D_FF_EXPERT, tn), lambda n, m, te, nv: (layer, te[m], 0, n))],
            out_specs=pl.BlockSpec((MOE_TILE, tn), lambda n, m, te, nv: (m, n)),
            scratch_shapes=[pltpu.VMEM((D_FF_EXPERT, tn), BF16)]),
        out_shape=jax.ShapeDtypeStruct((MOE_SLOTS, D_MODEL), F32),
        compiler_params=_cparams(("arbitrary", "arbitrary")),
        name="moe_down",
    )(tile_expert, n_valid, act, w_down)


def _combine_kernel(slot_ref, y_hbm, x_ref, rt_ref, pk_ref, pkn_ref, fin_ref, *rest, tok0, final):
    if final:
        out_ref, ybuf, sem = rest
    else:
        xo_ref, h_ref, ybuf, sem = rest
    tc = TC_COMBINE
    base = (tok0 + pl.program_id(0) * tc) * TOP_K

    def row_copy(r, k):
        return pltpu.make_async_copy(y_hbm.at[pl.ds(slot_ref[base + r * TOP_K + k], 1)],
                                     ybuf.at[k, pl.ds(r, 1)], sem)

    def start(r, carry):
        row_copy(r, 0).start()
        row_copy(r, 1).start()
        return carry

    def wait(r, carry):
        row_copy(r, 0).wait()
        row_copy(r, 1).wait()
        return carry

    lax.fori_loop(0, tc, start, 0)
    lax.fori_loop(0, tc, wait, 0)

    rt = rt_ref[...]
    moe = rt[:, 2:3] * ybuf[0] + rt[:, 3:4] * ybuf[1]
    pk = pk_ref[0, 0]
    x_new = x_ref[...] + pk[R_GATE_FFN:R_GATE_FFN + 1] * moe
    if final:
        out_ref[...] = _epilogue(x_new, None, None, fin_ref, True)
    else:
        xo_ref[...] = x_new
        h_ref[...] = _epilogue(x_new, pkn_ref[0, 0], (R_G_MIX, R_SCALE_MIX, R_SHIFT_MIX), None, False).astype(BF16)


def _combine(slot, y, x, route, pack, layer, next_layer, final_norm, tok0, n_rows, final):
    tc = TC_COMBINE
    b0 = tok0 // tc

    def tmap(m, s):
        return (b0 + m, 0)

    def pmap(l):
        return lambda m, s: (l, (tok0 + m * tc) // GROUP_TOKENS, 0, 0)

    if final:
        out_specs = pl.BlockSpec((tc, D_MODEL), lambda m, s: (m, 0))
        out_shape = jax.ShapeDtypeStruct((n_rows, D_MODEL), F32)
    else:
        out_specs = [pl.BlockSpec((tc, D_MODEL), lambda m, s: (m, 0)),
                     pl.BlockSpec((tc, D_MODEL), lambda m, s: (m, 0))]
        out_shape = [jax.ShapeDtypeStruct((n_rows, D_MODEL), F32),
                     jax.ShapeDtypeStruct((n_rows, D_MODEL), BF16)]
    kern = functools.partial(_combine_kernel, tok0=tok0, final=final)
    return pl.pallas_call(
        kern,
        grid_spec=pltpu.PrefetchScalarGridSpec(
            num_scalar_prefetch=1,
            grid=(n_rows // tc,),
            in_specs=[pl.BlockSpec(memory_space=pl.ANY),
                      pl.BlockSpec((tc, D_MODEL), tmap),
                      pl.BlockSpec((tc, LANES), tmap),
                      pl.BlockSpec((1, 1, 8, D_MODEL), pmap(layer)),
                      pl.BlockSpec((1, 1, 8, D_MODEL), pmap(next_layer)),
                      pl.BlockSpec((1, D_MODEL), lambda m, s: (0, 0))],
            out_specs=out_specs,
            scratch_shapes=[pltpu.VMEM((TOP_K, tc, D_MODEL), F32), pltpu.SemaphoreType.DMA(())]),
        out_shape=out_shape,
        compiler_params=_cparams(("arbitrary",)),
        name="moe_combine",
    )(slot, y, x, route, pack, pack, final_norm.reshape(1, D_MODEL))


def _routing_tables(route):
    e_flat = route[:, :TOP_K].astype(I32).reshape(-1)
    onehot = (e_flat[:, None] == jnp.arange(N_EXPERTS, dtype=I32)[None, :]).astype(I32)
    csum = jnp.cumsum(onehot, axis=0)
    rank = jnp.sum(onehot * (csum - 1), axis=1)
    count = csum[-1]
    padded = ((count + MOE_TILE - 1) // MOE_TILE) * MOE_TILE
    pend = jnp.cumsum(padded)
    poff = pend - padded
    slot = jnp.sum(onehot * poff[None, :], axis=1) + rank
    n_valid = pend[-1] // MOE_TILE
    tile_start = jnp.arange(MOE_NT, dtype=I32) * MOE_TILE
    te_raw = jnp.minimum(jnp.sum((tile_start[:, None] >= pend[None, :]).astype(I32), axis=1), N_EXPERTS - 1)
    last_e = jnp.max(jnp.where(tile_start < pend[-1], te_raw, 0))
    tile_expert = jnp.minimum(te_raw, last_e)
    tok = jnp.arange(TOP_K * N_TOK, dtype=I32) // TOP_K
    tok_of_slot = jnp.zeros((MOE_SLOTS,), I32).at[slot].set(tok)
    return slot.astype(I32), tok_of_slot, tile_expert.astype(I32), n_valid.astype(I32).reshape(1)


def _rope_tables(dh):
    nf = dh // 4
    t = jnp.arange(DEC_SEQ)
    freqs = ROPE_BASE ** (-jnp.arange(nf, dtype=F32) / nf)
    row = (t // GRID_W).astype(F32)
    col = (t % GRID_W).astype(F32)
    ang = jnp.concatenate([row[:, None] * freqs, col[:, None] * freqs], axis=-1)
    return jnp.cos(ang), jnp.sin(ang)


def kernel(x_prompt, x_sample, cache_k, cache_v, state_fwd, state_bwd, c, c_ctx, norm_mix, norm_ffn, w_mod, b_mod,
           w_qkv, w_attn_o, attn_sink, w_ret_in, ret_decay, w_ret_out, w_ffn_gu, w_ffn_down, w_router, w_exp_gu,
           w_exp_down, final_norm):
    x = jnp.concatenate([x_prompt.reshape(N_PROMPT, D_MODEL), x_sample.reshape(N_SAMPLE, D_MODEL)], axis=0)

    cond8 = jnp.concatenate([c_ctx[None, :], c, jnp.zeros((8 - 1 - DEC_BATCH, D_MODEL), F32)], axis=0)
    mods = _modulations(cond8, w_mod, b_mod)
    m3 = mods[:, :N_GROUPS].reshape(DEPTH, N_GROUPS, 6, D_MODEL)
    pack = jnp.concatenate([
        m3,
        jnp.broadcast_to(norm_mix[:, None, None, :], (DEPTH, N_GROUPS, 1, D_MODEL)),
        jnp.broadcast_to(norm_ffn[:, None, None, :], (DEPTH, N_GROUPS, 1, D_MODEL))], axis=2)

    cos_a, sin_a = _rope_tables(HEAD_DIM)
    cos_attn = jnp.tile(cos_a, (1, LANES // (HEAD_DIM // 2)))
    sin_attn = jnp.tile(jnp.concatenate([-sin_a, sin_a], axis=1), (1, LANES // HEAD_DIM))
    cos_ret, sin_ret = _rope_tables(RET_DK)

    kvw = N_KV_HEADS * HEAD_DIM
    cache_k4 = cache_k.reshape(DEC_BATCH, N_EVEN, PAST_LEN, kvw)
    cache_v4 = cache_v.reshape(DEC_BATCH, N_EVEN, PAST_LEN, kvw)
    sink_flat = attn_sink.reshape(-1)
    log_gamma = jax.nn.log_sigmoid(ret_decay.astype(F32)).reshape(-1)
    w_router_pad = jnp.pad(w_router, ((0, 0), (0, 0), (0, LANES - N_EXPERTS)))

    mix_rows = (R_G_MIX, R_SCALE_MIX, R_SHIFT_MIX)
    ffn_rows = (R_G_FFN, R_SCALE_FFN, R_SHIFT_FFN)

    new_k, new_v, new_sf, new_sb = [], [], [], []
    h = _pre(x, pack)
    y_prompt = y_sample = None
    for i in range(DEPTH):
        j = i // 2
        if i % 2 == 0:
            q = _attn_proj(h, w_qkv, j, cos_attn, sin_attn, col0=0, ncols=N_HEADS * HEAD_DIM, tn=512,
                           rope_tiles=2, scale=HEAD_DIM ** -0.5, out_dtype=BF16)
            kv = _attn_proj(h, w_qkv, j, cos_attn, sin_attn, col0=N_HEADS * HEAD_DIM, ncols=2 * kvw, tn=kvw,
                            rope_tiles=1, scale=1.0, out_dtype=F32)
            new_k.append(kv[:N_PROMPT, :kvw].reshape(BATCH, SEQ, N_KV_HEADS, HEAD_DIM))
            new_v.append(kv[:N_PROMPT, kvw:].reshape(BATCH, SEQ, N_KV_HEADS, HEAD_DIM))
            o_ctx = _ctx_attention(q, kv, sink_flat, j)
            o_lat = _lat_attention(q, kv, cache_k4, cache_v4, sink_flat, j)
            x, h = _down([o_ctx, o_lat], w_attn_o, j, x, pack, i, R_GATE_MIX, i, ffn_rows)
            act = _swiglu_up(h, w_ffn_gu, j)
            x, h = _down([act], w_ffn_down, j, x, pack, i, R_GATE_FFN, i + 1, mix_rows)
        else:
            proj = _ret_proj(h, w_ret_in, j, cos_ret, sin_ret)
            part, sf = _retention_pass(proj, state_fwd, log_gamma, j, False)
            o_ret, sb = _retention_pass(proj, state_bwd, log_gamma, j, True, partial=part)
            new_sf.append(sf)
            new_sb.append(sb)
            x, h, hf, route = _down([o_ret], w_ret_out, j, x, pack, i, R_GATE_MIX, i, ffn_rows,
                                    w_router=w_router_pad, router_layer=j)
            slot, tok_of_slot, tile_expert, n_valid = _routing_tables(route)
            hs = _gather_rows(tok_of_slot, hf)
            act = _moe_up(tile_expert, n_valid, hs, w_exp_gu, j)
            ys = _moe_down(tile_expert, n_valid, act, w_exp_down, j)
            if i == DEPTH - 1:
                y_prompt = _combine(slot, ys, x, route, pack, i, i, final_norm, 0, N_PROMPT, True)
                y_sample = _combine(slot, ys, x, route, pack, i, i, final_norm, N_PROMPT, N_SAMPLE, True)
            else:
                x, h = _combine(slot, ys, x, route, pack, i, i + 1, final_norm, 0, N_TOK, False)

    return (y_prompt.reshape(BATCH, SEQ, D_MODEL), y_sample.reshape(DEC_BATCH, DEC_SEQ, D_MODEL),
            jnp.stack(new_k, axis=1), jnp.stack(new_v, axis=1), jnp.stack(new_sf, axis=1), jnp.stack(new_sb, axis=1))
```

```python
import functools

import jax
import jax.numpy as jnp
from jax import lax
from jax.experimental import pallas as pl
from jax.experimental.pallas import tpu as pltpu

F32 = jnp.float32
BF16 = jnp.bfloat16
I32 = jnp.int32

D_MODEL = 1024
BATCH = 16
SEQ = 256
DEPTH = 4
DEC_BATCH = 2
DEC_SEQ = 4096
PAST_LEN = 256
GRID_W = 64
BLOCK = 128
EPS = 1e-6
N_HEADS = 16
N_KV_HEADS = 4
HEAD_DIM = 64
GROUP = N_HEADS // N_KV_HEADS
WINDOW = 128
ROPE_BASE = 10000.0
RET_HEADS = 4
RET_DK = 256
RET_DV = 512
RET_HK = RET_HEADS * RET_DK
RET_HV = RET_HEADS * RET_DV
D_FF = 2816
N_EXPERTS = 8
TOP_K = 2
D_FF_EXPERT = 3584
N_EVEN = 2
N_ODD = 2
NEG = -1e30

GROUP_TOKENS = 4096
N_PROMPT = BATCH * SEQ
N_SAMPLE = DEC_BATCH * DEC_SEQ
N_TOK = N_PROMPT + N_SAMPLE
N_GROUPS = N_TOK // GROUP_TOKENS

R_SHIFT_MIX, R_SCALE_MIX, R_GATE_MIX, R_SHIFT_FFN, R_SCALE_FFN, R_GATE_FFN, R_G_MIX, R_G_FFN = range(8)

VMEM_LIMIT_BYTES = 56 * 1024 * 1024
LANES = 128

TM = 2048
UP_CHUNK = 512
TM_DOWN = 512
DOWN_CHUNK = 128
MOE_TILE = 512
MOE_CHUNK = 256
MOE_SLOTS = TOP_K * N_TOK + N_EXPERTS * MOE_TILE
MOE_NT = MOE_SLOTS // MOE_TILE
TC_COMBINE = 256
DISPATCH_ROWS = 256


def _cparams(sem):
    return pltpu.CompilerParams(dimension_semantics=sem, vmem_limit_bytes=VMEM_LIMIT_BYTES)


def _normmod(x, g, scale, shift):
    ms = jnp.mean(x * x, axis=-1, keepdims=True)
    return (x * lax.rsqrt(ms + EPS) * g) * (1.0 + scale) + shift


def _mod_kernel(cond_ref, w_ref, b_ref, o_ref):
    c = cond_ref[...]
    s = c * jax.nn.sigmoid(c)
    o_ref[0] = jnp.dot(s.astype(BF16), w_ref[0].astype(BF16), preferred_element_type=F32) + b_ref[0]


def _modulations(cond8, w_mod, b_mod):
    tn = 2048
    n6 = 6 * D_MODEL
    return pl.pallas_call(
        _mod_kernel,
        grid=(DEPTH, n6 // tn),
        in_specs=[pl.BlockSpec((8, D_MODEL), lambda l, n: (0, 0)),
                  pl.BlockSpec((1, D_MODEL, tn), lambda l, n: (l, 0, n)),
                  pl.BlockSpec((1, 1, tn), lambda l, n: (l, 0, n))],
        out_specs=pl.BlockSpec((1, 8, tn), lambda l, n: (l, 0, n)),
        out_shape=jax.ShapeDtypeStruct((DEPTH, 8, n6), F32),
        compiler_params=_cparams(("arbitrary", "arbitrary")),
        name="modulations",
    )(cond8, w_mod, b_mod.reshape(DEPTH, 1, n6))


def _pre_kernel(x_ref, pk_ref, h_ref):
    pk = pk_ref[0, 0]
    h = _normmod(x_ref[...], pk[R_G_MIX:R_G_MIX + 1], pk[R_SCALE_MIX:R_SCALE_MIX + 1],
                 pk[R_SHIFT_MIX:R_SHIFT_MIX + 1])
    h_ref[...] = h.astype(BF16)


def _pre(x, pack):
    tm = TM
    return pl.pallas_call(
        _pre_kernel,
        grid=(N_TOK // tm,),
        in_specs=[pl.BlockSpec((tm, D_MODEL), lambda m: (m, 0)),
                  pl.BlockSpec((1, 1, 8, D_MODEL), lambda m: (0, (m * tm) // GROUP_TOKENS, 0, 0))],
        out_specs=pl.BlockSpec((tm, D_MODEL), lambda m: (m, 0)),
        out_shape=jax.ShapeDtypeStruct((N_TOK, D_MODEL), BF16),
        compiler_params=_cparams(("arbitrary",)),
        name="pre_norm",
    )(x, pack)


def _row_chunks(n_rows):
    return [pl.ds(c * UP_CHUNK, UP_CHUNK) for c in range(n_rows // UP_CHUNK)]


def _rope64(y, cos, sin_signed):
    width = y.shape[-1]
    lane = lax.broadcasted_iota(I32, y.shape, 1)
    first = (lane % HEAD_DIM) < (HEAD_DIM // 2)
    swapped = jnp.where(first, pltpu.roll(y, width - HEAD_DIM // 2, 1), pltpu.roll(y, HEAD_DIM // 2, 1))
    reps = width // LANES
    c = jnp.concatenate([cos] * reps, axis=1) if reps > 1 else cos
    s = jnp.concatenate([sin_signed] * reps, axis=1) if reps > 1 else sin_signed
    return y * c + swapped * s


def _attn_proj_kernel(h_ref, w_ref, cos_ref, sin_ref, o_ref, wb_ref, *, scale, rope_tiles, first_sample_tile):
    n = pl.program_id(0)
    m = pl.program_id(1)

    @pl.when(m == 0)
    def _():
        wb_ref[...] = w_ref[0].astype(BF16)

    def project(rope):
        for rows in _row_chunks(h_ref.shape[0]):
            y = jnp.dot(h_ref[rows, :], wb_ref[...], preferred_element_type=F32)
            if scale != 1.0:
                y = y * scale
            if rope:
                y = _rope64(y, cos_ref[rows, :], sin_ref[rows, :])
            o_ref[rows, :] = y.astype(o_ref.dtype)

    do_rope = jnp.logical_and(m >= first_sample_tile, n < rope_tiles)
    pl.when(do_rope)(lambda: project(True))
    pl.when(jnp.logical_not(do_rope))(lambda: project(False))


def _attn_proj(h, w_qkv, layer, cos, sin, *, col0, ncols, tn, rope_tiles, scale, out_dtype):
    tm = TM
    first_sample_tile = N_PROMPT // tm
    tiles_per_seq = DEC_SEQ // tm
    cb0 = col0 // tn

    def tab_map(n, m):
        return (jnp.maximum(m - first_sample_tile, 0) % tiles_per_seq, 0)

    kern = functools.partial(_attn_proj_kernel, scale=scale, rope_tiles=rope_tiles,
                             first_sample_tile=first_sample_tile)
    return pl.pallas_call(
        kern,
        grid=(ncols // tn, N_TOK // tm),
        in_specs=[pl.BlockSpec((tm, D_MODEL), lambda n, m: (m, 0)),
                  pl.BlockSpec((1, D_MODEL, tn), lambda n, m: (layer, 0, cb0 + n)),
                  pl.BlockSpec((tm, LANES), tab_map),
                  pl.BlockSpec((tm, LANES), tab_map)],
        out_specs=pl.BlockSpec((tm, tn), lambda n, m: (m, n)),
        out_shape=jax.ShapeDtypeStruct((N_TOK, ncols), out_dtype),
        scratch_shapes=[pltpu.VMEM((D_MODEL, tn), BF16)],
        compiler_params=_cparams(("arbitrary", "arbitrary")),
        name="attn_proj",
    )(h, w_qkv, cos, sin)


def _ret_proj_kernel(h_ref, w_ref, cos_ref, sin_ref, o_ref, wb_ref, *, first_sample_tile, tn):
    n = pl.program_id(0)
    m = pl.program_id(1)

    @pl.when(m == 0)
    def _():
        wb_ref[...] = w_ref[0].astype(BF16)

    k_tiles_lo = RET_HK // tn
    qk_tiles = 2 * RET_HK // tn
    is_k = jnp.logical_and(n >= k_tiles_lo, n < qk_tiles)
    k_scale = jnp.where(is_k, RET_DK ** -0.5, 1.0).astype(F32)
    half = RET_DK // 2

    def project(rope):
        for rows in _row_chunks(h_ref.shape[0]):
            y = jnp.dot(h_ref[rows, :], wb_ref[...], preferred_element_type=F32) * k_scale
            if rope:
                c = cos_ref[rows, :]
                s = sin_ref[rows, :]
                parts = []
                for hh in range(tn // RET_DK):
                    x1 = y[:, hh * RET_DK:hh * RET_DK + half]
                    x2 = y[:, hh * RET_DK + half:(hh + 1) * RET_DK]
                    parts.append(x1 * c - x2 * s)
                    parts.append(x1 * s + x2 * c)
                y = jnp.concatenate(parts, axis=1)
            o_ref[rows, :] = y.astype(BF16)

    do_rope = jnp.logical_and(m >= first_sample_tile, n < qk_tiles)
    pl.when(do_rope)(lambda: project(True))
    pl.when(jnp.logical_not(do_rope))(lambda: project(False))


def _ret_proj(h, w_ret_in, layer, cos, sin):
    tm, tn = TM, 512
    ncols = 2 * RET_HK + 3 * RET_HV
    first_sample_tile = N_PROMPT // tm
    tiles_per_seq = DEC_SEQ // tm

    def tab_map(n, m):
        return (jnp.maximum(m - first_sample_tile, 0) % tiles_per_seq, 0)

    kern = functools.partial(_ret_proj_kernel, first_sample_tile=first_sample_tile, tn=tn)
    return pl.pallas_call(
        kern,
        grid=(ncols // tn, N_TOK // tm),
        in_specs=[pl.BlockSpec((tm, D_MODEL), lambda n, m: (m, 0)),
                  pl.BlockSpec((1, D_MODEL, tn), lambda n, m: (layer, 0, n)),
                  pl.BlockSpec((tm, LANES), tab_map),
                  pl.BlockSpec((tm, LANES), tab_map)],
        out_specs=pl.BlockSpec((tm, tn), lambda n, m: (m, n)),
        out_shape=jax.ShapeDtypeStruct((N_TOK, ncols), BF16),
        scratch_shapes=[pltpu.VMEM((D_MODEL, tn), BF16)],
        compiler_params=_cparams(("arbitrary", "arbitrary")),
        name="ret_proj",
    )(h, w_ret_in, cos, sin)


def _swiglu_kernel(h_ref, wg_ref, wu_ref, o_ref, wgb_ref, wub_ref):
    m = pl.program_id(1)

    @pl.when(m == 0)
    def _():
        wgb_ref[...] = wg_ref[0].astype(BF16)
        wub_ref[...] = wu_ref[0].astype(BF16)

    for rows in _row_chunks(h_ref.shape[0]):
        h = h_ref[rows, :]
        g = jnp.dot(h, wgb_ref[...], preferred_element_type=F32)
        u = jnp.dot(h, wub_ref[...], preferred_element_type=F32)
        o_ref[rows, :] = (g * jax.nn.sigmoid(g) * u).astype(BF16)


def _swiglu_up(h, w_gu, layer):
    tm, tf = TM, 256
    nf = D_FF // tf
    return pl.pallas_call(
        _swiglu_kernel,
        grid=(nf, N_TOK // tm),
        in_specs=[pl.BlockSpec((tm, D_MODEL), lambda f, m: (m, 0)),
                  pl.BlockSpec((1, D_MODEL, tf), lambda f, m: (layer, 0, f)),
                  pl.BlockSpec((1, D_MODEL, tf), lambda f, m: (layer, 0, nf + f))],
        out_specs=pl.BlockSpec((tm, tf), lambda f, m: (m, f)),
        out_shape=jax.ShapeDtypeStruct((N_TOK, D_FF), BF16),
        scratch_shapes=[pltpu.VMEM((D_MODEL, tf), BF16), pltpu.VMEM((D_MODEL, tf), BF16)],
        compiler_params=_cparams(("arbitrary", "arbitrary")),
        name="swiglu_up",
    )(h, w_gu, w_gu)


def _route(logits):
    lane = lax.broadcasted_iota(I32, logits.shape, 1).astype(F32)
    lg = jnp.where(lane < N_EXPERTS, logits, -jnp.inf)
    m1 = jnp.max(lg, axis=-1, keepdims=True)
    i1 = jnp.min(jnp.where(lg == m1, lane, float(LANES)), axis=-1, keepdims=True)
    lg2 = jnp.where(lane == i1, -jnp.inf, lg)
    m2 = jnp.max(lg2, axis=-1, keepdims=True)
    i2 = jnp.min(jnp.where(lg2 == m2, lane, float(LANES)), axis=-1, keepdims=True)
    e2 = jnp.exp(m2 - m1)
    p1 = 1.0 / (1.0 + e2)
    p2 = e2 / (1.0 + e2)
    out = jnp.where(lane == 0, i1, 0.0)
    out = jnp.where(lane == 1, i2, out)
    out = jnp.where(lane == 2, p1, out)
    out = jnp.where(lane == 3, p2, out)
    return out


def _epilogue(x_new, pk_next, rows, fin_ref, final):
    if final:
        ms = jnp.mean(x_new * x_new, axis=-1, keepdims=True)
        return x_new * lax.rsqrt(ms + EPS) * fin_ref[...]
    g_row, sc_row, sh_row = rows
    return _normmod(x_new, pk_next[g_row:g_row + 1], pk_next[sc_row:sc_row + 1], pk_next[sh_row:sh_row + 1])


def _down_kernel(*refs, n_a, first_sample_tile, gate_row, next_rows, router):
    a_refs = refs[:n_a]
    w_ref, x_ref, pk_ref, pkn_ref = refs[n_a:n_a + 4]
    pos = n_a + 4
    if router:
        wrh_ref, wrl_ref = refs[pos:pos + 2]
        pos += 2
        xo_ref, h_ref, hf_ref, rt_ref, wb_ref = refs[pos:pos + 5]
    else:
        xo_ref, h_ref, wb_ref = refs[pos:pos + 3]
    m = pl.program_id(0)

    @pl.when(m == 0)
    def _():
        wb_ref[...] = w_ref[0].astype(BF16)

    def finish(a_ref):
        pk = pk_ref[0, 0]
        pkn = pkn_ref[0, 0]
        for c in range(a_ref.shape[0] // DOWN_CHUNK):
            rows = pl.ds(c * DOWN_CHUNK, DOWN_CHUNK)
            y = jnp.dot(a_ref[rows, :], wb_ref[...], preferred_element_type=F32)
            x_new = x_ref[rows, :] + pk[gate_row:gate_row + 1] * y
            xo_ref[rows, :] = x_new
            hn = _epilogue(x_new, pkn, next_rows, None, False)
            hi = hn.astype(BF16)
            h_ref[rows, :] = hi
            if router:
                hf_ref[rows, :] = hn
                lo = (hn - hi.astype(F32)).astype(BF16)
                logits = (jnp.dot(hi, wrh_ref[0], preferred_element_type=F32)
                          + jnp.dot(lo, wrh_ref[0], preferred_element_type=F32)
                          + jnp.dot(hi, wrl_ref[0], preferred_element_type=F32))
                rt_ref[rows, :] = _route(logits)

    if n_a == 1:
        finish(a_refs[0])
    else:
        @pl.when(m < first_sample_tile)
        def _():
            finish(a_refs[0])

        @pl.when(m >= first_sample_tile)
        def _():
            finish(a_refs[1])


def _down(a_list, w, layer, x, pack, pack_layer, gate_row, next_layer, next_rows, w_router=None, router_layer=0):
    tm = TM_DOWN
    kd = w.shape[1]
    n_a = len(a_list)
    fst = N_PROMPT // tm
    router = w_router is not None
    if n_a == 1:
        a_specs = [pl.BlockSpec((tm, kd), lambda m: (m, 0))]
    else:
        a_specs = [pl.BlockSpec((tm, kd), lambda m: (jnp.minimum(m, fst - 1), 0)),
                   pl.BlockSpec((tm, kd), lambda m: (jnp.maximum(m - fst, 0), 0))]
    in_specs = a_specs + [
        pl.BlockSpec((1, kd, D_MODEL), lambda m: (layer, 0, 0)),
        pl.BlockSpec((tm, D_MODEL), lambda m: (m, 0)),
        pl.BlockSpec((1, 1, 8, D_MODEL), lambda m: (pack_layer, (m * tm) // GROUP_TOKENS, 0, 0)),
        pl.BlockSpec((1, 1, 8, D_MODEL), lambda m: (next_layer, (m * tm) // GROUP_TOKENS, 0, 0)),
    ]
    args = list(a_list) + [w, x, pack, pack]
    out_specs = [pl.BlockSpec((tm, D_MODEL), lambda m: (m, 0)),
                 pl.BlockSpec((tm, D_MODEL), lambda m: (m, 0))]
    out_shape = [jax.ShapeDtypeStruct((N_TOK, D_MODEL), F32),
                 jax.ShapeDtypeStruct((N_TOK, D_MODEL), BF16)]
    if router:
        in_specs += [pl.BlockSpec((1, D_MODEL, LANES), lambda m: (router_layer, 0, 0)),
                     pl.BlockSpec((1, D_MODEL, LANES), lambda m: (router_layer, 0, 0))]
        args += list(w_router)
        out_specs += [pl.BlockSpec((tm, D_MODEL), lambda m: (m, 0)),
                      pl.BlockSpec((tm, LANES), lambda m: (m, 0))]
        out_shape += [jax.ShapeDtypeStruct((N_TOK, D_MODEL), F32),
                      jax.ShapeDtypeStruct((N_TOK, LANES), F32)]
    kern = functools.partial(_down_kernel, n_a=n_a, first_sample_tile=fst, gate_row=gate_row,
                             next_rows=next_rows, router=router)
    return pl.pallas_call(
        kern,
        grid=(N_TOK // tm,),
        in_specs=in_specs,
        out_specs=out_specs,
        out_shape=out_shape,
        scratch_shapes=[pltpu.VMEM((kd, D_MODEL), BF16)],
        compiler_params=_cparams(("arbitrary",)),
        name="down_proj",
    )(*args)


def _sink_row(sink_ref, layer, kvh, width):
    return jnp.concatenate(
        [jnp.full((1, width), sink_ref[layer * N_HEADS + kvh * GROUP + g], F32) for g in range(GROUP)], axis=1)


def _group_queries(q_ref, kvh):
    return jnp.concatenate(
        [q_ref[:, (kvh * GROUP + g) * HEAD_DIM:(kvh * GROUP + g + 1) * HEAD_DIM] for g in range(GROUP)], axis=0)


_NT = (((1,), (1,)), ((), ()))


def _ctx_attn_kernel(sink_ref, q_ref, kv_ref, o_ref, *, layer):
    kvw = N_KV_HEADS * HEAD_DIM
    k_all = kv_ref[:, :kvw].astype(BF16)
    v_t = kv_ref[:, kvw:].T.astype(BF16)
    outs = []
    for kvh in range(N_KV_HEADS):
        hs = slice(kvh * HEAD_DIM, (kvh + 1) * HEAD_DIM)
        s_t = lax.dot_general(k_all[:, hs], _group_queries(q_ref, kvh), _NT, preferred_element_type=F32)
        sink = _sink_row(sink_ref, layer, kvh, SEQ)
        m = jnp.maximum(jnp.max(s_t, axis=0, keepdims=True), sink)
        p = jnp.exp(s_t - m)
        denom = jnp.sum(p, axis=0, keepdims=True) + jnp.exp(sink - m)
        o_t = jnp.dot(v_t[hs], p.astype(BF16), preferred_element_type=F32) / denom
        outs += [o_t[:, g * SEQ:(g + 1) * SEQ] for g in range(GROUP)]
    o_ref[...] = jnp.concatenate(outs, axis=0).T.astype(BF16)


def _ctx_attention(q, kv, sink_flat, layer):
    kern = functools.partial(_ctx_attn_kernel, layer=layer)
    return pl.pallas_call(
        kern,
        grid_spec=pltpu.PrefetchScalarGridSpec(
            num_scalar_prefetch=1,
            grid=(BATCH,),
            in_specs=[pl.BlockSpec((SEQ, D_MODEL), lambda b, s: (b, 0)),
                      pl.BlockSpec((SEQ, 2 * N_KV_HEADS * HEAD_DIM), lambda b, s: (b, 0))],
            out_specs=pl.BlockSpec((SEQ, D_MODEL), lambda b, s: (b, 0))),
        out_shape=jax.ShapeDtypeStruct((N_PROMPT, D_MODEL), BF16),
        compiler_params=_cparams(("arbitrary",)),
        name="ctx_attention",
    )(sink_flat, q, kv)


def _lat_attn_kernel(sink_ref, q_ref, kvp_ref, kvc_ref, kvn_ref, ck_ref, cv_ref, o_ref, kc_ref, vct_ref, *, layer):
    blk = pl.program_id(1)
    kvw = N_KV_HEADS * HEAD_DIM
    cols = GROUP * BLOCK
    n_lat = 3 * BLOCK

    @pl.when(blk == 0)
    def _():
        kc_ref[...] = ck_ref[0, 0].astype(BF16)
        vct_ref[...] = cv_ref[0, 0].T.astype(BF16)

    j = lax.broadcasted_iota(I32, (n_lat, cols), 0)
    r = lax.broadcasted_iota(I32, (n_lat, cols), 1) % BLOCK
    lo = jnp.maximum(r, BLOCK - BLOCK * blk)
    hi = jnp.minimum(r + 2 * WINDOW, DEC_SEQ + BLOCK - 1 - BLOCK * blk)
    lat_mask = jnp.logical_and(j >= lo, j <= hi)
    k_lat = jnp.concatenate([kvp_ref[:, :kvw], kvc_ref[:, :kvw], kvn_ref[:, :kvw]], axis=0).astype(BF16)
    v_lat_t = jnp.concatenate([kvp_ref[:, kvw:].T, kvc_ref[:, kvw:].T, kvn_ref[:, kvw:].T], axis=1).astype(BF16)
    outs = []
    for kvh in range(N_KV_HEADS):
        hs = slice(kvh * HEAD_DIM, (kvh + 1) * HEAD_DIM)
        q4 = _group_queries(q_ref, kvh)
        s_ctx = lax.dot_general(kc_ref[:, hs], q4, _NT, preferred_element_type=F32)
        s_lat = lax.dot_general(k_lat[:, hs], q4, _NT, preferred_element_type=F32)
        s_lat = jnp.where(lat_mask, s_lat, NEG)
        sink = _sink_row(sink_ref, layer, kvh, BLOCK)
        m = jnp.maximum(jnp.maximum(jnp.max(s_ctx, axis=0, keepdims=True),
                                    jnp.max(s_lat, axis=0, keepdims=True)), sink)
        p_ctx = jnp.exp(s_ctx - m)
        p_lat = jnp.exp(s_lat - m)
        denom = (jnp.sum(p_ctx, axis=0, keepdims=True) + jnp.sum(p_lat, axis=0, keepdims=True)
                 + jnp.exp(sink - m))
        o_t = (jnp.dot(vct_ref[hs, :], p_ctx.astype(BF16), preferred_element_type=F32)
               + jnp.dot(v_lat_t[hs], p_lat.astype(BF16), preferred_element_type=F32)) / denom
        outs += [o_t[:, g * BLOCK:(g + 1) * BLOCK] for g in range(GROUP)]
    o_ref[...] = jnp.concatenate(outs, axis=0).T.astype(BF16)


def _lat_attention(q, kv, cache_k4, cache_v4, sink_flat, layer):
    nb = DEC_SEQ // BLOCK
    base = N_PROMPT // BLOCK
    kvc = 2 * N_KV_HEADS * HEAD_DIM

    def row(b, i):
        return base + b * nb + i

    kern = functools.partial(_lat_attn_kernel, layer=layer)
    return pl.pallas_call(
        kern,
        grid_spec=pltpu.PrefetchScalarGridSpec(
            num_scalar_prefetch=1,
            grid=(DEC_BATCH, nb),
            in_specs=[pl.BlockSpec((BLOCK, D_MODEL), lambda b, i, s: (row(b, i), 0)),
                      pl.BlockSpec((BLOCK, kvc), lambda b, i, s: (row(b, jnp.maximum(i - 1, 0)), 0)),
                      pl.BlockSpec((BLOCK, kvc), lambda b, i, s: (row(b, i), 0)),
                      pl.BlockSpec((BLOCK, kvc), lambda b, i, s: (row(b, jnp.minimum(i + 1, nb - 1)), 0)),
                      pl.BlockSpec((1, 1, PAST_LEN, N_KV_HEADS * HEAD_DIM), lambda b, i, s: (b, layer, 0, 0)),
                      pl.BlockSpec((1, 1, PAST_LEN, N_KV_HEADS * HEAD_DIM), lambda b, i, s: (b, layer, 0, 0))],
            out_specs=pl.BlockSpec((BLOCK, D_MODEL), lambda b, i, s: (b * nb + i, 0)),
            scratch_shapes=[pltpu.VMEM((PAST_LEN, N_KV_HEADS * HEAD_DIM), BF16),
                            pltpu.VMEM((N_KV_HEADS * HEAD_DIM, PAST_LEN), BF16)]),
        out_shape=jax.ShapeDtypeStruct((N_SAMPLE, D_MODEL), BF16),
        compiler_params=_cparams(("arbitrary", "arbitrary")),
        name="lat_attention",
    )(sink_flat, q, kv, kv, kv, cache_k4, cache_v4)


N_CHUNKS = N_TOK // BLOCK
PROMPT_CHUNKS = N_PROMPT // BLOCK
CHUNKS_PER_PROMPT = SEQ // BLOCK
CHUNKS_PER_SAMPLE = DEC_SEQ // BLOCK


def _ret_kernel(lg_ref, q_ref, k_ref, v_ref, gate_ref, s0_ref, *rest, backward, layer):
    part_ref = rest[0] if backward else None
    o_ref, sout_ref, state_ref = rest[-3:]
    step = pl.program_id(0)
    c = (N_CHUNKS - 1 - step) if backward else step
    in_prompt = c < PROMPT_CHUNKS
    if backward:
        seq_start = jnp.where(in_prompt, c % CHUNKS_PER_PROMPT == CHUNKS_PER_PROMPT - 1,
                              (c - PROMPT_CHUNKS) % CHUNKS_PER_SAMPLE == CHUNKS_PER_SAMPLE - 1)
        seq_end = jnp.logical_and(in_prompt, c % CHUNKS_PER_PROMPT == 0)
    else:
        seq_start = jnp.where(in_prompt, c % CHUNKS_PER_PROMPT == 0,
                              (c - PROMPT_CHUNKS) % CHUNKS_PER_SAMPLE == 0)
        seq_end = jnp.logical_and(in_prompt, c % CHUNKS_PER_PROMPT == CHUNKS_PER_PROMPT - 1)

    @pl.when(jnp.logical_and(seq_start, in_prompt))
    def _():
        state_ref[...] = jnp.zeros_like(state_ref)

    @pl.when(jnp.logical_and(seq_start, jnp.logical_not(in_prompt)))
    def _():
        state_ref[...] = s0_ref[0, 0]

    ii = lax.broadcasted_iota(I32, (BLOCK, BLOCK), 0).astype(F32)
    jj = lax.broadcasted_iota(I32, (BLOCK, BLOCK), 1).astype(F32)
    dist = (jj - ii) if backward else (ii - jj)
    ti = lax.broadcasted_iota(I32, (BLOCK, 1), 0).astype(F32)
    q_pow = (BLOCK - ti) if backward else (ti + 1.0)
    k_pow = ti if backward else (BLOCK - 1.0 - ti)

    for h in range(RET_HEADS):
        lg = lg_ref[(layer * 2 + (1 if backward else 0)) * RET_HEADS + h]
        intra = jnp.where(dist >= 0, jnp.exp(lg * jnp.maximum(dist, 0.0)), 0.0)
        q_dec = jnp.exp(lg * q_pow)
        k_dec = jnp.exp(lg * k_pow)
        c_dec = jnp.exp(lg * BLOCK)
        q = q_ref[:, h * RET_DK:(h + 1) * RET_DK]
        k = k_ref[:, h * RET_DK:(h + 1) * RET_DK]
        v = v_ref[:, h * RET_DV:(h + 1) * RET_DV]
        s = state_ref[h]
        a = lax.dot_general(q, k, (((1,), (1,)), ((), ())), preferred_element_type=F32) * intra
        o = (jnp.dot(a.astype(BF16), v, preferred_element_type=F32)
             + q_dec * jnp.dot(q, s.astype(BF16), preferred_element_type=F32))
        kd = (k.astype(F32) * k_dec).astype(BF16)
        s_new = c_dec * s + lax.dot_general(kd, v, (((0,), (0,)), ((), ())), preferred_element_type=F32)
        state_ref[h] = s_new
        gate = gate_ref[:, h * RET_DV:(h + 1) * RET_DV].astype(F32)
        on = o * lax.rsqrt(jnp.mean(o * o, axis=-1, keepdims=True) + EPS)
        res = on * (gate * jax.nn.sigmoid(gate))
        if backward:
            res = res + part_ref[:, h * RET_DV:(h + 1) * RET_DV].astype(F32)
        o_ref[:, h * RET_DV:(h + 1) * RET_DV] = res.astype(BF16)

    @pl.when(seq_end)
    def _():
        if sout_ref.shape[1] == 1:
            sout_ref[0, 0] = state_ref[...]
        else:
            for l in range(sout_ref.shape[1]):
                sout_ref[0, l] = state_ref[...] if l == layer else jnp.zeros_like(state_ref)


def _retention_pass(proj, s0, log_gamma_flat, layer, backward, partial=None, states=None):
    def chunk(i):
        return (N_CHUNKS - 1 - i) if backward else i

    def s0_map(i, lg):
        c = chunk(i)
        return (jnp.clip((c - PROMPT_CHUNKS) // CHUNKS_PER_SAMPLE, 0, DEC_BATCH - 1), layer, 0, 0, 0)

    sout_layers = N_ODD if states is None else 1

    def sout_map(i, lg):
        c = chunk(i)
        return (jnp.clip(c // CHUNKS_PER_PROMPT, 0, BATCH - 1), 0 if states is None else layer, 0, 0, 0)

    gate_block = 3 if backward else 2
    in_specs = [pl.BlockSpec((BLOCK, RET_HK), lambda i, lg: (chunk(i), 0)),
                pl.BlockSpec((BLOCK, RET_HK), lambda i, lg: (chunk(i), 1)),
                pl.BlockSpec((BLOCK, RET_HV), lambda i, lg: (chunk(i), 1)),
                pl.BlockSpec((BLOCK, RET_HV), lambda i, lg: (chunk(i), gate_block)),
                pl.BlockSpec((1, 1, RET_HEADS, RET_DK, RET_DV), s0_map)]
    args = [proj, proj, proj, proj, s0]
    if backward:
        in_specs.append(pl.BlockSpec((BLOCK, RET_HV), lambda i, lg: (chunk(i), 0)))
        args.append(partial)
    aliases = {}
    if states is not None:
        in_specs.append(pl.BlockSpec(memory_space=pl.ANY))
        args.append(states)
        aliases = {len(args): 1}
    kern = functools.partial(_ret_kernel, backward=backward, layer=layer)
    return pl.pallas_call(
        kern,
        grid_spec=pltpu.PrefetchScalarGridSpec(
            num_scalar_prefetch=1,
            grid=(N_CHUNKS,),
            in_specs=in_specs,
            out_specs=[pl.BlockSpec((BLOCK, RET_HV), lambda i, lg: (chunk(i), 0)),
                       pl.BlockSpec((1, sout_layers, RET_HEADS, RET_DK, RET_DV), sout_map)],
            scratch_shapes=[pltpu.VMEM((RET_HEADS, RET_DK, RET_DV), F32)]),
        out_shape=[jax.ShapeDtypeStruct((N_TOK, RET_HV), BF16),
                   jax.ShapeDtypeStruct((BATCH, N_ODD, RET_HEADS, RET_DK, RET_DV), F32)],
        input_output_aliases=aliases,
        compiler_params=_cparams(("arbitrary",)),
        name="retention_bwd" if backward else "retention_fwd",
    )(log_gamma_flat, *args)


def _dispatch_kernel(slot_ref, pend_ref, h_ref, o_hbm, zbuf, sem):
    i = pl.program_id(0)

    @pl.when(i == 0)
    def _():
        zbuf[...] = jnp.zeros_like(zbuf)

        def tile_fill(start):
            return pltpu.make_async_copy(zbuf, o_hbm.at[pl.ds(pl.multiple_of(start, MOE_TILE), MOE_TILE)], sem)

        def nonempty(e):
            return pend_ref[e] > (pend_ref[e - 1] if e else 0)

        def start_unused(t, carry):
            tile_fill(t * MOE_TILE).start()
            return carry

        def wait_unused(t, carry):
            tile_fill(t * MOE_TILE).wait()
            return carry

        first_unused = pend_ref[N_EXPERTS - 1] // MOE_TILE
        for e in range(N_EXPERTS):
            pl.when(nonempty(e))(lambda e=e: tile_fill(pend_ref[e] - MOE_TILE).start())
        lax.fori_loop(first_unused, MOE_NT, start_unused, 0)
        for e in range(N_EXPERTS):
            pl.when(nonempty(e))(lambda e=e: tile_fill(pend_ref[e] - MOE_TILE).wait())
        lax.fori_loop(first_unused, MOE_NT, wait_unused, 0)

    base = i * DISPATCH_ROWS * TOP_K

    def start(r, carry):
        for k in range(TOP_K):
            pltpu.make_async_copy(h_ref.at[pl.ds(r, 1)], o_hbm.at[pl.ds(slot_ref[base + r * TOP_K + k], 1)],
                                  sem).start()
        return carry

    lax.fori_loop(0, DISPATCH_ROWS, start, 0, unroll=8)
    for k in range(TOP_K):
        pltpu.make_async_copy(h_ref, o_hbm.at[pl.ds(0, DISPATCH_ROWS)], sem).wait()


def _dispatch_rows(slot, pend, h):
    return pl.pallas_call(
        _dispatch_kernel,
        grid_spec=pltpu.PrefetchScalarGridSpec(
            num_scalar_prefetch=2,
            grid=(N_TOK // DISPATCH_ROWS,),
            in_specs=[pl.BlockSpec((DISPATCH_ROWS, D_MODEL), lambda i, s, p: (i, 0))],
            out_specs=pl.BlockSpec(memory_space=pl.ANY),
            scratch_shapes=[pltpu.VMEM((MOE_TILE, D_MODEL), F32), pltpu.SemaphoreType.DMA(())]),
        out_shape=jax.ShapeDtypeStruct((MOE_SLOTS, D_MODEL), F32),
        compiler_params=_cparams(("arbitrary",)),
        name="moe_dispatch",
    )(slot, pend, h)


def _moe_up_kernel(te_ref, nv_ref, h_ref, wg_ref, wu_ref, o_ref, wgb_ref, wub_ref):
    m = pl.program_id(1)
    new_w = jnp.logical_or(m == 0, te_ref[m] != te_ref[jnp.maximum(m - 1, 0)])

    @pl.when(new_w)
    def _():
        wgb_ref[...] = wg_ref[0, 0].astype(BF16)
        wub_ref[...] = wu_ref[0, 0].astype(BF16)

    @pl.when(m < nv_ref[0])
    def _():
        for c in range(MOE_TILE // MOE_CHUNK):
            rows = pl.ds(c * MOE_CHUNK, MOE_CHUNK)
            h = h_ref[rows, :].astype(BF16)
            g = jnp.dot(h, wgb_ref[...], preferred_element_type=F32)
            u = jnp.dot(h, wub_ref[...], preferred_element_type=F32)
            o_ref[rows, :] = (g * jax.nn.sigmoid(g) * u).astype(BF16)

    @pl.when(m >= nv_ref[0])
    def _():
        o_ref[...] = jnp.zeros_like(o_ref)


def _moe_up(tile_expert, n_valid, hs, w_gu, layer):
    tf = 896
    nf = D_FF_EXPERT // tf
    return pl.pallas_call(
        _moe_up_kernel,
        grid_spec=pltpu.PrefetchScalarGridSpec(
            num_scalar_prefetch=2,
            grid=(nf, MOE_NT),
            in_specs=[pl.BlockSpec((MOE_TILE, D_MODEL), lambda f, m, te, nv: (jnp.minimum(m, nv[0] - 1), 0)),
                      pl.BlockSpec((1, 1, D_MODEL, tf), lambda f, m, te, nv: (layer, te[m], 0, f)),
                      pl.BlockSpec((1, 1, D_MODEL, tf), lambda f, m, te, nv: (layer, te[m], 0, nf + f))],
            out_specs=pl.BlockSpec((MOE_TILE, tf), lambda f, m, te, nv: (m, f)),
            scratch_shapes=[pltpu.VMEM((D_MODEL, tf), BF16), pltpu.VMEM((D_MODEL, tf), BF16)]),
        out_shape=jax.ShapeDtypeStruct((MOE_SLOTS, D_FF_EXPERT), BF16),
        compiler_params=_cparams(("arbitrary", "arbitrary")),
        name="moe_up",
    )(tile_expert, n_valid, hs, w_gu, w_gu)


def _moe_down_kernel(te_ref, nv_ref, a_ref, w_ref, o_ref, wb_ref):
    m = pl.program_id(1)
    new_w = jnp.logical_or(m == 0, te_ref[m] != te_ref[jnp.maximum(m - 1, 0)])

    @pl.when(new_w)
    def _():
        wb_ref[...] = w_ref[0, 0].astype(BF16)

    @pl.when(m < nv_ref[0])
    def _():
        o_ref[...] = jnp.dot(a_ref[...], wb_ref[...], preferred_element_type=F32)

    @pl.when(m >= nv_ref[0])
    def _():
        o_ref[...] = jnp.zeros_like(o_ref)


def _moe_down(tile_expert, n_valid, act, w_down, layer):
    tn = 512
    return pl.pallas_call(
        _moe_down_kernel,
        grid_spec=pltpu.PrefetchScalarGridSpec(
            num_scalar_prefetch=2,
            grid=(D_MODEL // tn, MOE_NT),
            in_specs=[pl.BlockSpec((MOE_TILE, D_FF_EXPERT), lambda n, m, te, nv: (jnp.minimum(m, nv[0] - 1), 0)),
                      pl.BlockSpec((1, 1, D_FF_EXPERT, tn), lambda n, m, te, nv: (layer, te[m], 0, n))],
            out_specs=pl.BlockSpec((MOE_TILE, tn), lambda n, m, te, nv: (m, n)),
            scratch_shapes=[pltpu.VMEM((D_FF_EXPERT, tn), BF16)]),
        out_shape=jax.ShapeDtypeStruct((MOE_SLOTS, D_MODEL), F32),
        compiler_params=_cparams(("arbitrary", "arbitrary")),
        name="moe_down",
    )(tile_expert, n_valid, act, w_down)


def _combine_kernel(slot_ref, y_hbm, x_ref, rt_ref, pk_ref, pkn_ref, fin_ref, *rest, tok0, final):
    if final:
        out_ref, ybuf, sem = rest
    else:
        xo_ref, h_ref, ybuf, sem = rest
    tc = TC_COMBINE
    base = (tok0 + pl.program_id(0) * tc) * TOP_K

    def start(r, carry):
        for k in range(TOP_K):
            pltpu.make_async_copy(y_hbm.at[pl.ds(slot_ref[base + r * TOP_K + k], 1)],
                                  ybuf.at[k, pl.ds(r, 1)], sem).start()
        return carry

    lax.fori_loop(0, tc, start, 0, unroll=8)
    for k in range(TOP_K):
        pltpu.make_async_copy(y_hbm.at[pl.ds(0, tc)], ybuf.at[k], sem).wait()

    rt = rt_ref[...]
    moe = rt[:, 2:3] * ybuf[0] + rt[:, 3:4] * ybuf[1]
    pk = pk_ref[0, 0]
    x_new = x_ref[...] + pk[R_GATE_FFN:R_GATE_FFN + 1] * moe
    if final:
        out_ref[...] = _epilogue(x_new, None, None, fin_ref, True)
    else:
        xo_ref[...] = x_new
        h_ref[...] = _epilogue(x_new, pkn_ref[0, 0], (R_G_MIX, R_SCALE_MIX, R_SHIFT_MIX), None, False).astype(BF16)


def _combine(slot, y, x, route, pack, layer, next_layer, final_norm, tok0, n_rows, final):
    tc = TC_COMBINE
    b0 = tok0 // tc

    def tmap(m, s):
        return (b0 + m, 0)

    def pmap(l):
        return lambda m, s: (l, (tok0 + m * tc) // GROUP_TOKENS, 0, 0)

    if final:
        out_specs = pl.BlockSpec((tc, D_MODEL), lambda m, s: (m, 0))
        out_shape = jax.ShapeDtypeStruct((n_rows, D_MODEL), F32)
    else:
        out_specs = [pl.BlockSpec((tc, D_MODEL), lambda m, s: (m, 0)),
                     pl.BlockSpec((tc, D_MODEL), lambda m, s: (m, 0))]
        out_shape = [jax.ShapeDtypeStruct((n_rows, D_MODEL), F32),
                     jax.ShapeDtypeStruct((n_rows, D_MODEL), BF16)]
    kern = functools.partial(_combine_kernel, tok0=tok0, final=final)
    return pl.pallas_call(
        kern,
        grid_spec=pltpu.PrefetchScalarGridSpec(
            num_scalar_prefetch=1,
            grid=(n_rows // tc,),
            in_specs=[pl.BlockSpec(memory_space=pl.ANY),
                      pl.BlockSpec((tc, D_MODEL), tmap),
                      pl.BlockSpec((tc, LANES), tmap),
                      pl.BlockSpec((1, 1, 8, D_MODEL), pmap(layer)),
                      pl.BlockSpec((1, 1, 8, D_MODEL), pmap(next_layer)),
                      pl.BlockSpec((1, D_MODEL), lambda m, s: (0, 0))],
            out_specs=out_specs,
            scratch_shapes=[pltpu.VMEM((TOP_K, tc, D_MODEL), F32), pltpu.SemaphoreType.DMA(())]),
        out_shape=out_shape,
        compiler_params=_cparams(("arbitrary",)),
        name="moe_combine",
    )(slot, y, x, route, pack, pack, final_norm.reshape(1, D_MODEL))


def _routing_tables(route):
    e_flat = route[:, :TOP_K].astype(I32).reshape(-1)
    onehot = (e_flat[:, None] == jnp.arange(N_EXPERTS, dtype=I32)[None, :]).astype(I32)
    csum = jnp.cumsum(onehot, axis=0)
    rank = jnp.sum(onehot * (csum - 1), axis=1)
    count = csum[-1]
    padded = ((count + MOE_TILE - 1) // MOE_TILE) * MOE_TILE
    pend = jnp.cumsum(padded)
    poff = pend - padded
    slot = jnp.sum(onehot * poff[None, :], axis=1) + rank
    n_valid = pend[-1] // MOE_TILE
    tile_start = jnp.arange(MOE_NT, dtype=I32) * MOE_TILE
    te_raw = jnp.minimum(jnp.sum((tile_start[:, None] >= pend[None, :]).astype(I32), axis=1), N_EXPERTS - 1)
    last_e = jnp.max(jnp.where(tile_start < pend[-1], te_raw, 0))
    tile_expert = jnp.minimum(te_raw, last_e)
    return slot.astype(I32), pend.astype(I32), tile_expert.astype(I32), n_valid.astype(I32).reshape(1)


def _rope_tables(dh):
    nf = dh // 4
    t = jnp.arange(DEC_SEQ)
    freqs = ROPE_BASE ** (-jnp.arange(nf, dtype=F32) / nf)
    row = (t // GRID_W).astype(F32)
    col = (t % GRID_W).astype(F32)
    ang = jnp.concatenate([row[:, None] * freqs, col[:, None] * freqs], axis=-1)
    return jnp.cos(ang), jnp.sin(ang)


def kernel(x_prompt, x_sample, cache_k, cache_v, state_fwd, state_bwd, c, c_ctx, norm_mix, norm_ffn, w_mod, b_mod,
           w_qkv, w_attn_o, attn_sink, w_ret_in, ret_decay, w_ret_out, w_ffn_gu, w_ffn_down, w_router, w_exp_gu,
           w_exp_down, final_norm):
    x = jnp.concatenate([x_prompt.reshape(N_PROMPT, D_MODEL), x_sample.reshape(N_SAMPLE, D_MODEL)], axis=0)

    cond8 = jnp.concatenate([c_ctx[None, :], c, jnp.zeros((8 - 1 - DEC_BATCH, D_MODEL), F32)], axis=0)
    mods = _modulations(cond8, w_mod, b_mod)
    m3 = mods[:, :N_GROUPS].reshape(DEPTH, N_GROUPS, 6, D_MODEL)
    pack = jnp.concatenate([
        m3,
        jnp.broadcast_to(norm_mix[:, None, None, :], (DEPTH, N_GROUPS, 1, D_MODEL)),
        jnp.broadcast_to(norm_ffn[:, None, None, :], (DEPTH, N_GROUPS, 1, D_MODEL))], axis=2)

    cos_a, sin_a = _rope_tables(HEAD_DIM)
    cos_attn = jnp.tile(cos_a, (1, LANES // (HEAD_DIM // 2)))
    sin_attn = jnp.tile(jnp.concatenate([-sin_a, sin_a], axis=1), (1, LANES // HEAD_DIM))
    cos_ret, sin_ret = _rope_tables(RET_DK)

    kvw = N_KV_HEADS * HEAD_DIM
    cache_k4 = cache_k.reshape(DEC_BATCH, N_EVEN, PAST_LEN, kvw)
    cache_v4 = cache_v.reshape(DEC_BATCH, N_EVEN, PAST_LEN, kvw)
    sink_flat = attn_sink.reshape(-1)
    log_gamma = jax.nn.log_sigmoid(ret_decay.astype(F32)).reshape(-1)
    w_router_pad = jnp.pad(w_router, ((0, 0), (0, 0), (0, LANES - N_EXPERTS)))
    w_router_hi = w_router_pad.astype(BF16)
    w_router_lo = (w_router_pad - w_router_hi.astype(F32)).astype(BF16)

    mix_rows = (R_G_MIX, R_SCALE_MIX, R_SHIFT_MIX)
    ffn_rows = (R_G_FFN, R_SCALE_FFN, R_SHIFT_FFN)

    new_k, new_v = [], []
    new_sf = new_sb = None
    h = _pre(x, pack)
    y_prompt = y_sample = None
    for i in range(DEPTH):
        j = i // 2
        if i % 2 == 0:
            q = _attn_proj(h, w_qkv, j, cos_attn, sin_attn, col0=0, ncols=N_HEADS * HEAD_DIM, tn=512,
                           rope_tiles=2, scale=HEAD_DIM ** -0.5, out_dtype=BF16)
            kv = _attn_proj(h, w_qkv, j, cos_attn, sin_attn, col0=N_HEADS * HEAD_DIM, ncols=2 * kvw, tn=kvw,
                            rope_tiles=1, scale=1.0, out_dtype=F32)
            new_k.append(kv[:N_PROMPT, :kvw].reshape(BATCH, SEQ, N_KV_HEADS, HEAD_DIM))
            new_v.append(kv[:N_PROMPT, kvw:].reshape(BATCH, SEQ, N_KV_HEADS, HEAD_DIM))
            o_ctx = _ctx_attention(q, kv, sink_flat, j)
            o_lat = _lat_attention(q, kv, cache_k4, cache_v4, sink_flat, j)
            x, h = _down([o_ctx, o_lat], w_attn_o, j, x, pack, i, R_GATE_MIX, i, ffn_rows)
            act = _swiglu_up(h, w_ffn_gu, j)
            x, h = _down([act], w_ffn_down, j, x, pack, i, R_GATE_FFN, i + 1, mix_rows)
        else:
            proj = _ret_proj(h, w_ret_in, j, cos_ret, sin_ret)
            part, new_sf = _retention_pass(proj, state_fwd, log_gamma, j, False, states=new_sf)
            o_ret, new_sb = _retention_pass(proj, state_bwd, log_gamma, j, True, partial=part, states=new_sb)
            x, h, hf, route = _down([o_ret], w_ret_out, j, x, pack, i, R_GATE_MIX, i, ffn_rows,
                                    w_router=(w_router_hi, w_router_lo), router_layer=j)
            slot, pend, tile_expert, n_valid = _routing_tables(route)
            hs = _dispatch_rows(slot, pend, hf)
            act = _moe_up(tile_expert, n_valid, hs, w_exp_gu, j)
            ys = _moe_down(tile_expert, n_valid, act, w_exp_down, j)
            if i == DEPTH - 1:
                y_prompt = _combine(slot, ys, x, route, pack, i, i, final_norm, 0, N_PROMPT, True)
                y_sample = _combine(slot, ys, x, route, pack, i, i, final_norm, N_PROMPT, N_SAMPLE, True)
            else:
                x, h = _combine(slot, ys, x, route, pack, i, i + 1, final_norm, 0, N_TOK, False)

    return (y_prompt.reshape(BATCH, SEQ, D_MODEL), y_sample.reshape(DEC_BATCH, DEC_SEQ, D_MODEL),
            jnp.stack(new_k, axis=1), jnp.stack(new_v, axis=1), new_sf, new_sb)
```

```python
import functools

import jax
import jax.numpy as jnp
from jax import lax
from jax.experimental import pallas as pl
from jax.experimental.pallas import tpu as pltpu

F32 = jnp.float32
BF16 = jnp.bfloat16
I32 = jnp.int32

D_MODEL = 1024
BATCH = 16
SEQ = 256
DEPTH = 4
DEC_BATCH = 2
DEC_SEQ = 4096
PAST_LEN = 256
GRID_W = 64
BLOCK = 128
EPS = 1e-6
N_HEADS = 16
N_KV_HEADS = 4
HEAD_DIM = 64
GROUP = N_HEADS // N_KV_HEADS
WINDOW = 128
ROPE_BASE = 10000.0
RET_HEADS = 4
RET_DK = 256
RET_DV = 512
RET_HK = RET_HEADS * RET_DK
RET_HV = RET_HEADS * RET_DV
D_FF = 2816
N_EXPERTS = 8
TOP_K = 2
D_FF_EXPERT = 3584
N_EVEN = 2
N_ODD = 2
NEG = -1e30

GROUP_TOKENS = 4096
N_PROMPT = BATCH * SEQ
N_SAMPLE = DEC_BATCH * DEC_SEQ
N_TOK = N_PROMPT + N_SAMPLE
N_GROUPS = N_TOK // GROUP_TOKENS

R_SHIFT_MIX, R_SCALE_MIX, R_GATE_MIX, R_SHIFT_FFN, R_SCALE_FFN, R_GATE_FFN, R_G_MIX, R_G_FFN = range(8)

VMEM_LIMIT_BYTES = 56 * 1024 * 1024
LANES = 128

TM = 2048
UP_CHUNK = 512
TM_DOWN = 512
DOWN_CHUNK = 128
MOE_TILE = 512
MOE_CHUNK = 256
MOE_SLOTS = TOP_K * N_TOK + N_EXPERTS * MOE_TILE
MOE_NT = MOE_SLOTS // MOE_TILE
TC_COMBINE = 256
DISPATCH_ROWS = 256


def _cparams(sem):
    return pltpu.CompilerParams(dimension_semantics=sem, vmem_limit_bytes=VMEM_LIMIT_BYTES)


def _normmod(x, g, scale, shift):
    ms = jnp.mean(x * x, axis=-1, keepdims=True)
    return (x * lax.rsqrt(ms + EPS) * g) * (1.0 + scale) + shift


def _mod_kernel(cond_ref, w_ref, b_ref, o_ref):
    c = cond_ref[...]
    s = c * jax.nn.sigmoid(c)
    o_ref[0] = jnp.dot(s.astype(BF16), w_ref[0].astype(BF16), preferred_element_type=F32) + b_ref[0]


def _modulations(cond8, w_mod, b_mod):
    tn = 2048
    n6 = 6 * D_MODEL
    return pl.pallas_call(
        _mod_kernel,
        grid=(DEPTH, n6 // tn),
        in_specs=[pl.BlockSpec((8, D_MODEL), lambda l, n: (0, 0)),
                  pl.BlockSpec((1, D_MODEL, tn), lambda l, n: (l, 0, n)),
                  pl.BlockSpec((1, 1, tn), lambda l, n: (l, 0, n))],
        out_specs=pl.BlockSpec((1, 8, tn), lambda l, n: (l, 0, n)),
        out_shape=jax.ShapeDtypeStruct((DEPTH, 8, n6), F32),
        compiler_params=_cparams(("arbitrary", "arbitrary")),
        name="modulations",
    )(cond8, w_mod, b_mod.reshape(DEPTH, 1, n6))


def _split_specs(tm, width):
    fst = N_PROMPT // tm
    return [pl.BlockSpec((tm, width), lambda m: (jnp.minimum(m, fst - 1), 0)),
            pl.BlockSpec((tm, width), lambda m: (jnp.maximum(m - fst, 0), 0))]


def _pre_kernel(xp_ref, xs_ref, pk_ref, h_ref, *, first_sample_tile):
    def norm(x_ref):
        pk = pk_ref[0, 0]
        h = _normmod(x_ref[...], pk[R_G_MIX:R_G_MIX + 1], pk[R_SCALE_MIX:R_SCALE_MIX + 1],
                     pk[R_SHIFT_MIX:R_SHIFT_MIX + 1])
        h_ref[...] = h.astype(BF16)

    m = pl.program_id(0)
    pl.when(m < first_sample_tile)(lambda: norm(xp_ref))
    pl.when(m >= first_sample_tile)(lambda: norm(xs_ref))


def _pre(x_pair, pack):
    tm = TM
    return pl.pallas_call(
        functools.partial(_pre_kernel, first_sample_tile=N_PROMPT // tm),
        grid=(N_TOK // tm,),
        in_specs=_split_specs(tm, D_MODEL) + [
            pl.BlockSpec((1, 1, 8, D_MODEL), lambda m: (0, (m * tm) // GROUP_TOKENS, 0, 0))],
        out_specs=pl.BlockSpec((tm, D_MODEL), lambda m: (m, 0)),
        out_shape=jax.ShapeDtypeStruct((N_TOK, D_MODEL), BF16),
        compiler_params=_cparams(("arbitrary",)),
        name="pre_norm",
    )(*x_pair, pack)


UP_GROUP = 2048
N_UP_GROUPS = N_TOK // UP_GROUP
PROMPT_UP_GROUPS = N_PROMPT // UP_GROUP
CHUNKS_PER_GROUP = UP_GROUP // UP_CHUNK
assert DEC_SEQ % UP_GROUP == 0 and N_PROMPT % UP_GROUP == 0


def _token_group_copy(h_hbm, h_vmem, sems, g):
    rows = pl.ds(g * UP_GROUP, UP_GROUP)
    return pltpu.make_async_copy(h_hbm.at[rows], h_vmem.at[rows], sems.at[g])


def _for_token_chunks(h_hbm, h_vmem, sems, body):
    first = pl.program_id(0) == 0

    @pl.when(first)
    def _():
        for g in range(N_UP_GROUPS):
            _token_group_copy(h_hbm, h_vmem, sems, g).start()

    for g in range(N_UP_GROUPS):
        pl.when(first)(lambda g=g: _token_group_copy(h_hbm, h_vmem, sems, g).wait())
        is_sample = g >= PROMPT_UP_GROUPS
        table0 = (g * UP_GROUP) % DEC_SEQ if is_sample else 0

        def step(c, carry, g=g, table0=table0, is_sample=is_sample):
            off = pl.multiple_of(c * UP_CHUNK, UP_CHUNK)
            body(pl.ds(g * UP_GROUP + off, UP_CHUNK), pl.ds(table0 + off, UP_CHUNK), is_sample)
            return carry

        lax.fori_loop(0, CHUNKS_PER_GROUP, step, 0, unroll=2)


def _rope64(y, cos, sin_signed):
    width = y.shape[-1]
    lane = lax.broadcasted_iota(I32, y.shape, 1)
    first = (lane % HEAD_DIM) < (HEAD_DIM // 2)
    swapped = jnp.where(first, pltpu.roll(y, width - HEAD_DIM // 2, 1), pltpu.roll(y, HEAD_DIM // 2, 1))
    reps = width // LANES
    c = jnp.concatenate([cos] * reps, axis=1) if reps > 1 else cos
    s = jnp.concatenate([sin_signed] * reps, axis=1) if reps > 1 else sin_signed
    return y * c + swapped * s


def _up_scratch(n_weights, tn):
    return ([pltpu.VMEM((N_TOK, D_MODEL), BF16)] + [pltpu.VMEM((D_MODEL, tn), BF16)] * n_weights
            + [pltpu.SemaphoreType.DMA((N_UP_GROUPS,))])


def _attn_proj_kernel(h_hbm, w_ref, cos_ref, sin_ref, o_ref, h_vmem, wb_ref, sems, *, scale, rope_tiles):
    wb_ref[...] = w_ref[0].astype(BF16)

    def project(rows, table_rows, is_sample, rope):
        y = jnp.dot(h_vmem[rows, :], wb_ref[...], preferred_element_type=F32)
        if scale != 1.0:
            y = y * scale
        if rope and is_sample:
            y = _rope64(y, cos_ref[table_rows, :], sin_ref[table_rows, :])
        o_ref[rows, :] = y.astype(o_ref.dtype)

    do_rope = pl.program_id(0) < rope_tiles
    pl.when(do_rope)(lambda: _for_token_chunks(h_hbm, h_vmem, sems, functools.partial(project, rope=True)))
    pl.when(jnp.logical_not(do_rope))(
        lambda: _for_token_chunks(h_hbm, h_vmem, sems, functools.partial(project, rope=False)))


def _attn_proj(h, w_qkv, layer, cos, sin, *, col0, ncols, tn, rope_tiles, scale, out_dtype):
    cb0 = col0 // tn
    kern = functools.partial(_attn_proj_kernel, scale=scale, rope_tiles=rope_tiles)
    return pl.pallas_call(
        kern,
        grid=(ncols // tn,),
        in_specs=[pl.BlockSpec(memory_space=pl.ANY),
                  pl.BlockSpec((1, D_MODEL, tn), lambda n: (layer, 0, cb0 + n)),
                  pl.BlockSpec((DEC_SEQ, LANES), lambda n: (0, 0)),
                  pl.BlockSpec((DEC_SEQ, LANES), lambda n: (0, 0))],
        out_specs=pl.BlockSpec((N_TOK, tn), lambda n: (0, n)),
        out_shape=jax.ShapeDtypeStruct((N_TOK, ncols), out_dtype),
        scratch_shapes=_up_scratch(1, tn),
        compiler_params=_cparams(("arbitrary",)),
        name="attn_proj",
    )(h, w_qkv, cos, sin)


RET_TN = RET_DK


def _ret_proj_kernel(h_hbm, w_ref, cos_ref, sin_ref, o_ref, h_vmem, wb_ref, sems):
    n = pl.program_id(0)
    wb_ref[...] = w_ref[0].astype(BF16)
    k_tiles_lo = RET_HK // RET_TN
    qk_tiles = 2 * RET_HK // RET_TN
    is_k = jnp.logical_and(n >= k_tiles_lo, n < qk_tiles)
    k_scale = jnp.where(is_k, RET_DK ** -0.5, 1.0).astype(F32)
    half = RET_DK // 2

    def project(rows, table_rows, is_sample, rope):
        y = jnp.dot(h_vmem[rows, :], wb_ref[...], preferred_element_type=F32) * k_scale
        if rope and is_sample:
            c = cos_ref[table_rows, :]
            s = sin_ref[table_rows, :]
            x1 = y[:, :half]
            x2 = y[:, half:]
            y = jnp.concatenate([x1 * c - x2 * s, x1 * s + x2 * c], axis=1)
        o_ref[rows, :] = y.astype(BF16)

    do_rope = n < qk_tiles
    pl.when(do_rope)(lambda: _for_token_chunks(h_hbm, h_vmem, sems, functools.partial(project, rope=True)))
    pl.when(jnp.logical_not(do_rope))(
        lambda: _for_token_chunks(h_hbm, h_vmem, sems, functools.partial(project, rope=False)))


def _ret_proj(h, w_ret_in, layer, cos, sin):
    ncols = 2 * RET_HK + 3 * RET_HV
    return pl.pallas_call(
        _ret_proj_kernel,
        grid=(ncols // RET_TN,),
        in_specs=[pl.BlockSpec(memory_space=pl.ANY),
                  pl.BlockSpec((1, D_MODEL, RET_TN), lambda n: (layer, 0, n)),
                  pl.BlockSpec((DEC_SEQ, LANES), lambda n: (0, 0)),
                  pl.BlockSpec((DEC_SEQ, LANES), lambda n: (0, 0))],
        out_specs=pl.BlockSpec((N_TOK, RET_TN), lambda n: (0, n)),
        out_shape=jax.ShapeDtypeStruct((N_TOK, ncols), BF16),
        scratch_shapes=_up_scratch(1, RET_TN),
        compiler_params=_cparams(("arbitrary",)),
        name="ret_proj",
    )(h, w_ret_in, cos, sin)


def _swiglu_kernel(h_hbm, wg_ref, wu_ref, o_ref, h_vmem, wgb_ref, wub_ref, sems):
    wgb_ref[...] = wg_ref[0].astype(BF16)
    wub_ref[...] = wu_ref[0].astype(BF16)

    def project(rows, table_rows, is_sample):
        h = h_vmem[rows, :]
        g = jnp.dot(h, wgb_ref[...], preferred_element_type=F32)
        u = jnp.dot(h, wub_ref[...], preferred_element_type=F32)
        o_ref[rows, :] = (g * jax.nn.sigmoid(g) * u).astype(BF16)

    _for_token_chunks(h_hbm, h_vmem, sems, project)


def _swiglu_up(h, w_gu, layer):
    tf = 256
    nf = D_FF // tf
    return pl.pallas_call(
        _swiglu_kernel,
        grid=(nf,),
        in_specs=[pl.BlockSpec(memory_space=pl.ANY),
                  pl.BlockSpec((1, D_MODEL, tf), lambda f: (layer, 0, f)),
                  pl.BlockSpec((1, D_MODEL, tf), lambda f: (layer, 0, nf + f))],
        out_specs=pl.BlockSpec((N_TOK, tf), lambda f: (0, f)),
        out_shape=jax.ShapeDtypeStruct((N_TOK, D_FF), BF16),
        scratch_shapes=_up_scratch(2, tf),
        compiler_params=_cparams(("arbitrary",)),
        name="swiglu_up",
    )(h, w_gu, w_gu)


def _route(logits):
    lane = lax.broadcasted_iota(I32, logits.shape, 1).astype(F32)
    lg = jnp.where(lane < N_EXPERTS, logits, -jnp.inf)
    m1 = jnp.max(lg, axis=-1, keepdims=True)
    i1 = jnp.min(jnp.where(lg == m1, lane, float(LANES)), axis=-1, keepdims=True)
    lg2 = jnp.where(lane == i1, -jnp.inf, lg)
    m2 = jnp.max(lg2, axis=-1, keepdims=True)
    i2 = jnp.min(jnp.where(lg2 == m2, lane, float(LANES)), axis=-1, keepdims=True)
    e2 = jnp.exp(m2 - m1)
    p1 = 1.0 / (1.0 + e2)
    p2 = e2 / (1.0 + e2)
    out = jnp.where(lane == 0, i1, 0.0)
    out = jnp.where(lane == 1, i2, out)
    out = jnp.where(lane == 2, p1, out)
    out = jnp.where(lane == 3, p2, out)
    return out


def _epilogue(x_new, pk_next, rows, fin_ref, final):
    if final:
        ms = jnp.mean(x_new * x_new, axis=-1, keepdims=True)
        return x_new * lax.rsqrt(ms + EPS) * fin_ref[...]
    g_row, sc_row, sh_row = rows
    return _normmod(x_new, pk_next[g_row:g_row + 1], pk_next[sc_row:sc_row + 1], pk_next[sh_row:sh_row + 1])


def _down_kernel(*refs, n_a, n_x, first_sample_tile, gate_row, next_rows, router):
    a_refs = refs[:n_a]
    w_ref = refs[n_a]
    x_refs = refs[n_a + 1:n_a + 1 + n_x]
    pk_ref, pkn_ref = refs[n_a + 1 + n_x:n_a + 3 + n_x]
    pos = n_a + 3 + n_x
    if router:
        wrh_ref, wrl_ref = refs[pos:pos + 2]
        pos += 2
        xo_ref, h_ref, hf_ref, rt_ref, wb_ref = refs[pos:pos + 5]
    else:
        xo_ref, h_ref, wb_ref = refs[pos:pos + 3]
    m = pl.program_id(0)

    @pl.when(m == 0)
    def _():
        wb_ref[...] = w_ref[0].astype(BF16)

    def finish(a_ref, x_ref):
        pk = pk_ref[0, 0]
        pkn = pkn_ref[0, 0]
        for c in range(a_ref.shape[0] // DOWN_CHUNK):
            rows = pl.ds(c * DOWN_CHUNK, DOWN_CHUNK)
            y = jnp.dot(a_ref[rows, :], wb_ref[...], preferred_element_type=F32)
            x_new = x_ref[rows, :] + pk[gate_row:gate_row + 1] * y
            xo_ref[rows, :] = x_new
            hn = _epilogue(x_new, pkn, next_rows, None, False)
            hi = hn.astype(BF16)
            h_ref[rows, :] = hi
            if router:
                hf_ref[rows, :] = hn
                lo = (hn - hi.astype(F32)).astype(BF16)
                logits = (jnp.dot(hi, wrh_ref[0], preferred_element_type=F32)
                          + jnp.dot(lo, wrh_ref[0], preferred_element_type=F32)
                          + jnp.dot(hi, wrl_ref[0], preferred_element_type=F32))
                rt_ref[rows, :] = _route(logits)

    if n_a == 1 and n_x == 1:
        finish(a_refs[0], x_refs[0])
    else:
        pl.when(m < first_sample_tile)(lambda: finish(a_refs[0], x_refs[0]))
        pl.when(m >= first_sample_tile)(lambda: finish(a_refs[-1], x_refs[-1]))


def _down(a_list, w, layer, x, pack, pack_layer, gate_row, next_layer, next_rows, w_router=None, router_layer=0):
    tm = TM_DOWN
    kd = w.shape[1]
    n_a = len(a_list)
    fst = N_PROMPT // tm
    router = w_router is not None
    x_list = list(x) if isinstance(x, (list, tuple)) else [x]
    n_x = len(x_list)
    a_specs = [pl.BlockSpec((tm, kd), lambda m: (m, 0))] if n_a == 1 else _split_specs(tm, kd)
    x_specs = [pl.BlockSpec((tm, D_MODEL), lambda m: (m, 0))] if n_x == 1 else _split_specs(tm, D_MODEL)
    in_specs = a_specs + [pl.BlockSpec((1, kd, D_MODEL), lambda m: (layer, 0, 0))] + x_specs + [
        pl.BlockSpec((1, 1, 8, D_MODEL), lambda m: (pack_layer, (m * tm) // GROUP_TOKENS, 0, 0)),
        pl.BlockSpec((1, 1, 8, D_MODEL), lambda m: (next_layer, (m * tm) // GROUP_TOKENS, 0, 0)),
    ]
    args = list(a_list) + [w] + list(x_list) + [pack, pack]
    out_specs = [pl.BlockSpec((tm, D_MODEL), lambda m: (m, 0)),
                 pl.BlockSpec((tm, D_MODEL), lambda m: (m, 0))]
    out_shape = [jax.ShapeDtypeStruct((N_TOK, D_MODEL), F32),
                 jax.ShapeDtypeStruct((N_TOK, D_MODEL), BF16)]
    if router:
        in_specs += [pl.BlockSpec((1, D_MODEL, LANES), lambda m: (router_layer, 0, 0)),
                     pl.BlockSpec((1, D_MODEL, LANES), lambda m: (router_layer, 0, 0))]
        args += list(w_router)
        out_specs += [pl.BlockSpec((tm, D_MODEL), lambda m: (m, 0)),
                      pl.BlockSpec((tm, LANES), lambda m: (m, 0))]
        out_shape += [jax.ShapeDtypeStruct((N_TOK, D_MODEL), F32),
                      jax.ShapeDtypeStruct((N_TOK, LANES), F32)]
    kern = functools.partial(_down_kernel, n_a=n_a, n_x=n_x, first_sample_tile=fst, gate_row=gate_row,
                             next_rows=next_rows, router=router)
    return pl.pallas_call(
        kern,
        grid=(N_TOK // tm,),
        in_specs=in_specs,
        out_specs=out_specs,
        out_shape=out_shape,
        scratch_shapes=[pltpu.VMEM((kd, D_MODEL), BF16)],
        compiler_params=_cparams(("arbitrary",)),
        name="down_proj",
    )(*args)


def _sink_row(sink_ref, layer, kvh, width):
    return jnp.concatenate(
        [jnp.full((1, width), sink_ref[layer * N_HEADS + kvh * GROUP + g], F32) for g in range(GROUP)], axis=1)


def _group_queries(q_ref, kvh):
    return jnp.concatenate(
        [q_ref[:, (kvh * GROUP + g) * HEAD_DIM:(kvh * GROUP + g + 1) * HEAD_DIM] for g in range(GROUP)], axis=0)


_NT = (((1,), (1,)), ((), ()))


def _ctx_attn_kernel(sink_ref, q_ref, kv_ref, o_ref, *, layer):
    kvw = N_KV_HEADS * HEAD_DIM
    k_all = kv_ref[:, :kvw].astype(BF16)
    v_t = kv_ref[:, kvw:].T.astype(BF16)
    outs = []
    for kvh in range(N_KV_HEADS):
        hs = slice(kvh * HEAD_DIM, (kvh + 1) * HEAD_DIM)
        s_t = lax.dot_general(k_all[:, hs], _group_queries(q_ref, kvh), _NT, preferred_element_type=F32)
        sink = _sink_row(sink_ref, layer, kvh, SEQ)
        m = jnp.maximum(jnp.max(s_t, axis=0, keepdims=True), sink)
        p = jnp.exp(s_t - m)
        denom = jnp.sum(p, axis=0, keepdims=True) + jnp.exp(sink - m)
        o_t = jnp.dot(v_t[hs], p.astype(BF16), preferred_element_type=F32) / denom
        outs += [o_t[:, g * SEQ:(g + 1) * SEQ] for g in range(GROUP)]
    o_ref[...] = jnp.concatenate(outs, axis=0).T.astype(BF16)


def _ctx_attention(q, kv, sink_flat, layer):
    kern = functools.partial(_ctx_attn_kernel, layer=layer)
    return pl.pallas_call(
        kern,
        grid_spec=pltpu.PrefetchScalarGridSpec(
            num_scalar_prefetch=1,
            grid=(BATCH,),
            in_specs=[pl.BlockSpec((SEQ, D_MODEL), lambda b, s: (b, 0)),
                      pl.BlockSpec((SEQ, 2 * N_KV_HEADS * HEAD_DIM), lambda b, s: (b, 0))],
            out_specs=pl.BlockSpec((SEQ, D_MODEL), lambda b, s: (b, 0))),
        out_shape=jax.ShapeDtypeStruct((N_PROMPT, D_MODEL), BF16),
        compiler_params=_cparams(("arbitrary",)),
        name="ctx_attention",
    )(sink_flat, q, kv)


def _lat_attn_kernel(sink_ref, q_ref, kvp_ref, kvc_ref, kvn_ref, ck_ref, cv_ref, o_ref, kc_ref, vct_ref, *, layer):
    blk = pl.program_id(1)
    kvw = N_KV_HEADS * HEAD_DIM
    cols = GROUP * BLOCK
    n_lat = 3 * BLOCK

    @pl.when(blk == 0)
    def _():
        kc_ref[...] = ck_ref[0, 0].astype(BF16)
        vct_ref[...] = cv_ref[0, 0].T.astype(BF16)

    j = lax.broadcasted_iota(I32, (n_lat, cols), 0)
    r = lax.broadcasted_iota(I32, (n_lat, cols), 1) % BLOCK
    lo = jnp.maximum(r, BLOCK - BLOCK * blk)
    hi = jnp.minimum(r + 2 * WINDOW, DEC_SEQ + BLOCK - 1 - BLOCK * blk)
    lat_mask = jnp.logical_and(j >= lo, j <= hi)
    k_lat = jnp.concatenate([kvp_ref[:, :kvw], kvc_ref[:, :kvw], kvn_ref[:, :kvw]], axis=0).astype(BF16)
    v_lat_t = jnp.concatenate([kvp_ref[:, kvw:].T, kvc_ref[:, kvw:].T, kvn_ref[:, kvw:].T], axis=1).astype(BF16)
    outs = []
    for kvh in range(N_KV_HEADS):
        hs = slice(kvh * HEAD_DIM, (kvh + 1) * HEAD_DIM)
        q4 = _group_queries(q_ref, kvh)
        s_ctx = lax.dot_general(kc_ref[:, hs], q4, _NT, preferred_element_type=F32)
        s_lat = lax.dot_general(k_lat[:, hs], q4, _NT, preferred_element_type=F32)
        s_lat = jnp.where(lat_mask, s_lat, NEG)
        sink = _sink_row(sink_ref, layer, kvh, BLOCK)
        m = jnp.maximum(jnp.maximum(jnp.max(s_ctx, axis=0, keepdims=True),
                                    jnp.max(s_lat, axis=0, keepdims=True)), sink)
        p_ctx = jnp.exp(s_ctx - m)
        p_lat = jnp.exp(s_lat - m)
        denom = (jnp.sum(p_ctx, axis=0, keepdims=True) + jnp.sum(p_lat, axis=0, keepdims=True)
                 + jnp.exp(sink - m))
        o_t = (jnp.dot(vct_ref[hs, :], p_ctx.astype(BF16), preferred_element_type=F32)
               + jnp.dot(v_lat_t[hs], p_lat.astype(BF16), preferred_element_type=F32)) / denom
        outs += [o_t[:, g * BLOCK:(g + 1) * BLOCK] for g in range(GROUP)]
    o_ref[...] = jnp.concatenate(outs, axis=0).T.astype(BF16)


def _lat_attention(q, kv, cache_k4, cache_v4, sink_flat, layer):
    nb = DEC_SEQ // BLOCK
    base = N_PROMPT // BLOCK
    kvc = 2 * N_KV_HEADS * HEAD_DIM

    def row(b, i):
        return base + b * nb + i

    kern = functools.partial(_lat_attn_kernel, layer=layer)
    return pl.pallas_call(
        kern,
        grid_spec=pltpu.PrefetchScalarGridSpec(
            num_scalar_prefetch=1,
            grid=(DEC_BATCH, nb),
            in_specs=[pl.BlockSpec((BLOCK, D_MODEL), lambda b, i, s: (row(b, i), 0)),
                      pl.BlockSpec((BLOCK, kvc), lambda b, i, s: (row(b, jnp.maximum(i - 1, 0)), 0)),
                      pl.BlockSpec((BLOCK, kvc), lambda b, i, s: (row(b, i), 0)),
                      pl.BlockSpec((BLOCK, kvc), lambda b, i, s: (row(b, jnp.minimum(i + 1, nb - 1)), 0)),
                      pl.BlockSpec((1, 1, PAST_LEN, N_KV_HEADS * HEAD_DIM), lambda b, i, s: (b, layer, 0, 0)),
                      pl.BlockSpec((1, 1, PAST_LEN, N_KV_HEADS * HEAD_DIM), lambda b, i, s: (b, layer, 0, 0))],
            out_specs=pl.BlockSpec((BLOCK, D_MODEL), lambda b, i, s: (b * nb + i, 0)),
            scratch_shapes=[pltpu.VMEM((PAST_LEN, N_KV_HEADS * HEAD_DIM), BF16),
                            pltpu.VMEM((N_KV_HEADS * HEAD_DIM, PAST_LEN), BF16)]),
        out_shape=jax.ShapeDtypeStruct((N_SAMPLE, D_MODEL), BF16),
        compiler_params=_cparams(("arbitrary", "arbitrary")),
        name="lat_attention",
    )(sink_flat, q, kv, kv, kv, cache_k4, cache_v4)


RET_SUB = 2
RET_ROWS = RET_SUB * BLOCK
RET_STEPS = N_TOK // RET_ROWS
PROMPT_STEPS = N_PROMPT // RET_ROWS
STEPS_PER_SAMPLE = DEC_SEQ // RET_ROWS
assert SEQ == RET_ROWS


def _ret_kernel(lg_ref, q_ref, k_ref, v_ref, gate_ref, s0_ref, *rest, backward, layer):
    part_ref = rest[0] if backward else None
    o_ref, sout_ref, state_ref = rest[-3:]
    step = pl.program_id(0)
    blk = (RET_STEPS - 1 - step) if backward else step
    in_prompt = blk < PROMPT_STEPS
    first_of_sample = (blk - PROMPT_STEPS) % STEPS_PER_SAMPLE == (STEPS_PER_SAMPLE - 1 if backward else 0)

    @pl.when(in_prompt)
    def _():
        state_ref[...] = jnp.zeros_like(state_ref)

    @pl.when(jnp.logical_and(jnp.logical_not(in_prompt), first_of_sample))
    def _():
        state_ref[...] = s0_ref[0, 0]

    ii = lax.broadcasted_iota(I32, (BLOCK, BLOCK), 0).astype(F32)
    jj = lax.broadcasted_iota(I32, (BLOCK, BLOCK), 1).astype(F32)
    dist = (jj - ii) if backward else (ii - jj)
    ti = lax.broadcasted_iota(I32, (BLOCK, 1), 0).astype(F32)
    q_pow = (BLOCK - ti) if backward else (ti + 1.0)
    k_pow = ti if backward else (BLOCK - 1.0 - ti)

    for h in range(RET_HEADS):
        lg = lg_ref[(layer * 2 + (1 if backward else 0)) * RET_HEADS + h]
        intra = jnp.where(dist >= 0, jnp.exp(lg * jnp.maximum(dist, 0.0)), 0.0)
        q_dec = jnp.exp(lg * q_pow)
        k_dec = jnp.exp(lg * k_pow)
        c_dec = jnp.exp(lg * BLOCK)
        for sub in (reversed(range(RET_SUB)) if backward else range(RET_SUB)):
            rows = pl.ds(sub * BLOCK, BLOCK)
            q = q_ref[rows, h * RET_DK:(h + 1) * RET_DK]
            k = k_ref[rows, h * RET_DK:(h + 1) * RET_DK]
            v = v_ref[rows, h * RET_DV:(h + 1) * RET_DV]
            s = state_ref[h]
            a = lax.dot_general(q, k, (((1,), (1,)), ((), ())), preferred_element_type=F32) * intra
            o = (jnp.dot(a.astype(BF16), v, preferred_element_type=F32)
                 + q_dec * jnp.dot(q, s.astype(BF16), preferred_element_type=F32))
            kd = (k.astype(F32) * k_dec).astype(BF16)
            state_ref[h] = c_dec * s + lax.dot_general(kd, v, (((0,), (0,)), ((), ())), preferred_element_type=F32)
            gate = gate_ref[rows, h * RET_DV:(h + 1) * RET_DV].astype(F32)
            on = o * lax.rsqrt(jnp.mean(o * o, axis=-1, keepdims=True) + EPS)
            res = on * (gate * jax.nn.sigmoid(gate))
            if backward:
                res = res + part_ref[rows, h * RET_DV:(h + 1) * RET_DV].astype(F32)
            o_ref[rows, h * RET_DV:(h + 1) * RET_DV] = res.astype(BF16)

    @pl.when(in_prompt)
    def _():
        if sout_ref.shape[1] == 1:
            sout_ref[0, 0] = state_ref[...]
        else:
            for l in range(sout_ref.shape[1]):
                sout_ref[0, l] = state_ref[...] if l == layer else jnp.zeros_like(state_ref)


def _retention_pass(proj, s0, log_gamma_flat, layer, backward, partial=None, states=None):
    def blk(i):
        return (RET_STEPS - 1 - i) if backward else i

    def s0_map(i, lg):
        return (jnp.clip((blk(i) - PROMPT_STEPS) // STEPS_PER_SAMPLE, 0, DEC_BATCH - 1), layer, 0, 0, 0)

    sout_layers = N_ODD if states is None else 1

    def sout_map(i, lg):
        return (jnp.minimum(blk(i), BATCH - 1), 0 if states is None else layer, 0, 0, 0)

    gate_block = 3 if backward else 2
    in_specs = [pl.BlockSpec((RET_ROWS, RET_HK), lambda i, lg: (blk(i), 0)),
                pl.BlockSpec((RET_ROWS, RET_HK), lambda i, lg: (blk(i), 1)),
                pl.BlockSpec((RET_ROWS, RET_HV), lambda i, lg: (blk(i), 1)),
                pl.BlockSpec((RET_ROWS, RET_HV), lambda i, lg: (blk(i), gate_block)),
                pl.BlockSpec((1, 1, RET_HEADS, RET_DK, RET_DV), s0_map)]
    args = [proj, proj, proj, proj, s0]
    if backward:
        in_specs.append(pl.BlockSpec((RET_ROWS, RET_HV), lambda i, lg: (blk(i), 0)))
        args.append(partial)
    aliases = {}
    if states is not None:
        in_specs.append(pl.BlockSpec(memory_space=pl.ANY))
        args.append(states)
        aliases = {len(args): 1}
    kern = functools.partial(_ret_kernel, backward=backward, layer=layer)
    return pl.pallas_call(
        kern,
        grid_spec=pltpu.PrefetchScalarGridSpec(
            num_scalar_prefetch=1,
            grid=(RET_STEPS,),
            in_specs=in_specs,
            out_specs=[pl.BlockSpec((RET_ROWS, RET_HV), lambda i, lg: (blk(i), 0)),
                       pl.BlockSpec((1, sout_layers, RET_HEADS, RET_DK, RET_DV), sout_map)],
            scratch_shapes=[pltpu.VMEM((RET_HEADS, RET_DK, RET_DV), F32)]),
        out_shape=[jax.ShapeDtypeStruct((N_TOK, RET_HV), BF16),
                   jax.ShapeDtypeStruct((BATCH, N_ODD, RET_HEADS, RET_DK, RET_DV), F32)],
        input_output_aliases=aliases,
        compiler_params=_cparams(("arbitrary",)),
        name="retention_bwd" if backward else "retention_fwd",
    )(log_gamma_flat, *args)


def _dispatch_kernel(slot_ref, pend_ref, h_ref, o_hbm, zbuf, sem):
    i = pl.program_id(0)

    @pl.when(i == 0)
    def _():
        zbuf[...] = jnp.zeros_like(zbuf)

        def tile_fill(start):
            return pltpu.make_async_copy(zbuf, o_hbm.at[pl.ds(pl.multiple_of(start, MOE_TILE), MOE_TILE)], sem)

        def nonempty(e):
            return pend_ref[e] > (pend_ref[e - 1] if e else 0)

        def start_unused(t, carry):
            tile_fill(t * MOE_TILE).start()
            return carry

        def wait_unused(t, carry):
            tile_fill(t * MOE_TILE).wait()
            return carry

        first_unused = pend_ref[N_EXPERTS - 1] // MOE_TILE
        for e in range(N_EXPERTS):
            pl.when(nonempty(e))(lambda e=e: tile_fill(pend_ref[e] - MOE_TILE).start())
        lax.fori_loop(first_unused, MOE_NT, start_unused, 0)
        for e in range(N_EXPERTS):
            pl.when(nonempty(e))(lambda e=e: tile_fill(pend_ref[e] - MOE_TILE).wait())
        lax.fori_loop(first_unused, MOE_NT, wait_unused, 0)

    base = i * DISPATCH_ROWS * TOP_K

    def start(r, carry):
        for k in range(TOP_K):
            pltpu.make_async_copy(h_ref.at[pl.ds(r, 1)], o_hbm.at[pl.ds(slot_ref[base + r * TOP_K + k], 1)],
                                  sem).start()
        return carry

    lax.fori_loop(0, DISPATCH_ROWS, start, 0, unroll=8)
    for k in range(TOP_K):
        pltpu.make_async_copy(h_ref, o_hbm.at[pl.ds(0, DISPATCH_ROWS)], sem).wait()


def _dispatch_rows(slot, pend, h):
    return pl.pallas_call(
        _dispatch_kernel,
        grid_spec=pltpu.PrefetchScalarGridSpec(
            num_scalar_prefetch=2,
            grid=(N_TOK // DISPATCH_ROWS,),
            in_specs=[pl.BlockSpec((DISPATCH_ROWS, D_MODEL), lambda i, s, p: (i, 0))],
            out_specs=pl.BlockSpec(memory_space=pl.ANY),
            scratch_shapes=[pltpu.VMEM((MOE_TILE, D_MODEL), F32), pltpu.SemaphoreType.DMA(())]),
        out_shape=jax.ShapeDtypeStruct((MOE_SLOTS, D_MODEL), F32),
        compiler_params=_cparams(("arbitrary",)),
        name="moe_dispatch",
    )(slot, pend, h)


def _moe_up_kernel(te_ref, nv_ref, h_ref, wg_ref, wu_ref, o_ref, wgb_ref, wub_ref):
    m = pl.program_id(1)
    new_w = jnp.logical_or(m == 0, te_ref[m] != te_ref[jnp.maximum(m - 1, 0)])

    @pl.when(new_w)
    def _():
        wgb_ref[...] = wg_ref[0, 0].astype(BF16)
        wub_ref[...] = wu_ref[0, 0].astype(BF16)

    @pl.when(m < nv_ref[0])
    def _():
        for c in range(MOE_TILE // MOE_CHUNK):
            rows = pl.ds(c * MOE_CHUNK, MOE_CHUNK)
            h = h_ref[rows, :].astype(BF16)
            g = jnp.dot(h, wgb_ref[...], preferred_element_type=F32)
            u = jnp.dot(h, wub_ref[...], preferred_element_type=F32)
            o_ref[rows, :] = (g * jax.nn.sigmoid(g) * u).astype(BF16)

    @pl.when(m >= nv_ref[0])
    def _():
        o_ref[...] = jnp.zeros_like(o_ref)


def _moe_up(tile_expert, n_valid, hs, w_gu, layer):
    tf = 1792
    nf = D_FF_EXPERT // tf
    return pl.pallas_call(
        _moe_up_kernel,
        grid_spec=pltpu.PrefetchScalarGridSpec(
            num_scalar_prefetch=2,
            grid=(nf, MOE_NT),
            in_specs=[pl.BlockSpec((MOE_TILE, D_MODEL), lambda f, m, te, nv: (jnp.minimum(m, nv[0] - 1), 0)),
                      pl.BlockSpec((1, 1, D_MODEL, tf), lambda f, m, te, nv: (layer, te[m], 0, f)),
                      pl.BlockSpec((1, 1, D_MODEL, tf), lambda f, m, te, nv: (layer, te[m], 0, nf + f))],
            out_specs=pl.BlockSpec((MOE_TILE, tf), lambda f, m, te, nv: (m, f)),
            scratch_shapes=[pltpu.VMEM((D_MODEL, tf), BF16), pltpu.VMEM((D_MODEL, tf), BF16)]),
        out_shape=jax.ShapeDtypeStruct((MOE_SLOTS, D_FF_EXPERT), BF16),
        compiler_params=_cparams(("arbitrary", "arbitrary")),
        name="moe_up",
    )(tile_expert, n_valid, hs, w_gu, w_gu)


def _moe_down_kernel(te_ref, nv_ref, a_ref, w_ref, o_ref, wb_ref):
    m = pl.program_id(1)
    new_w = jnp.logical_or(m == 0, te_ref[m] != te_ref[jnp.maximum(m - 1, 0)])

    @pl.when(new_w)
    def _():
        wb_ref[...] = w_ref[0, 0].astype(BF16)

    @pl.when(m < nv_ref[0])
    def _():
        o_ref[...] = jnp.dot(a_ref[...], wb_ref[...], preferred_element_type=F32)

    @pl.when(m >= nv_ref[0])
    def _():
        o_ref[...] = jnp.zeros_like(o_ref)


def _moe_down(tile_expert, n_valid, act, w_down, layer):
    tn = D_MODEL
    return pl.pallas_call(
        _moe_down_kernel,
        grid_spec=pltpu.PrefetchScalarGridSpec(
            num_scalar_prefetch=2,
            grid=(D_MODEL // tn, MOE_NT),
            in_specs=[pl.BlockSpec((MOE_TILE, D_FF_EXPERT), lambda n, m, te, nv: (jnp.minimum(m, nv[0] - 1), 0)),
                      pl.BlockSpec((1, 1, D_FF_EXPERT, tn), lambda n, m, te, nv: (layer, te[m], 0, n))],
            out_specs=pl.BlockSpec((MOE_TILE, tn), lambda n, m, te, nv: (m, n)),
            scratch_shapes=[pltpu.VMEM((D_FF_EXPERT, tn), BF16)]),
        out_shape=jax.ShapeDtypeStruct((MOE_SLOTS, D_MODEL), F32),
        compiler_params=_cparams(("arbitrary", "arbitrary")),
        name="moe_down",
    )(tile_expert, n_valid, act, w_down)


def _combine_kernel(slot_ref, y_hbm, x_ref, rt_ref, pk_ref, pkn_ref, fin_ref, *rest, tok0, final):
    if final:
        out_ref, ybuf, sem = rest
    else:
        xo_ref, h_ref, ybuf, sem = rest
    tc = TC_COMBINE
    base = (tok0 + pl.program_id(0) * tc) * TOP_K

    def start(r, carry):
        for k in range(TOP_K):
            pltpu.make_async_copy(y_hbm.at[pl.ds(slot_ref[base + r * TOP_K + k], 1)],
                                  ybuf.at[k, pl.ds(r, 1)], sem).start()
        return carry

    lax.fori_loop(0, tc, start, 0, unroll=8)
    for k in range(TOP_K):
        pltpu.make_async_copy(y_hbm.at[pl.ds(0, tc)], ybuf.at[k], sem).wait()

    rt = rt_ref[...]
    moe = rt[:, 2:3] * ybuf[0] + rt[:, 3:4] * ybuf[1]
    pk = pk_ref[0, 0]
    x_new = x_ref[...] + pk[R_GATE_FFN:R_GATE_FFN + 1] * moe
    if final:
        out_ref[...] = _epilogue(x_new, None, None, fin_ref, True)
    else:
        xo_ref[...] = x_new
        h_ref[...] = _epilogue(x_new, pkn_ref[0, 0], (R_G_MIX, R_SCALE_MIX, R_SHIFT_MIX), None, False).astype(BF16)


def _combine(slot, y, x, route, pack, layer, next_layer, final_norm, tok0, n_rows, final):
    tc = TC_COMBINE
    b0 = tok0 // tc

    def tmap(m, s):
        return (b0 + m, 0)

    def pmap(l):
        return lambda m, s: (l, (tok0 + m * tc) // GROUP_TOKENS, 0, 0)

    if final:
        out_specs = pl.BlockSpec((tc, D_MODEL), lambda m, s: (m, 0))
        out_shape = jax.ShapeDtypeStruct((n_rows, D_MODEL), F32)
    else:
        out_specs = [pl.BlockSpec((tc, D_MODEL), lambda m, s: (m, 0)),
                     pl.BlockSpec((tc, D_MODEL), lambda m, s: (m, 0))]
        out_shape = [jax.ShapeDtypeStruct((n_rows, D_MODEL), F32),
                     jax.ShapeDtypeStruct((n_rows, D_MODEL), BF16)]
    kern = functools.partial(_combine_kernel, tok0=tok0, final=final)
    return pl.pallas_call(
        kern,
        grid_spec=pltpu.PrefetchScalarGridSpec(
            num_scalar_prefetch=1,
            grid=(n_rows // tc,),
            in_specs=[pl.BlockSpec(memory_space=pl.ANY),
                      pl.BlockSpec((tc, D_MODEL), tmap),
                      pl.BlockSpec((tc, LANES), tmap),
                      pl.BlockSpec((1, 1, 8, D_MODEL), pmap(layer)),
                      pl.BlockSpec((1, 1, 8, D_MODEL), pmap(next_layer)),
                      pl.BlockSpec((1, D_MODEL), lambda m, s: (0, 0))],
            out_specs=out_specs,
            scratch_shapes=[pltpu.VMEM((TOP_K, tc, D_MODEL), F32), pltpu.SemaphoreType.DMA(())]),
        out_shape=out_shape,
        compiler_params=_cparams(("arbitrary",)),
        name="moe_combine",
    )(slot, y, x, route, pack, pack, final_norm.reshape(1, D_MODEL))


def _routing_tables(route):
    e_flat = route[:, :TOP_K].astype(I32).reshape(-1)
    onehot = (e_flat[:, None] == jnp.arange(N_EXPERTS, dtype=I32)[None, :]).astype(I32)
    csum = jnp.cumsum(onehot, axis=0)
    rank = jnp.sum(onehot * (csum - 1), axis=1)
    count = csum[-1]
    padded = ((count + MOE_TILE - 1) // MOE_TILE) * MOE_TILE
    pend = jnp.cumsum(padded)
    poff = pend - padded
    slot = jnp.sum(onehot * poff[None, :], axis=1) + rank
    n_valid = pend[-1] // MOE_TILE
    tile_start = jnp.arange(MOE_NT, dtype=I32) * MOE_TILE
    te_raw = jnp.minimum(jnp.sum((tile_start[:, None] >= pend[None, :]).astype(I32), axis=1), N_EXPERTS - 1)
    last_e = jnp.max(jnp.where(tile_start < pend[-1], te_raw, 0))
    tile_expert = jnp.minimum(te_raw, last_e)
    return slot.astype(I32), pend.astype(I32), tile_expert.astype(I32), n_valid.astype(I32).reshape(1)


def _rope_tables(dh):
    nf = dh // 4
    t = jnp.arange(DEC_SEQ)
    freqs = ROPE_BASE ** (-jnp.arange(nf, dtype=F32) / nf)
    row = (t // GRID_W).astype(F32)
    col = (t % GRID_W).astype(F32)
    ang = jnp.concatenate([row[:, None] * freqs, col[:, None] * freqs], axis=-1)
    return jnp.cos(ang), jnp.sin(ang)


def kernel(x_prompt, x_sample, cache_k, cache_v, state_fwd, state_bwd, c, c_ctx, norm_mix, norm_ffn, w_mod, b_mod,
           w_qkv, w_attn_o, attn_sink, w_ret_in, ret_decay, w_ret_out, w_ffn_gu, w_ffn_down, w_router, w_exp_gu,
           w_exp_down, final_norm):
    x = (x_prompt.reshape(N_PROMPT, D_MODEL), x_sample.reshape(N_SAMPLE, D_MODEL))

    cond8 = jnp.concatenate([c_ctx[None, :], c, jnp.zeros((8 - 1 - DEC_BATCH, D_MODEL), F32)], axis=0)
    mods = _modulations(cond8, w_mod, b_mod)
    m3 = mods[:, :N_GROUPS].reshape(DEPTH, N_GROUPS, 6, D_MODEL)
    pack = jnp.concatenate([
        m3,
        jnp.broadcast_to(norm_mix[:, None, None, :], (DEPTH, N_GROUPS, 1, D_MODEL)),
        jnp.broadcast_to(norm_ffn[:, None, None, :], (DEPTH, N_GROUPS, 1, D_MODEL))], axis=2)

    cos_a, sin_a = _rope_tables(HEAD_DIM)
    cos_attn = jnp.tile(cos_a, (1, LANES // (HEAD_DIM // 2)))
    sin_attn = jnp.tile(jnp.concatenate([-sin_a, sin_a], axis=1), (1, LANES // HEAD_DIM))
    cos_ret, sin_ret = _rope_tables(RET_DK)

    kvw = N_KV_HEADS * HEAD_DIM
    cache_k4 = cache_k.reshape(DEC_BATCH, N_EVEN, PAST_LEN, kvw)
    cache_v4 = cache_v.reshape(DEC_BATCH, N_EVEN, PAST_LEN, kvw)
    sink_flat = attn_sink.reshape(-1)
    log_gamma = jax.nn.log_sigmoid(ret_decay.astype(F32)).reshape(-1)
    w_router_pad = jnp.pad(w_router, ((0, 0), (0, 0), (0, LANES - N_EXPERTS)))
    w_router_hi = w_router_pad.astype(BF16)
    w_router_lo = (w_router_pad - w_router_hi.astype(F32)).astype(BF16)

    mix_rows = (R_G_MIX, R_SCALE_MIX, R_SHIFT_MIX)
    ffn_rows = (R_G_FFN, R_SCALE_FFN, R_SHIFT_FFN)

    new_k, new_v = [], []
    new_sf = new_sb = None
    h = _pre(x, pack)
    y_prompt = y_sample = None
    for i in range(DEPTH):
        j = i // 2
        if i % 2 == 0:
            q = _attn_proj(h, w_qkv, j, cos_attn, sin_attn, col0=0, ncols=N_HEADS * HEAD_DIM, tn=256,
                           rope_tiles=N_HEADS * HEAD_DIM // 256, scale=HEAD_DIM ** -0.5, out_dtype=BF16)
            kv = _attn_proj(h, w_qkv, j, cos_attn, sin_attn, col0=N_HEADS * HEAD_DIM, ncols=2 * kvw, tn=LANES,
                            rope_tiles=kvw // LANES, scale=1.0, out_dtype=F32)
            new_k.append(kv[:N_PROMPT, :kvw].reshape(BATCH, SEQ, N_KV_HEADS, HEAD_DIM))
            new_v.append(kv[:N_PROMPT, kvw:].reshape(BATCH, SEQ, N_KV_HEADS, HEAD_DIM))
            o_ctx = _ctx_attention(q, kv, sink_flat, j)
            o_lat = _lat_attention(q, kv, cache_k4, cache_v4, sink_flat, j)
            x, h = _down([o_ctx, o_lat], w_attn_o, j, x, pack, i, R_GATE_MIX, i, ffn_rows)
            act = _swiglu_up(h, w_ffn_gu, j)
            x, h = _down([act], w_ffn_down, j, x, pack, i, R_GATE_FFN, i + 1, mix_rows)
        else:
            proj = _ret_proj(h, w_ret_in, j, cos_ret, sin_ret)
            part, new_sf = _retention_pass(proj, state_fwd, log_gamma, j, False, states=new_sf)
            o_ret, new_sb = _retention_pass(proj, state_bwd, log_gamma, j, True, partial=part, states=new_sb)
            x, h, hf, route = _down([o_ret], w_ret_out, j, x, pack, i, R_GATE_MIX, i, ffn_rows,
                                    w_router=(w_router_hi, w_router_lo), router_layer=j)
            slot, pend, tile_expert, n_valid = _routing_tables(route)
            hs = _dispatch_rows(slot, pend, hf)
            act = _moe_up(tile_expert, n_valid, hs, w_exp_gu, j)
            ys = _moe_down(tile_expert, n_valid, act, w_exp_down, j)
            if i == DEPTH - 1:
                y_prompt = _combine(slot, ys, x, route, pack, i, i, final_norm, 0, N_PROMPT, True)
                y_sample = _combine(slot, ys, x, route, pack, i, i, final_norm, N_PROMPT, N_SAMPLE, True)
            else:
                x, h = _combine(slot, ys, x, route, pack, i, i + 1, final_norm, 0, N_TOK, False)

    return (y_prompt.reshape(BATCH, SEQ, D_MODEL), y_sample.reshape(DEC_BATCH, DEC_SEQ, D_MODEL),
            jnp.stack(new_k, axis=1), jnp.stack(new_v, axis=1), new_sf, new_sb)
```

```python
import functools

import jax
import jax.numpy as jnp
from jax import lax
from jax.experimental import pallas as pl
from jax.experimental.pallas import tpu as pltpu

F32 = jnp.float32
BF16 = jnp.bfloat16
I32 = jnp.int32

D_MODEL = 1024
BATCH = 16
SEQ = 256
DEPTH = 4
DEC_BATCH = 2
DEC_SEQ = 4096
PAST_LEN = 256
GRID_W = 64
BLOCK = 128
EPS = 1e-6
N_HEADS = 16
N_KV_HEADS = 4
HEAD_DIM = 64
GROUP = N_HEADS // N_KV_HEADS
WINDOW = 128
ROPE_BASE = 10000.0
RET_HEADS = 4
RET_DK = 256
RET_DV = 512
RET_HK = RET_HEADS * RET_DK
RET_HV = RET_HEADS * RET_DV
D_FF = 2816
N_EXPERTS = 8
TOP_K = 2
D_FF_EXPERT = 3584
N_EVEN = 2
N_ODD = 2
NEG = -1e30
LOG2E = 1.4426950408889634

GROUP_TOKENS = 4096
N_PROMPT = BATCH * SEQ
N_SAMPLE = DEC_BATCH * DEC_SEQ
N_TOK = N_PROMPT + N_SAMPLE
N_GROUPS = N_TOK // GROUP_TOKENS

R_SHIFT_MIX, R_SCALE_MIX, R_GATE_MIX, R_SHIFT_FFN, R_SCALE_FFN, R_GATE_FFN, R_G_MIX, R_G_FFN = range(8)

VMEM_LIMIT_BYTES = 56 * 1024 * 1024
LANES = 128

TM = 2048
UP_CHUNK = 512
TM_DOWN = 512
DOWN_CHUNK = 128
MOE_TILE = 512
MOE_CHUNK = 256
MOE_SLOTS = TOP_K * N_TOK + N_EXPERTS * MOE_TILE
MOE_NT = MOE_SLOTS // MOE_TILE
TC_COMBINE = 256
DISPATCH_ROWS = 256


def _cparams(sem):
    return pltpu.CompilerParams(dimension_semantics=sem, vmem_limit_bytes=VMEM_LIMIT_BYTES)


def _normmod(x, g, scale, shift):
    ms = jnp.mean(x * x, axis=-1, keepdims=True)
    return (x * lax.rsqrt(ms + EPS) * g) * (1.0 + scale) + shift


def _mod_kernel(cond_ref, w_ref, b_ref, o_ref):
    c = cond_ref[...]
    s = c * jax.nn.sigmoid(c)
    o_ref[0] = jnp.dot(s.astype(BF16), w_ref[0].astype(BF16), preferred_element_type=F32) + b_ref[0]


def _modulations(cond8, w_mod, b_mod):
    tn = 2048
    n6 = 6 * D_MODEL
    return pl.pallas_call(
        _mod_kernel,
        grid=(DEPTH, n6 // tn),
        in_specs=[pl.BlockSpec((8, D_MODEL), lambda l, n: (0, 0)),
                  pl.BlockSpec((1, D_MODEL, tn), lambda l, n: (l, 0, n)),
                  pl.BlockSpec((1, 1, tn), lambda l, n: (l, 0, n))],
        out_specs=pl.BlockSpec((1, 8, tn), lambda l, n: (l, 0, n)),
        out_shape=jax.ShapeDtypeStruct((DEPTH, 8, n6), F32),
        compiler_params=_cparams(("arbitrary", "arbitrary")),
        name="modulations",
    )(cond8, w_mod, b_mod.reshape(DEPTH, 1, n6))


def _split_specs(tm, width):
    fst = N_PROMPT // tm
    return [pl.BlockSpec((tm, width), lambda m: (jnp.minimum(m, fst - 1), 0)),
            pl.BlockSpec((tm, width), lambda m: (jnp.maximum(m - fst, 0), 0))]


def _pre_kernel(xp_ref, xs_ref, pk_ref, h_ref, *, first_sample_tile):
    def norm(x_ref):
        pk = pk_ref[0, 0]
        h = _normmod(x_ref[...], pk[R_G_MIX:R_G_MIX + 1], pk[R_SCALE_MIX:R_SCALE_MIX + 1],
                     pk[R_SHIFT_MIX:R_SHIFT_MIX + 1])
        h_ref[...] = h.astype(BF16)

    m = pl.program_id(0)
    pl.when(m < first_sample_tile)(lambda: norm(xp_ref))
    pl.when(m >= first_sample_tile)(lambda: norm(xs_ref))


def _pre(x_pair, pack):
    tm = TM
    return pl.pallas_call(
        functools.partial(_pre_kernel, first_sample_tile=N_PROMPT // tm),
        grid=(N_TOK // tm,),
        in_specs=_split_specs(tm, D_MODEL) + [
            pl.BlockSpec((1, 1, 8, D_MODEL), lambda m: (0, (m * tm) // GROUP_TOKENS, 0, 0))],
        out_specs=pl.BlockSpec((tm, D_MODEL), lambda m: (m, 0)),
        out_shape=jax.ShapeDtypeStruct((N_TOK, D_MODEL), BF16),
        compiler_params=_cparams(("arbitrary",)),
        name="pre_norm",
    )(*x_pair, pack)


def _row_chunks(n_rows):
    return [pl.ds(c * UP_CHUNK, UP_CHUNK) for c in range(n_rows // UP_CHUNK)]


def _table_map(tm):
    first_sample_tile = N_PROMPT // tm
    tiles_per_seq = DEC_SEQ // tm
    return lambda m: (jnp.maximum(m - first_sample_tile, 0) % tiles_per_seq, 0)


def _rope64(y, cos, sin_signed):
    width = y.shape[-1]
    lane = lax.broadcasted_iota(I32, y.shape, 1)
    first = (lane % HEAD_DIM) < (HEAD_DIM // 2)
    swapped = jnp.where(first, pltpu.roll(y, width - HEAD_DIM // 2, 1), pltpu.roll(y, HEAD_DIM // 2, 1))
    reps = width // LANES
    c = jnp.concatenate([cos] * reps, axis=1) if reps > 1 else cos
    s = jnp.concatenate([sin_signed] * reps, axis=1) if reps > 1 else sin_signed
    return y * c + swapped * s


def _attn_proj_kernel(h_ref, w_ref, cos_ref, sin_ref, o_ref, wb_ref, *, scale, rope_cols, first_sample_tile):
    m = pl.program_id(0)

    @pl.when(m == 0)
    def _():
        wb_ref[...] = w_ref[0].astype(BF16)

    def project(rope):
        for rows in _row_chunks(h_ref.shape[0]):
            y = jnp.dot(h_ref[rows, :], wb_ref[...], preferred_element_type=F32)
            if scale != 1.0:
                y = y * scale
            if rope:
                roped = _rope64(y[:, :rope_cols], cos_ref[rows, :], sin_ref[rows, :])
                y = roped if rope_cols == y.shape[1] else jnp.concatenate([roped, y[:, rope_cols:]], axis=1)
            o_ref[rows, :] = y.astype(o_ref.dtype)

    pl.when(m >= first_sample_tile)(lambda: project(True))
    pl.when(m < first_sample_tile)(lambda: project(False))


def _attn_proj(h, w_qkv, layer, cos, sin, *, col0, ncols, rope_cols, scale, out_dtype):
    tm = TM
    kern = functools.partial(_attn_proj_kernel, scale=scale, rope_cols=rope_cols, first_sample_tile=N_PROMPT // tm)
    return pl.pallas_call(
        kern,
        grid=(N_TOK // tm,),
        in_specs=[pl.BlockSpec((tm, D_MODEL), lambda m: (m, 0)),
                  pl.BlockSpec((1, D_MODEL, ncols), lambda m: (layer, 0, col0 // ncols)),
                  pl.BlockSpec((tm, LANES), _table_map(tm)),
                  pl.BlockSpec((tm, LANES), _table_map(tm))],
        out_specs=pl.BlockSpec((tm, ncols), lambda m: (m, 0)),
        out_shape=jax.ShapeDtypeStruct((N_TOK, ncols), out_dtype),
        scratch_shapes=[pltpu.VMEM((D_MODEL, ncols), BF16)],
        compiler_params=_cparams(("arbitrary",)),
        name="attn_proj",
    )(h, w_qkv, cos, sin)


RET_TN = RET_HK


def _ret_proj_kernel(h_ref, w_ref, cos_ref, sin_ref, o_ref, wb_ref, *, first_sample_tile):
    n = pl.program_id(0)
    m = pl.program_id(1)

    @pl.when(m == 0)
    def _():
        wb_ref[...] = w_ref[0].astype(BF16)

    k_scale = jnp.where(n == 1, RET_DK ** -0.5, 1.0).astype(F32)
    half = RET_DK // 2

    def project(rope):
        for rows in _row_chunks(h_ref.shape[0]):
            y = jnp.dot(h_ref[rows, :], wb_ref[...], preferred_element_type=F32) * k_scale
            if rope:
                c = cos_ref[rows, :]
                s = sin_ref[rows, :]
                parts = []
                for hh in range(RET_TN // RET_DK):
                    x1 = y[:, hh * RET_DK:hh * RET_DK + half]
                    x2 = y[:, hh * RET_DK + half:(hh + 1) * RET_DK]
                    parts.append(x1 * c - x2 * s)
                    parts.append(x1 * s + x2 * c)
                y = jnp.concatenate(parts, axis=1)
            o_ref[rows, :] = y.astype(BF16)

    do_rope = jnp.logical_and(m >= first_sample_tile, n < 2)
    pl.when(do_rope)(lambda: project(True))
    pl.when(jnp.logical_not(do_rope))(lambda: project(False))


def _ret_proj(h, w_ret_in, layer, cos, sin):
    tm = TM
    ncols = 2 * RET_HK + 3 * RET_HV
    tmap = _table_map(tm)
    kern = functools.partial(_ret_proj_kernel, first_sample_tile=N_PROMPT // tm)
    return pl.pallas_call(
        kern,
        grid=(ncols // RET_TN, N_TOK // tm),
        in_specs=[pl.BlockSpec((tm, D_MODEL), lambda n, m: (m, 0)),
                  pl.BlockSpec((1, D_MODEL, RET_TN), lambda n, m: (layer, 0, n)),
                  pl.BlockSpec((tm, LANES), lambda n, m: tmap(m)),
                  pl.BlockSpec((tm, LANES), lambda n, m: tmap(m))],
        out_specs=pl.BlockSpec((tm, RET_TN), lambda n, m: (m, n)),
        out_shape=jax.ShapeDtypeStruct((N_TOK, ncols), BF16),
        scratch_shapes=[pltpu.VMEM((D_MODEL, RET_TN), BF16)],
        compiler_params=_cparams(("arbitrary", "arbitrary")),
        name="ret_proj",
    )(h, w_ret_in, cos, sin)


def _swiglu_kernel(h_ref, wg_ref, wu_ref, o_ref, wgb_ref, wub_ref):
    m = pl.program_id(1)

    @pl.when(m == 0)
    def _():
        wgb_ref[...] = wg_ref[0].astype(BF16)
        wub_ref[...] = wu_ref[0].astype(BF16)

    for rows in _row_chunks(h_ref.shape[0]):
        h = h_ref[rows, :]
        g = jnp.dot(h, wgb_ref[...], preferred_element_type=F32)
        u = jnp.dot(h, wub_ref[...], preferred_element_type=F32)
        o_ref[rows, :] = (g * jax.nn.sigmoid(g) * u).astype(BF16)


def _swiglu_up(h, w_gu, layer):
    tm, tf = TM, 256
    nf = D_FF // tf
    return pl.pallas_call(
        _swiglu_kernel,
        grid=(nf, N_TOK // tm),
        in_specs=[pl.BlockSpec((tm, D_MODEL), lambda f, m: (m, 0)),
                  pl.BlockSpec((1, D_MODEL, tf), lambda f, m: (layer, 0, f)),
                  pl.BlockSpec((1, D_MODEL, tf), lambda f, m: (layer, 0, nf + f))],
        out_specs=pl.BlockSpec((tm, tf), lambda f, m: (m, f)),
        out_shape=jax.ShapeDtypeStruct((N_TOK, D_FF), BF16),
        scratch_shapes=[pltpu.VMEM((D_MODEL, tf), BF16), pltpu.VMEM((D_MODEL, tf), BF16)],
        compiler_params=_cparams(("arbitrary", "arbitrary")),
        name="swiglu_up",
    )(h, w_gu, w_gu)


def _route(logits):
    lane = lax.broadcasted_iota(I32, logits.shape, 1).astype(F32)
    lg = jnp.where(lane < N_EXPERTS, logits, -jnp.inf)
    m1 = jnp.max(lg, axis=-1, keepdims=True)
    i1 = jnp.min(jnp.where(lg == m1, lane, float(LANES)), axis=-1, keepdims=True)
    lg2 = jnp.where(lane == i1, -jnp.inf, lg)
    m2 = jnp.max(lg2, axis=-1, keepdims=True)
    i2 = jnp.min(jnp.where(lg2 == m2, lane, float(LANES)), axis=-1, keepdims=True)
    e2 = jnp.exp(m2 - m1)
    p1 = 1.0 / (1.0 + e2)
    p2 = e2 / (1.0 + e2)
    out = jnp.where(lane == 0, i1, 0.0)
    out = jnp.where(lane == 1, i2, out)
    out = jnp.where(lane == 2, p1, out)
    out = jnp.where(lane == 3, p2, out)
    return out


def _epilogue(x_new, pk_next, rows, fin_ref, final):
    if final:
        ms = jnp.mean(x_new * x_new, axis=-1, keepdims=True)
        return x_new * lax.rsqrt(ms + EPS) * fin_ref[...]
    g_row, sc_row, sh_row = rows
    return _normmod(x_new, pk_next[g_row:g_row + 1], pk_next[sc_row:sc_row + 1], pk_next[sh_row:sh_row + 1])


def _down_kernel(*refs, n_a, n_x, first_sample_tile, gate_row, next_rows, router):
    a_refs = refs[:n_a]
    w_ref = refs[n_a]
    x_refs = refs[n_a + 1:n_a + 1 + n_x]
    pk_ref, pkn_ref = refs[n_a + 1 + n_x:n_a + 3 + n_x]
    pos = n_a + 3 + n_x
    if router:
        wrh_ref, wrl_ref = refs[pos:pos + 2]
        pos += 2
        xo_ref, h_ref, hf_ref, rt_ref, wb_ref = refs[pos:pos + 5]
    else:
        xo_ref, h_ref, wb_ref = refs[pos:pos + 3]
    m = pl.program_id(0)

    @pl.when(m == 0)
    def _():
        wb_ref[...] = w_ref[0].astype(BF16)

    def finish(a_ref, x_ref):
        pk = pk_ref[0, 0]
        pkn = pkn_ref[0, 0]
        for c in range(a_ref.shape[0] // DOWN_CHUNK):
            rows = pl.ds(c * DOWN_CHUNK, DOWN_CHUNK)
            y = jnp.dot(a_ref[rows, :], wb_ref[...], preferred_element_type=F32)
            x_new = x_ref[rows, :] + pk[gate_row:gate_row + 1] * y
            xo_ref[rows, :] = x_new
            hn = _epilogue(x_new, pkn, next_rows, None, False)
            hi = hn.astype(BF16)
            h_ref[rows, :] = hi
            if router:
                hf_ref[rows, :] = hn
                lo = (hn - hi.astype(F32)).astype(BF16)
                logits = (jnp.dot(hi, wrh_ref[0], preferred_element_type=F32)
                          + jnp.dot(lo, wrh_ref[0], preferred_element_type=F32)
                          + jnp.dot(hi, wrl_ref[0], preferred_element_type=F32))
                rt_ref[rows, :] = _route(logits)

    if n_a == 1 and n_x == 1:
        finish(a_refs[0], x_refs[0])
    else:
        pl.when(m < first_sample_tile)(lambda: finish(a_refs[0], x_refs[0]))
        pl.when(m >= first_sample_tile)(lambda: finish(a_refs[-1], x_refs[-1]))


def _down(a_list, w, layer, x, pack, pack_layer, gate_row, next_layer, next_rows, w_router=None, router_layer=0):
    tm = TM_DOWN
    kd = w.shape[1]
    n_a = len(a_list)
    fst = N_PROMPT // tm
    router = w_router is not None
    x_list = list(x) if isinstance(x, (list, tuple)) else [x]
    n_x = len(x_list)
    a_specs = [pl.BlockSpec((tm, kd), lambda m: (m, 0))] if n_a == 1 else _split_specs(tm, kd)
    x_specs = [pl.BlockSpec((tm, D_MODEL), lambda m: (m, 0))] if n_x == 1 else _split_specs(tm, D_MODEL)
    in_specs = a_specs + [pl.BlockSpec((1, kd, D_MODEL), lambda m: (layer, 0, 0))] + x_specs + [
        pl.BlockSpec((1, 1, 8, D_MODEL), lambda m: (pack_layer, (m * tm) // GROUP_TOKENS, 0, 0)),
        pl.BlockSpec((1, 1, 8, D_MODEL), lambda m: (next_layer, (m * tm) // GROUP_TOKENS, 0, 0)),
    ]
    args = list(a_list) + [w] + list(x_list) + [pack, pack]
    out_specs = [pl.BlockSpec((tm, D_MODEL), lambda m: (m, 0)),
                 pl.BlockSpec((tm, D_MODEL), lambda m: (m, 0))]
    out_shape = [jax.ShapeDtypeStruct((N_TOK, D_MODEL), F32),
                 jax.ShapeDtypeStruct((N_TOK, D_MODEL), BF16)]
    if router:
        in_specs += [pl.BlockSpec((1, D_MODEL, LANES), lambda m: (router_layer, 0, 0)),
                     pl.BlockSpec((1, D_MODEL, LANES), lambda m: (router_layer, 0, 0))]
        args += list(w_router)
        out_specs += [pl.BlockSpec((tm, D_MODEL), lambda m: (m, 0)),
                      pl.BlockSpec((tm, LANES), lambda m: (m, 0))]
        out_shape += [jax.ShapeDtypeStruct((N_TOK, D_MODEL), F32),
                      jax.ShapeDtypeStruct((N_TOK, LANES), F32)]
    kern = functools.partial(_down_kernel, n_a=n_a, n_x=n_x, first_sample_tile=fst, gate_row=gate_row,
                             next_rows=next_rows, router=router)
    return pl.pallas_call(
        kern,
        grid=(N_TOK // tm,),
        in_specs=in_specs,
        out_specs=out_specs,
        out_shape=out_shape,
        scratch_shapes=[pltpu.VMEM((kd, D_MODEL), BF16)],
        compiler_params=_cparams(("arbitrary",)),
        name="down_proj",
    )(*args)


def _sink_row(sink_ref, layer, kvh, width):
    return jnp.concatenate(
        [jnp.full((1, width), sink_ref[layer * N_HEADS + kvh * GROUP + g] * LOG2E, F32) for g in range(GROUP)],
        axis=1)


def _group_queries(q_ref, kvh):
    return jnp.concatenate(
        [q_ref[:, (kvh * GROUP + g) * HEAD_DIM:(kvh * GROUP + g + 1) * HEAD_DIM] for g in range(GROUP)], axis=0)


_NT = (((1,), (1,)), ((), ()))


def _ctx_attn_kernel(sink_ref, q_ref, kv_ref, o_ref, *, layer):
    kvw = N_KV_HEADS * HEAD_DIM
    k_all = kv_ref[:, :kvw].astype(BF16)
    v_t = kv_ref[:, kvw:].T.astype(BF16)
    outs = []
    for kvh in range(N_KV_HEADS):
        hs = slice(kvh * HEAD_DIM, (kvh + 1) * HEAD_DIM)
        s_t = lax.dot_general(k_all[:, hs], _group_queries(q_ref, kvh), _NT, preferred_element_type=F32)
        sink = _sink_row(sink_ref, layer, kvh, SEQ)
        m = jnp.maximum(jnp.max(s_t, axis=0, keepdims=True), sink)
        p = jnp.exp2(s_t - m)
        denom = jnp.sum(p, axis=0, keepdims=True) + jnp.exp2(sink - m)
        o_t = jnp.dot(v_t[hs], p.astype(BF16), preferred_element_type=F32) / denom
        outs += [o_t[:, g * SEQ:(g + 1) * SEQ] for g in range(GROUP)]
    o_ref[...] = jnp.concatenate(outs, axis=0).T.astype(BF16)


def _ctx_attention(q, kv, sink_flat, layer):
    kern = functools.partial(_ctx_attn_kernel, layer=layer)
    return pl.pallas_call(
        kern,
        grid_spec=pltpu.PrefetchScalarGridSpec(
            num_scalar_prefetch=1,
            grid=(BATCH,),
            in_specs=[pl.BlockSpec((SEQ, D_MODEL), lambda b, s: (b, 0)),
                      pl.BlockSpec((SEQ, 2 * N_KV_HEADS * HEAD_DIM), lambda b, s: (b, 0))],
            out_specs=pl.BlockSpec((SEQ, D_MODEL), lambda b, s: (b, 0))),
        out_shape=jax.ShapeDtypeStruct((N_PROMPT, D_MODEL), BF16),
        compiler_params=_cparams(("arbitrary",)),
        name="ctx_attention",
    )(sink_flat, q, kv)


def _lat_attn_kernel(sink_ref, q_ref, kvp_ref, kvc_ref, kvn_ref, ck_ref, cv_ref, o_ref, kc_ref, vct_ref, *, layer):
    blk = pl.program_id(1)
    kvw = N_KV_HEADS * HEAD_DIM
    cols = GROUP * BLOCK
    n_lat = 3 * BLOCK

    @pl.when(blk == 0)
    def _():
        kc_ref[...] = ck_ref[0, 0].astype(BF16)
        vct_ref[...] = cv_ref[0, 0].T.astype(BF16)

    j = lax.broadcasted_iota(I32, (n_lat, cols), 0)
    r = lax.broadcasted_iota(I32, (n_lat, cols), 1) % BLOCK
    lo = jnp.maximum(r, BLOCK - BLOCK * blk)
    hi = jnp.minimum(r + 2 * WINDOW, DEC_SEQ + BLOCK - 1 - BLOCK * blk)
    lat_mask = jnp.logical_and(j >= lo, j <= hi)
    k_lat = jnp.concatenate([kvp_ref[:, :kvw], kvc_ref[:, :kvw], kvn_ref[:, :kvw]], axis=0).astype(BF16)
    v_lat_t = jnp.concatenate([kvp_ref[:, kvw:].T, kvc_ref[:, kvw:].T, kvn_ref[:, kvw:].T], axis=1).astype(BF16)
    outs = []
    for kvh in range(N_KV_HEADS):
        hs = slice(kvh * HEAD_DIM, (kvh + 1) * HEAD_DIM)
        q4 = _group_queries(q_ref, kvh)
        s_ctx = lax.dot_general(kc_ref[:, hs], q4, _NT, preferred_element_type=F32)
        s_lat = lax.dot_general(k_lat[:, hs], q4, _NT, preferred_element_type=F32)
        s_lat = jnp.where(lat_mask, s_lat, NEG)
        sink = _sink_row(sink_ref, layer, kvh, BLOCK)
        m = jnp.maximum(jnp.maximum(jnp.max(s_ctx, axis=0, keepdims=True),
                                    jnp.max(s_lat, axis=0, keepdims=True)), sink)
        p_ctx = jnp.exp2(s_ctx - m)
        p_lat = jnp.exp2(s_lat - m)
        denom = (jnp.sum(p_ctx, axis=0, keepdims=True) + jnp.sum(p_lat, axis=0, keepdims=True)
                 + jnp.exp2(sink - m))
        o_t = (jnp.dot(vct_ref[hs, :], p_ctx.astype(BF16), preferred_element_type=F32)
               + jnp.dot(v_lat_t[hs], p_lat.astype(BF16), preferred_element_type=F32)) / denom
        outs += [o_t[:, g * BLOCK:(g + 1) * BLOCK] for g in range(GROUP)]
    o_ref[...] = jnp.concatenate(outs, axis=0).T.astype(BF16)


def _lat_attention(q, kv, cache_k4, cache_v4, sink_flat, layer):
    nb = DEC_SEQ // BLOCK
    base = N_PROMPT // BLOCK
    kvc = 2 * N_KV_HEADS * HEAD_DIM

    def row(b, i):
        return base + b * nb + i

    kern = functools.partial(_lat_attn_kernel, layer=layer)
    return pl.pallas_call(
        kern,
        grid_spec=pltpu.PrefetchScalarGridSpec(
            num_scalar_prefetch=1,
            grid=(DEC_BATCH, nb),
            in_specs=[pl.BlockSpec((BLOCK, D_MODEL), lambda b, i, s: (row(b, i), 0)),
                      pl.BlockSpec((BLOCK, kvc), lambda b, i, s: (row(b, jnp.maximum(i - 1, 0)), 0)),
                      pl.BlockSpec((BLOCK, kvc), lambda b, i, s: (row(b, i), 0)),
                      pl.BlockSpec((BLOCK, kvc), lambda b, i, s: (row(b, jnp.minimum(i + 1, nb - 1)), 0)),
                      pl.BlockSpec((1, 1, PAST_LEN, N_KV_HEADS * HEAD_DIM), lambda b, i, s: (b, layer, 0, 0)),
                      pl.BlockSpec((1, 1, PAST_LEN, N_KV_HEADS * HEAD_DIM), lambda b, i, s: (b, layer, 0, 0))],
            out_specs=pl.BlockSpec((BLOCK, D_MODEL), lambda b, i, s: (b * nb + i, 0)),
            scratch_shapes=[pltpu.VMEM((PAST_LEN, N_KV_HEADS * HEAD_DIM), BF16),
                            pltpu.VMEM((N_KV_HEADS * HEAD_DIM, PAST_LEN), BF16)]),
        out_shape=jax.ShapeDtypeStruct((N_SAMPLE, D_MODEL), BF16),
        compiler_params=_cparams(("arbitrary", "arbitrary")),
        name="lat_attention",
    )(sink_flat, q, kv, kv, kv, cache_k4, cache_v4)


RET_SUB = 2
RET_ROWS = RET_SUB * BLOCK
RET_STEPS = N_TOK // RET_ROWS
PROMPT_STEPS = N_PROMPT // RET_ROWS
STEPS_PER_SAMPLE = DEC_SEQ // RET_ROWS
assert SEQ == RET_ROWS


def _ret_kernel(lg_ref, q_ref, k_ref, v_ref, gate_ref, s0_ref, *rest, backward, layer):
    part_ref = rest[0] if backward else None
    o_ref, sout_ref, state_ref = rest[-3:]
    step = pl.program_id(0)
    blk = (RET_STEPS - 1 - step) if backward else step
    in_prompt = blk < PROMPT_STEPS
    first_of_sample = (blk - PROMPT_STEPS) % STEPS_PER_SAMPLE == (STEPS_PER_SAMPLE - 1 if backward else 0)

    @pl.when(in_prompt)
    def _():
        state_ref[...] = jnp.zeros_like(state_ref)

    @pl.when(jnp.logical_and(jnp.logical_not(in_prompt), first_of_sample))
    def _():
        state_ref[...] = s0_ref[0, 0]

    ii = lax.broadcasted_iota(I32, (BLOCK, BLOCK), 0).astype(F32)
    jj = lax.broadcasted_iota(I32, (BLOCK, BLOCK), 1).astype(F32)
    dist = (jj - ii) if backward else (ii - jj)
    ti = lax.broadcasted_iota(I32, (BLOCK, 1), 0).astype(F32)
    q_pow = (BLOCK - ti) if backward else (ti + 1.0)
    k_pow = ti if backward else (BLOCK - 1.0 - ti)

    for h in range(RET_HEADS):
        lg = lg_ref[(layer * 2 + (1 if backward else 0)) * RET_HEADS + h]
        intra = jnp.where(dist >= 0, jnp.exp(lg * jnp.maximum(dist, 0.0)), 0.0)
        q_dec = jnp.exp(lg * q_pow)
        k_dec = jnp.exp(lg * k_pow)
        c_dec = jnp.exp(lg * BLOCK)
        for sub in (reversed(range(RET_SUB)) if backward else range(RET_SUB)):
            rows = pl.ds(sub * BLOCK, BLOCK)
            q = q_ref[rows, h * RET_DK:(h + 1) * RET_DK]
            k = k_ref[rows, h * RET_DK:(h + 1) * RET_DK]
            v = v_ref[rows, h * RET_DV:(h + 1) * RET_DV]
            s = state_ref[h]
            a = lax.dot_general(q, k, (((1,), (1,)), ((), ())), preferred_element_type=F32) * intra
            o = (jnp.dot(a.astype(BF16), v, preferred_element_type=F32)
                 + q_dec * jnp.dot(q, s.astype(BF16), preferred_element_type=F32))
            kd = (k.astype(F32) * k_dec).astype(BF16)
            state_ref[h] = c_dec * s + lax.dot_general(kd, v, (((0,), (0,)), ((), ())), preferred_element_type=F32)
            gate = gate_ref[rows, h * RET_DV:(h + 1) * RET_DV].astype(F32)
            on = o * lax.rsqrt(jnp.mean(o * o, axis=-1, keepdims=True) + EPS)
            res = on * (gate * jax.nn.sigmoid(gate))
            if backward:
                res = res + part_ref[rows, h * RET_DV:(h + 1) * RET_DV].astype(F32)
            o_ref[rows, h * RET_DV:(h + 1) * RET_DV] = res.astype(BF16)

    @pl.when(in_prompt)
    def _():
        if sout_ref.shape[1] == 1:
            sout_ref[0, 0] = state_ref[...]
        else:
            for l in range(sout_ref.shape[1]):
                sout_ref[0, l] = state_ref[...] if l == layer else jnp.zeros_like(state_ref)


def _retention_pass(proj, s0, log_gamma_flat, layer, backward, partial=None, states=None):
    def blk(i):
        return (RET_STEPS - 1 - i) if backward else i

    def s0_map(i, lg):
        return (jnp.clip((blk(i) - PROMPT_STEPS) // STEPS_PER_SAMPLE, 0, DEC_BATCH - 1), layer, 0, 0, 0)

    sout_layers = N_ODD if states is None else 1

    def sout_map(i, lg):
        return (jnp.minimum(blk(i), BATCH - 1), 0 if states is None else layer, 0, 0, 0)

    gate_block = 3 if backward else 2
    in_specs = [pl.BlockSpec((RET_ROWS, RET_HK), lambda i, lg: (blk(i), 0)),
                pl.BlockSpec((RET_ROWS, RET_HK), lambda i, lg: (blk(i), 1)),
                pl.BlockSpec((RET_ROWS, RET_HV), lambda i, lg: (blk(i), 1)),
                pl.BlockSpec((RET_ROWS, RET_HV), lambda i, lg: (blk(i), gate_block)),
                pl.BlockSpec((1, 1, RET_HEADS, RET_DK, RET_DV), s0_map)]
    args = [proj, proj, proj, proj, s0]
    if backward:
        in_specs.append(pl.BlockSpec((RET_ROWS, RET_HV), lambda i, lg: (blk(i), 0)))
        args.append(partial)
    aliases = {}
    if states is not None:
        in_specs.append(pl.BlockSpec(memory_space=pl.ANY))
        args.append(states)
        aliases = {len(args): 1}
    kern = functools.partial(_ret_kernel, backward=backward, layer=layer)
    return pl.pallas_call(
        kern,
        grid_spec=pltpu.PrefetchScalarGridSpec(
            num_scalar_prefetch=1,
            grid=(RET_STEPS,),
            in_specs=in_specs,
            out_specs=[pl.BlockSpec((RET_ROWS, RET_HV), lambda i, lg: (blk(i), 0)),
                       pl.BlockSpec((1, sout_layers, RET_HEADS, RET_DK, RET_DV), sout_map)],
            scratch_shapes=[pltpu.VMEM((RET_HEADS, RET_DK, RET_DV), F32)]),
        out_shape=[jax.ShapeDtypeStruct((N_TOK, RET_HV), BF16),
                   jax.ShapeDtypeStruct((BATCH, N_ODD, RET_HEADS, RET_DK, RET_DV), F32)],
        input_output_aliases=aliases,
        compiler_params=_cparams(("arbitrary",)),
        name="retention_bwd" if backward else "retention_fwd",
    )(log_gamma_flat, *args)


def _dispatch_kernel(slot_ref, pend_ref, h_ref, o_hbm, zbuf, sem):
    i = pl.program_id(0)

    @pl.when(i == 0)
    def _():
        zbuf[...] = jnp.zeros_like(zbuf)

        def tile_fill(start):
            return pltpu.make_async_copy(zbuf, o_hbm.at[pl.ds(pl.multiple_of(start, MOE_TILE), MOE_TILE)], sem)

        def nonempty(e):
            return pend_ref[e] > (pend_ref[e - 1] if e else 0)

        def start_unused(t, carry):
            tile_fill(t * MOE_TILE).start()
            return carry

        def wait_unused(t, carry):
            tile_fill(t * MOE_TILE).wait()
            return carry

        first_unused = pend_ref[N_EXPERTS - 1] // MOE_TILE
        for e in range(N_EXPERTS):
            pl.when(nonempty(e))(lambda e=e: tile_fill(pend_ref[e] - MOE_TILE).start())
        lax.fori_loop(first_unused, MOE_NT, start_unused, 0)
        for e in range(N_EXPERTS):
            pl.when(nonempty(e))(lambda e=e: tile_fill(pend_ref[e] - MOE_TILE).wait())
        lax.fori_loop(first_unused, MOE_NT, wait_unused, 0)

    base = i * DISPATCH_ROWS * TOP_K

    def start(r, carry):
        for k in range(TOP_K):
            pltpu.make_async_copy(h_ref.at[pl.ds(r, 1)], o_hbm.at[pl.ds(slot_ref[base + r * TOP_K + k], 1)],
                                  sem).start()
        return carry

    lax.fori_loop(0, DISPATCH_ROWS, start, 0, unroll=8)
    for k in range(TOP_K):
        pltpu.make_async_copy(h_ref, o_hbm.at[pl.ds(0, DISPATCH_ROWS)], sem).wait()


def _dispatch_rows(slot, pend, h):
    return pl.pallas_call(
        _dispatch_kernel,
        grid_spec=pltpu.PrefetchScalarGridSpec(
            num_scalar_prefetch=2,
            grid=(N_TOK // DISPATCH_ROWS,),
            in_specs=[pl.BlockSpec((DISPATCH_ROWS, D_MODEL), lambda i, s, p: (i, 0))],
            out_specs=pl.BlockSpec(memory_space=pl.ANY),
            scratch_shapes=[pltpu.VMEM((MOE_TILE, D_MODEL), F32), pltpu.SemaphoreType.DMA(())]),
        out_shape=jax.ShapeDtypeStruct((MOE_SLOTS, D_MODEL), F32),
        compiler_params=_cparams(("arbitrary",)),
        name="moe_dispatch",
    )(slot, pend, h)


def _moe_up_kernel(te_ref, nv_ref, tr_ref, h_ref, wg_ref, wu_ref, o_ref, wgb_ref, wub_ref):
    m = pl.program_id(1)
    new_w = jnp.logical_or(m == 0, te_ref[m] != te_ref[jnp.maximum(m - 1, 0)])

    @pl.when(new_w)
    def _():
        wgb_ref[...] = wg_ref[0, 0].astype(BF16)
        wub_ref[...] = wu_ref[0, 0].astype(BF16)

    for c in range(MOE_TILE // MOE_CHUNK):
        rows = pl.ds(c * MOE_CHUNK, MOE_CHUNK)
        has_rows = tr_ref[m] > c * MOE_CHUNK

        @pl.when(has_rows)
        def _():
            h = h_ref[rows, :].astype(BF16)
            g = jnp.dot(h, wgb_ref[...], preferred_element_type=F32)
            u = jnp.dot(h, wub_ref[...], preferred_element_type=F32)
            o_ref[rows, :] = (g * jax.nn.sigmoid(g) * u).astype(BF16)

        @pl.when(jnp.logical_not(has_rows))
        def _():
            o_ref[rows, :] = jnp.zeros((MOE_CHUNK, o_ref.shape[1]), o_ref.dtype)


def _moe_up(tile_expert, n_valid, tile_rows, hs, w_gu, layer):
    tf = 1792
    nf = D_FF_EXPERT // tf
    return pl.pallas_call(
        _moe_up_kernel,
        grid_spec=pltpu.PrefetchScalarGridSpec(
            num_scalar_prefetch=3,
            grid=(nf, MOE_NT),
            in_specs=[pl.BlockSpec((MOE_TILE, D_MODEL), lambda f, m, te, nv, tr: (jnp.minimum(m, nv[0] - 1), 0)),
                      pl.BlockSpec((1, 1, D_MODEL, tf), lambda f, m, te, nv, tr: (layer, te[m], 0, f)),
                      pl.BlockSpec((1, 1, D_MODEL, tf), lambda f, m, te, nv, tr: (layer, te[m], 0, nf + f))],
            out_specs=pl.BlockSpec((MOE_TILE, tf), lambda f, m, te, nv, tr: (m, f)),
            scratch_shapes=[pltpu.VMEM((D_MODEL, tf), BF16), pltpu.VMEM((D_MODEL, tf), BF16)]),
        out_shape=jax.ShapeDtypeStruct((MOE_SLOTS, D_FF_EXPERT), BF16),
        compiler_params=_cparams(("arbitrary", "arbitrary")),
        name="moe_up",
    )(tile_expert, n_valid, tile_rows, hs, w_gu, w_gu)


def _moe_down_kernel(te_ref, nv_ref, tr_ref, a_ref, w_ref, o_ref, wb_ref):
    m = pl.program_id(1)
    new_w = jnp.logical_or(m == 0, te_ref[m] != te_ref[jnp.maximum(m - 1, 0)])

    @pl.when(new_w)
    def _():
        wb_ref[...] = w_ref[0, 0].astype(BF16)

    for c in range(MOE_TILE // MOE_CHUNK):
        rows = pl.ds(c * MOE_CHUNK, MOE_CHUNK)
        has_rows = tr_ref[m] > c * MOE_CHUNK

        @pl.when(has_rows)
        def _():
            o_ref[rows, :] = jnp.dot(a_ref[rows, :], wb_ref[...], preferred_element_type=F32)

        @pl.when(jnp.logical_not(has_rows))
        def _():
            o_ref[rows, :] = jnp.zeros((MOE_CHUNK, o_ref.shape[1]), o_ref.dtype)


def _moe_down(tile_expert, n_valid, tile_rows, act, w_down, layer):
    tn = D_MODEL
    return pl.pallas_call(
        _moe_down_kernel,
        grid_spec=pltpu.PrefetchScalarGridSpec(
            num_scalar_prefetch=3,
            grid=(D_MODEL // tn, MOE_NT),
            in_specs=[pl.BlockSpec((MOE_TILE, D_FF_EXPERT), lambda n, m, te, nv, tr: (jnp.minimum(m, nv[0] - 1), 0)),
                      pl.BlockSpec((1, 1, D_FF_EXPERT, tn), lambda n, m, te, nv, tr: (layer, te[m], 0, n))],
            out_specs=pl.BlockSpec((MOE_TILE, tn), lambda n, m, te, nv, tr: (m, n)),
            scratch_shapes=[pltpu.VMEM((D_FF_EXPERT, tn), BF16)]),
        out_shape=jax.ShapeDtypeStruct((MOE_SLOTS, D_MODEL), F32),
        compiler_params=_cparams(("arbitrary", "arbitrary")),
        name="moe_down",
    )(tile_expert, n_valid, tile_rows, act, w_down)


def _combine_kernel(slot_ref, y_hbm, x_ref, rt_ref, pk_ref, pkn_ref, fin_ref, *rest, tok0, final):
    if final:
        out_ref, ybuf, sem = rest
    else:
        xo_ref, h_ref, ybuf, sem = rest
    tc = TC_COMBINE
    base = (tok0 + pl.program_id(0) * tc) * TOP_K

    def start(r, carry):
        for k in range(TOP_K):
            pltpu.make_async_copy(y_hbm.at[pl.ds(slot_ref[base + r * TOP_K + k], 1)],
                                  ybuf.at[k, pl.ds(r, 1)], sem).start()
        return carry

    lax.fori_loop(0, tc, start, 0, unroll=8)
    for k in range(TOP_K):
        pltpu.make_async_copy(y_hbm.at[pl.ds(0, tc)], ybuf.at[k], sem).wait()

    rt = rt_ref[...]
    moe = rt[:, 2:3] * ybuf[0] + rt[:, 3:4] * ybuf[1]
    pk = pk_ref[0, 0]
    x_new = x_ref[...] + pk[R_GATE_FFN:R_GATE_FFN + 1] * moe
    if final:
        out_ref[...] = _epilogue(x_new, None, None, fin_ref, True)
    else:
        xo_ref[...] = x_new
        h_ref[...] = _epilogue(x_new, pkn_ref[0, 0], (R_G_MIX, R_SCALE_MIX, R_SHIFT_MIX), None, False).astype(BF16)


def _combine(slot, y, x, route, pack, layer, next_layer, final_norm, tok0, n_rows, final):
    tc = TC_COMBINE
    b0 = tok0 // tc

    def tmap(m, s):
        return (b0 + m, 0)

    def pmap(l):
        return lambda m, s: (l, (tok0 + m * tc) // GROUP_TOKENS, 0, 0)

    if final:
        out_specs = pl.BlockSpec((tc, D_MODEL), lambda m, s: (m, 0))
        out_shape = jax.ShapeDtypeStruct((n_rows, D_MODEL), F32)
    else:
        out_specs = [pl.BlockSpec((tc, D_MODEL), lambda m, s: (m, 0)),
                     pl.BlockSpec((tc, D_MODEL), lambda m, s: (m, 0))]
        out_shape = [jax.ShapeDtypeStruct((n_rows, D_MODEL), F32),
                     jax.ShapeDtypeStruct((n_rows, D_MODEL), BF16)]
    kern = functools.partial(_combine_kernel, tok0=tok0, final=final)
    return pl.pallas_call(
        kern,
        grid_spec=pltpu.PrefetchScalarGridSpec(
            num_scalar_prefetch=1,
            grid=(n_rows // tc,),
            in_specs=[pl.BlockSpec(memory_space=pl.ANY),
                      pl.BlockSpec((tc, D_MODEL), tmap),
                      pl.BlockSpec((tc, LANES), tmap),
                      pl.BlockSpec((1, 1, 8, D_MODEL), pmap(layer)),
                      pl.BlockSpec((1, 1, 8, D_MODEL), pmap(next_layer)),
                      pl.BlockSpec((1, D_MODEL), lambda m, s: (0, 0))],
            out_specs=out_specs,
            scratch_shapes=[pltpu.VMEM((TOP_K, tc, D_MODEL), F32), pltpu.SemaphoreType.DMA(())]),
        out_shape=out_shape,
        compiler_params=_cparams(("arbitrary",)),
        name="moe_combine",
    )(slot, y, x, route, pack, pack, final_norm.reshape(1, D_MODEL))


def _routing_tables(route):
    e_flat = route[:, :TOP_K].astype(I32).reshape(-1)
    onehot = (e_flat[:, None] == jnp.arange(N_EXPERTS, dtype=I32)[None, :]).astype(I32)
    csum = jnp.cumsum(onehot, axis=0)
    rank = jnp.sum(onehot * (csum - 1), axis=1)
    count = csum[-1]
    padded = ((count + MOE_TILE - 1) // MOE_TILE) * MOE_TILE
    pend = jnp.cumsum(padded)
    poff = pend - padded
    slot = jnp.sum(onehot * poff[None, :], axis=1) + rank
    n_valid = pend[-1] // MOE_TILE
    tile_start = jnp.arange(MOE_NT, dtype=I32) * MOE_TILE
    te_raw = jnp.minimum(jnp.sum((tile_start[:, None] >= pend[None, :]).astype(I32), axis=1), N_EXPERTS - 1)
    last_e = jnp.max(jnp.where(tile_start < pend[-1], te_raw, 0))
    tile_expert = jnp.minimum(te_raw, last_e)
    group_rows_end = jnp.sum((te_raw[:, None] == jnp.arange(N_EXPERTS, dtype=I32)[None, :]) * (poff + count)[None, :],
                             axis=1)
    tile_rows = jnp.where(tile_start < pend[-1], jnp.clip(group_rows_end - tile_start, 0, MOE_TILE), 0)
    return (slot.astype(I32), pend.astype(I32), tile_expert.astype(I32), n_valid.astype(I32).reshape(1),
            tile_rows.astype(I32))


def _rope_tables(dh):
    nf = dh // 4
    t = jnp.arange(DEC_SEQ)
    freqs = ROPE_BASE ** (-jnp.arange(nf, dtype=F32) / nf)
    row = (t // GRID_W).astype(F32)
    col = (t % GRID_W).astype(F32)
    ang = jnp.concatenate([row[:, None] * freqs, col[:, None] * freqs], axis=-1)
    return jnp.cos(ang), jnp.sin(ang)


def kernel(x_prompt, x_sample, cache_k, cache_v, state_fwd, state_bwd, c, c_ctx, norm_mix, norm_ffn, w_mod, b_mod,
           w_qkv, w_attn_o, attn_sink, w_ret_in, ret_decay, w_ret_out, w_ffn_gu, w_ffn_down, w_router, w_exp_gu,
           w_exp_down, final_norm):
    x = (x_prompt.reshape(N_PROMPT, D_MODEL), x_sample.reshape(N_SAMPLE, D_MODEL))

    cond8 = jnp.concatenate([c_ctx[None, :], c, jnp.zeros((8 - 1 - DEC_BATCH, D_MODEL), F32)], axis=0)
    mods = _modulations(cond8, w_mod, b_mod)
    m3 = mods[:, :N_GROUPS].reshape(DEPTH, N_GROUPS, 6, D_MODEL)
    pack = jnp.concatenate([
        m3,
        jnp.broadcast_to(norm_mix[:, None, None, :], (DEPTH, N_GROUPS, 1, D_MODEL)),
        jnp.broadcast_to(norm_ffn[:, None, None, :], (DEPTH, N_GROUPS, 1, D_MODEL))], axis=2)

    cos_a, sin_a = _rope_tables(HEAD_DIM)
    cos_attn = jnp.tile(cos_a, (1, LANES // (HEAD_DIM // 2)))
    sin_attn = jnp.tile(jnp.concatenate([-sin_a, sin_a], axis=1), (1, LANES // HEAD_DIM))
    cos_ret, sin_ret = _rope_tables(RET_DK)

    kvw = N_KV_HEADS * HEAD_DIM
    cache_k4 = cache_k.reshape(DEC_BATCH, N_EVEN, PAST_LEN, kvw)
    cache_v4 = cache_v.reshape(DEC_BATCH, N_EVEN, PAST_LEN, kvw)
    sink_flat = attn_sink.reshape(-1)
    log_gamma = jax.nn.log_sigmoid(ret_decay.astype(F32)).reshape(-1)
    w_router_pad = jnp.pad(w_router, ((0, 0), (0, 0), (0, LANES - N_EXPERTS)))
    w_router_hi = w_router_pad.astype(BF16)
    w_router_lo = (w_router_pad - w_router_hi.astype(F32)).astype(BF16)

    mix_rows = (R_G_MIX, R_SCALE_MIX, R_SHIFT_MIX)
    ffn_rows = (R_G_FFN, R_SCALE_FFN, R_SHIFT_FFN)

    new_k, new_v = [], []
    new_sf = new_sb = None
    h = _pre(x, pack)
    y_prompt = y_sample = None
    for i in range(DEPTH):
        j = i // 2
        if i % 2 == 0:
            q = _attn_proj(h, w_qkv, j, cos_attn, sin_attn, col0=0, ncols=N_HEADS * HEAD_DIM,
                           rope_cols=N_HEADS * HEAD_DIM, scale=HEAD_DIM ** -0.5 * LOG2E, out_dtype=BF16)
            kv = _attn_proj(h, w_qkv, j, cos_attn, sin_attn, col0=N_HEADS * HEAD_DIM, ncols=2 * kvw,
                            rope_cols=kvw, scale=1.0, out_dtype=F32)
            new_k.append(kv[:N_PROMPT, :kvw].reshape(BATCH, SEQ, N_KV_HEADS, HEAD_DIM))
            new_v.append(kv[:N_PROMPT, kvw:].reshape(BATCH, SEQ, N_KV_HEADS, HEAD_DIM))
            o_ctx = _ctx_attention(q, kv, sink_flat, j)
            o_lat = _lat_attention(q, kv, cache_k4, cache_v4, sink_flat, j)
            x, h = _down([o_ctx, o_lat], w_attn_o, j, x, pack, i, R_GATE_MIX, i, ffn_rows)
            act = _swiglu_up(h, w_ffn_gu, j)
            x, h = _down([act], w_ffn_down, j, x, pack, i, R_GATE_FFN, i + 1, mix_rows)
        else:
            proj = _ret_proj(h, w_ret_in, j, cos_ret, sin_ret)
            part, new_sf = _retention_pass(proj, state_fwd, log_gamma, j, False, states=new_sf)
            o_ret, new_sb = _retention_pass(proj, state_bwd, log_gamma, j, True, partial=part, states=new_sb)
            x, h, hf, route = _down([o_ret], w_ret_out, j, x, pack, i, R_GATE_MIX, i, ffn_rows,
                                    w_router=(w_router_hi, w_router_lo), router_layer=j)
            slot, pend, tile_expert, n_valid, tile_rows = _routing_tables(route)
            hs = _dispatch_rows(slot, pend, hf)
            act = _moe_up(tile_expert, n_valid, tile_rows, hs, w_exp_gu, j)
            ys = _moe_down(tile_expert, n_valid, tile_rows, act, w_exp_down, j)
            if i == DEPTH - 1:
                y_prompt = _combine(slot, ys, x, route, pack, i, i, final_norm, 0, N_PROMPT, True)
                y_sample = _combine(slot, ys, x, route, pack, i, i, final_norm, N_PROMPT, N_SAMPLE, True)
            else:
                x, h = _combine(slot, ys, x, route, pack, i, i + 1, final_norm, 0, N_TOK, False)

    return (y_prompt.reshape(BATCH, SEQ, D_MODEL), y_sample.reshape(DEC_BATCH, DEC_SEQ, D_MODEL),
            jnp.stack(new_k, axis=1), jnp.stack(new_v, axis=1), new_sf, new_sb)
```

```python
import functools

import jax
import jax.numpy as jnp
from jax import lax
from jax.experimental import pallas as pl
from jax.experimental.pallas import tpu as pltpu

F32 = jnp.float32
BF16 = jnp.bfloat16
I32 = jnp.int32

D_MODEL = 1024
BATCH = 16
SEQ = 256
DEPTH = 4
DEC_BATCH = 2
DEC_SEQ = 4096
PAST_LEN = 256
GRID_W = 64
BLOCK = 128
EPS = 1e-6
N_HEADS = 16
N_KV_HEADS = 4
HEAD_DIM = 64
GROUP = N_HEADS // N_KV_HEADS
WINDOW = 128
ROPE_BASE = 10000.0
RET_HEADS = 4
RET_DK = 256
RET_DV = 512
RET_HK = RET_HEADS * RET_DK
RET_HV = RET_HEADS * RET_DV
D_FF = 2816
N_EXPERTS = 8
TOP_K = 2
D_FF_EXPERT = 3584
N_EVEN = 2
N_ODD = 2
NEG = -1e30
LOG2E = 1.4426950408889634

GROUP_TOKENS = 4096
N_PROMPT = BATCH * SEQ
N_SAMPLE = DEC_BATCH * DEC_SEQ
N_TOK = N_PROMPT + N_SAMPLE
N_GROUPS = N_TOK // GROUP_TOKENS

R_SHIFT_MIX, R_SCALE_MIX, R_GATE_MIX, R_SHIFT_FFN, R_SCALE_FFN, R_GATE_FFN, R_G_MIX, R_G_FFN = range(8)

VMEM_LIMIT_BYTES = 56 * 1024 * 1024
LANES = 128

TM = 2048
UP_CHUNK = 512
TM_DOWN = 512
DOWN_CHUNK = 128
MOE_TILE = 512
MOE_CHUNK = 256
MOE_SLOTS = TOP_K * N_TOK + N_EXPERTS * MOE_TILE
MOE_NT = MOE_SLOTS // MOE_TILE
TC_COMBINE = 256
COMBINE_GROUP = 16
DISPATCH_ROWS = 256


def _cparams(sem):
    return pltpu.CompilerParams(dimension_semantics=sem, vmem_limit_bytes=VMEM_LIMIT_BYTES)


def _normmod(x, g, scale, shift):
    ms = jnp.mean(x * x, axis=-1, keepdims=True)
    return (x * lax.rsqrt(ms + EPS) * g) * (1.0 + scale) + shift


def _mod_kernel(cond_ref, w_ref, b_ref, o_ref):
    c = cond_ref[...]
    s = c * jax.nn.sigmoid(c)
    o_ref[0] = jnp.dot(s.astype(BF16), w_ref[0].astype(BF16), preferred_element_type=F32) + b_ref[0]


def _modulations(cond8, w_mod, b_mod):
    tn = 2048
    n6 = 6 * D_MODEL
    return pl.pallas_call(
        _mod_kernel,
        grid=(DEPTH, n6 // tn),
        in_specs=[pl.BlockSpec((8, D_MODEL), lambda l, n: (0, 0)),
                  pl.BlockSpec((1, D_MODEL, tn), lambda l, n: (l, 0, n)),
                  pl.BlockSpec((1, 1, tn), lambda l, n: (l, 0, n))],
        out_specs=pl.BlockSpec((1, 8, tn), lambda l, n: (l, 0, n)),
        out_shape=jax.ShapeDtypeStruct((DEPTH, 8, n6), F32),
        compiler_params=_cparams(("arbitrary", "arbitrary")),
        name="modulations",
    )(cond8, w_mod, b_mod.reshape(DEPTH, 1, n6))


def _split_specs(tm, width):
    fst = N_PROMPT // tm
    return [pl.BlockSpec((tm, width), lambda m: (jnp.minimum(m, fst - 1), 0)),
            pl.BlockSpec((tm, width), lambda m: (jnp.maximum(m - fst, 0), 0))]


def _pre_kernel(xp_ref, xs_ref, pk_ref, h_ref, *, first_sample_tile):
    def norm(x_ref):
        pk = pk_ref[0, 0]
        h = _normmod(x_ref[...], pk[R_G_MIX:R_G_MIX + 1], pk[R_SCALE_MIX:R_SCALE_MIX + 1],
                     pk[R_SHIFT_MIX:R_SHIFT_MIX + 1])
        h_ref[...] = h.astype(BF16)

    m = pl.program_id(0)
    pl.when(m < first_sample_tile)(lambda: norm(xp_ref))
    pl.when(m >= first_sample_tile)(lambda: norm(xs_ref))


def _pre(x_pair, pack):
    tm = TM
    return pl.pallas_call(
        functools.partial(_pre_kernel, first_sample_tile=N_PROMPT // tm),
        grid=(N_TOK // tm,),
        in_specs=_split_specs(tm, D_MODEL) + [
            pl.BlockSpec((1, 1, 8, D_MODEL), lambda m: (0, (m * tm) // GROUP_TOKENS, 0, 0))],
        out_specs=pl.BlockSpec((tm, D_MODEL), lambda m: (m, 0)),
        out_shape=jax.ShapeDtypeStruct((N_TOK, D_MODEL), BF16),
        compiler_params=_cparams(("arbitrary",)),
        name="pre_norm",
    )(*x_pair, pack)


def _row_chunks(n_rows):
    return [pl.ds(c * UP_CHUNK, UP_CHUNK) for c in range(n_rows // UP_CHUNK)]


def _table_map(tm):
    first_sample_tile = N_PROMPT // tm
    tiles_per_seq = DEC_SEQ // tm
    return lambda m: (jnp.maximum(m - first_sample_tile, 0) % tiles_per_seq, 0)


def _rope64(y, cos, sin_signed):
    width = y.shape[-1]
    lane = lax.broadcasted_iota(I32, y.shape, 1)
    first = (lane % HEAD_DIM) < (HEAD_DIM // 2)
    swapped = jnp.where(first, pltpu.roll(y, width - HEAD_DIM // 2, 1), pltpu.roll(y, HEAD_DIM // 2, 1))
    reps = width // LANES
    c = jnp.concatenate([cos] * reps, axis=1) if reps > 1 else cos
    s = jnp.concatenate([sin_signed] * reps, axis=1) if reps > 1 else sin_signed
    return y * c + swapped * s


def _attn_proj_kernel(h_ref, w_ref, cos_ref, sin_ref, o_ref, wb_ref, *, scale, rope_cols, first_sample_tile):
    m = pl.program_id(0)

    @pl.when(m == 0)
    def _():
        wb_ref[...] = w_ref[0].astype(BF16)

    def project(rope):
        for rows in _row_chunks(h_ref.shape[0]):
            y = jnp.dot(h_ref[rows, :], wb_ref[...], preferred_element_type=F32)
            if scale != 1.0:
                y = y * scale
            if rope:
                roped = _rope64(y[:, :rope_cols], cos_ref[rows, :], sin_ref[rows, :])
                y = roped if rope_cols == y.shape[1] else jnp.concatenate([roped, y[:, rope_cols:]], axis=1)
            o_ref[rows, :] = y.astype(o_ref.dtype)

    pl.when(m >= first_sample_tile)(lambda: project(True))
    pl.when(m < first_sample_tile)(lambda: project(False))


def _attn_proj(h, w_qkv, layer, cos, sin, *, col0, ncols, rope_cols, scale, out_dtype):
    tm = TM
    kern = functools.partial(_attn_proj_kernel, scale=scale, rope_cols=rope_cols, first_sample_tile=N_PROMPT // tm)
    return pl.pallas_call(
        kern,
        grid=(N_TOK // tm,),
        in_specs=[pl.BlockSpec((tm, D_MODEL), lambda m: (m, 0)),
                  pl.BlockSpec((1, D_MODEL, ncols), lambda m: (layer, 0, col0 // ncols)),
                  pl.BlockSpec((tm, LANES), _table_map(tm)),
                  pl.BlockSpec((tm, LANES), _table_map(tm))],
        out_specs=pl.BlockSpec((tm, ncols), lambda m: (m, 0)),
        out_shape=jax.ShapeDtypeStruct((N_TOK, ncols), out_dtype),
        scratch_shapes=[pltpu.VMEM((D_MODEL, ncols), BF16)],
        compiler_params=_cparams(("arbitrary",)),
        name="attn_proj",
    )(h, w_qkv, cos, sin)


RET_TN = RET_HK


def _ret_proj_kernel(h_ref, w_ref, cos_ref, sin_ref, o_ref, wb_ref, *, first_sample_tile):
    n = pl.program_id(0)
    m = pl.program_id(1)

    @pl.when(m == 0)
    def _():
        wb_ref[...] = w_ref[0].astype(BF16)

    k_scale = jnp.where(n == 1, RET_DK ** -0.5, 1.0).astype(F32)
    half = RET_DK // 2

    def project(rope, gate=False):
        for rows in _row_chunks(h_ref.shape[0]):
            y = jnp.dot(h_ref[rows, :], wb_ref[...], preferred_element_type=F32) * k_scale
            if gate:
                y = y * jax.nn.sigmoid(y)
            if rope:
                c = cos_ref[rows, :]
                s = sin_ref[rows, :]
                parts = []
                for hh in range(RET_TN // RET_DK):
                    x1 = y[:, hh * RET_DK:hh * RET_DK + half]
                    x2 = y[:, hh * RET_DK + half:(hh + 1) * RET_DK]
                    parts.append(x1 * c - x2 * s)
                    parts.append(x1 * s + x2 * c)
                y = jnp.concatenate(parts, axis=1)
            o_ref[rows, :] = y.astype(BF16)

    gate_tile0 = (2 * RET_HK + RET_HV) // RET_TN
    do_rope = jnp.logical_and(m >= first_sample_tile, n < 2)
    is_gate = n >= gate_tile0
    pl.when(do_rope)(lambda: project(True))
    pl.when(is_gate)(lambda: project(False, gate=True))
    pl.when(jnp.logical_not(jnp.logical_or(do_rope, is_gate)))(lambda: project(False))


def _ret_proj(h, w_ret_in, layer, cos, sin):
    tm = TM
    ncols = 2 * RET_HK + 3 * RET_HV
    tmap = _table_map(tm)
    kern = functools.partial(_ret_proj_kernel, first_sample_tile=N_PROMPT // tm)
    return pl.pallas_call(
        kern,
        grid=(ncols // RET_TN, N_TOK // tm),
        in_specs=[pl.BlockSpec((tm, D_MODEL), lambda n, m: (m, 0)),
                  pl.BlockSpec((1, D_MODEL, RET_TN), lambda n, m: (layer, 0, n)),
                  pl.BlockSpec((tm, LANES), lambda n, m: tmap(m)),
                  pl.BlockSpec((tm, LANES), lambda n, m: tmap(m))],
        out_specs=pl.BlockSpec((tm, RET_TN), lambda n, m: (m, n)),
        out_shape=jax.ShapeDtypeStruct((N_TOK, ncols), BF16),
        scratch_shapes=[pltpu.VMEM((D_MODEL, RET_TN), BF16)],
        compiler_params=_cparams(("arbitrary", "arbitrary")),
        name="ret_proj",
    )(h, w_ret_in, cos, sin)


def _swiglu_kernel(h_ref, wg_ref, wu_ref, o_ref, wgb_ref, wub_ref):
    m = pl.program_id(1)

    @pl.when(m == 0)
    def _():
        wgb_ref[...] = wg_ref[0].astype(BF16)
        wub_ref[...] = wu_ref[0].astype(BF16)

    for rows in _row_chunks(h_ref.shape[0]):
        h = h_ref[rows, :]
        g = jnp.dot(h, wgb_ref[...], preferred_element_type=F32)
        u = jnp.dot(h, wub_ref[...], preferred_element_type=F32)
        o_ref[rows, :] = (g * jax.nn.sigmoid(g) * u).astype(BF16)


def _swiglu_up(h, w_gu, layer):
    tm, tf = 1024, D_FF // 2
    nf = D_FF // tf
    return pl.pallas_call(
        _swiglu_kernel,
        grid=(nf, N_TOK // tm),
        in_specs=[pl.BlockSpec((tm, D_MODEL), lambda f, m: (m, 0)),
                  pl.BlockSpec((1, D_MODEL, tf), lambda f, m: (layer, 0, f)),
                  pl.BlockSpec((1, D_MODEL, tf), lambda f, m: (layer, 0, nf + f))],
        out_specs=pl.BlockSpec((tm, tf), lambda f, m: (m, f)),
        out_shape=jax.ShapeDtypeStruct((N_TOK, D_FF), BF16),
        scratch_shapes=[pltpu.VMEM((D_MODEL, tf), BF16), pltpu.VMEM((D_MODEL, tf), BF16)],
        compiler_params=_cparams(("arbitrary", "arbitrary")),
        name="swiglu_up",
    )(h, w_gu, w_gu)


def _route(logits):
    lane = lax.broadcasted_iota(I32, logits.shape, 1).astype(F32)
    lg = jnp.where(lane < N_EXPERTS, logits, -jnp.inf)
    m1 = jnp.max(lg, axis=-1, keepdims=True)
    i1 = jnp.min(jnp.where(lg == m1, lane, float(LANES)), axis=-1, keepdims=True)
    lg2 = jnp.where(lane == i1, -jnp.inf, lg)
    m2 = jnp.max(lg2, axis=-1, keepdims=True)
    i2 = jnp.min(jnp.where(lg2 == m2, lane, float(LANES)), axis=-1, keepdims=True)
    e2 = jnp.exp(m2 - m1)
    p1 = 1.0 / (1.0 + e2)
    p2 = e2 / (1.0 + e2)
    out = jnp.where(lane == 0, i1, 0.0)
    out = jnp.where(lane == 1, i2, out)
    out = jnp.where(lane == 2, p1, out)
    out = jnp.where(lane == 3, p2, out)
    return out


def _epilogue(x_new, pk_next, rows, fin_ref, final):
    if final:
        ms = jnp.mean(x_new * x_new, axis=-1, keepdims=True)
        return x_new * lax.rsqrt(ms + EPS) * fin_ref[...]
    g_row, sc_row, sh_row = rows
    return _normmod(x_new, pk_next[g_row:g_row + 1], pk_next[sc_row:sc_row + 1], pk_next[sh_row:sh_row + 1])


def _down_kernel(*refs, n_a, n_x, first_sample_tile, gate_row, next_rows, router):
    a_refs = refs[:n_a]
    w_ref = refs[n_a]
    x_refs = refs[n_a + 1:n_a + 1 + n_x]
    pk_ref, pkn_ref = refs[n_a + 1 + n_x:n_a + 3 + n_x]
    pos = n_a + 3 + n_x
    if router:
        wrh_ref, wrl_ref = refs[pos:pos + 2]
        pos += 2
        xo_ref, h_ref, hf_ref, rt_ref, wb_ref = refs[pos:pos + 5]
    else:
        xo_ref, h_ref, wb_ref = refs[pos:pos + 3]
    m = pl.program_id(0)

    @pl.when(m == 0)
    def _():
        wb_ref[...] = w_ref[0].astype(BF16)

    def finish(a_ref, x_ref):
        pk = pk_ref[0, 0]
        pkn = pkn_ref[0, 0]
        for c in range(a_ref.shape[0] // DOWN_CHUNK):
            rows = pl.ds(c * DOWN_CHUNK, DOWN_CHUNK)
            y = jnp.dot(a_ref[rows, :], wb_ref[...], preferred_element_type=F32)
            x_new = x_ref[rows, :] + pk[gate_row:gate_row + 1] * y
            xo_ref[rows, :] = x_new
            hn = _epilogue(x_new, pkn, next_rows, None, False)
            hi = hn.astype(BF16)
            h_ref[rows, :] = hi
            if router:
                hf_ref[rows, :] = hn
                lo = (hn - hi.astype(F32)).astype(BF16)
                logits = (jnp.dot(hi, wrh_ref[0], preferred_element_type=F32)
                          + jnp.dot(lo, wrh_ref[0], preferred_element_type=F32)
                          + jnp.dot(hi, wrl_ref[0], preferred_element_type=F32))
                rt_ref[rows, :] = _route(logits)

    if n_a == 1 and n_x == 1:
        finish(a_refs[0], x_refs[0])
    else:
        pl.when(m < first_sample_tile)(lambda: finish(a_refs[0], x_refs[0]))
        pl.when(m >= first_sample_tile)(lambda: finish(a_refs[-1], x_refs[-1]))


def _down(a_list, w, layer, x, pack, pack_layer, gate_row, next_layer, next_rows, w_router=None, router_layer=0):
    tm = TM_DOWN
    kd = w.shape[1]
    n_a = len(a_list)
    fst = N_PROMPT // tm
    router = w_router is not None
    x_list = list(x) if isinstance(x, (list, tuple)) else [x]
    n_x = len(x_list)
    a_specs = [pl.BlockSpec((tm, kd), lambda m: (m, 0))] if n_a == 1 else _split_specs(tm, kd)
    x_specs = [pl.BlockSpec((tm, D_MODEL), lambda m: (m, 0))] if n_x == 1 else _split_specs(tm, D_MODEL)
    in_specs = a_specs + [pl.BlockSpec((1, kd, D_MODEL), lambda m: (layer, 0, 0))] + x_specs + [
        pl.BlockSpec((1, 1, 8, D_MODEL), lambda m: (pack_layer, (m * tm) // GROUP_TOKENS, 0, 0)),
        pl.BlockSpec((1, 1, 8, D_MODEL), lambda m: (next_layer, (m * tm) // GROUP_TOKENS, 0, 0)),
    ]
    args = list(a_list) + [w] + list(x_list) + [pack, pack]
    out_specs = [pl.BlockSpec((tm, D_MODEL), lambda m: (m, 0)),
                 pl.BlockSpec((tm, D_MODEL), lambda m: (m, 0))]
    out_shape = [jax.ShapeDtypeStruct((N_TOK, D_MODEL), F32),
                 jax.ShapeDtypeStruct((N_TOK, D_MODEL), BF16)]
    if router:
        in_specs += [pl.BlockSpec((1, D_MODEL, LANES), lambda m: (router_layer, 0, 0)),
                     pl.BlockSpec((1, D_MODEL, LANES), lambda m: (router_layer, 0, 0))]
        args += list(w_router)
        out_specs += [pl.BlockSpec((tm, D_MODEL), lambda m: (m, 0)),
                      pl.BlockSpec((tm, LANES), lambda m: (m, 0))]
        out_shape += [jax.ShapeDtypeStruct((N_TOK, D_MODEL), F32),
                      jax.ShapeDtypeStruct((N_TOK, LANES), F32)]
    kern = functools.partial(_down_kernel, n_a=n_a, n_x=n_x, first_sample_tile=fst, gate_row=gate_row,
                             next_rows=next_rows, router=router)
    return pl.pallas_call(
        kern,
        grid=(N_TOK // tm,),
        in_specs=in_specs,
        out_specs=out_specs,
        out_shape=out_shape,
        scratch_shapes=[pltpu.VMEM((kd, D_MODEL), BF16)],
        compiler_params=_cparams(("arbitrary",)),
        name="down_proj",
    )(*args)


def _sink_row(sink_ref, layer, kvh, width):
    return jnp.concatenate(
        [jnp.full((1, width), sink_ref[layer * N_HEADS + kvh * GROUP + g] * LOG2E, F32) for g in range(GROUP)],
        axis=1)


def _group_queries(q_ref, kvh):
    return jnp.concatenate(
        [q_ref[:, (kvh * GROUP + g) * HEAD_DIM:(kvh * GROUP + g + 1) * HEAD_DIM] for g in range(GROUP)], axis=0)


_NT = (((1,), (1,)), ((), ()))


def _ctx_attn_kernel(sink_ref, q_ref, kv_ref, o_ref, *, layer):
    kvw = N_KV_HEADS * HEAD_DIM
    k_all = kv_ref[:, :kvw].astype(BF16)
    v_t = kv_ref[:, kvw:].T.astype(BF16)
    outs = []
    for kvh in range(N_KV_HEADS):
        hs = slice(kvh * HEAD_DIM, (kvh + 1) * HEAD_DIM)
        s_t = lax.dot_general(k_all[:, hs], _group_queries(q_ref, kvh), _NT, preferred_element_type=F32)
        sink = _sink_row(sink_ref, layer, kvh, SEQ)
        m = jnp.maximum(jnp.max(s_t, axis=0, keepdims=True), sink)
        p = jnp.exp2(s_t - m)
        denom = jnp.sum(p, axis=0, keepdims=True) + jnp.exp2(sink - m)
        o_t = jnp.dot(v_t[hs], p.astype(BF16), preferred_element_type=F32) / denom
        outs += [o_t[:, g * SEQ:(g + 1) * SEQ] for g in range(GROUP)]
    o_ref[...] = jnp.concatenate(outs, axis=0).T.astype(BF16)


def _ctx_attention(q, kv, sink_flat, layer):
    kern = functools.partial(_ctx_attn_kernel, layer=layer)
    return pl.pallas_call(
        kern,
        grid_spec=pltpu.PrefetchScalarGridSpec(
            num_scalar_prefetch=1,
            grid=(BATCH,),
            in_specs=[pl.BlockSpec((SEQ, D_MODEL), lambda b, s: (b, 0)),
                      pl.BlockSpec((SEQ, 2 * N_KV_HEADS * HEAD_DIM), lambda b, s: (b, 0))],
            out_specs=pl.BlockSpec((SEQ, D_MODEL), lambda b, s: (b, 0))),
        out_shape=jax.ShapeDtypeStruct((N_PROMPT, D_MODEL), BF16),
        compiler_params=_cparams(("arbitrary",)),
        name="ctx_attention",
    )(sink_flat, q, kv)


def _lat_attn_kernel(sink_ref, q_ref, kvp_ref, kvc_ref, kvn_ref, ck_ref, cv_ref, o_ref, kc_ref, vct_ref, *, layer):
    blk = pl.program_id(1)
    kvw = N_KV_HEADS * HEAD_DIM
    cols = GROUP * BLOCK
    n_lat = 3 * BLOCK

    @pl.when(blk == 0)
    def _():
        kc_ref[...] = ck_ref[0, 0].astype(BF16)
        vct_ref[...] = cv_ref[0, 0].T.astype(BF16)

    j = lax.broadcasted_iota(I32, (n_lat, cols), 0)
    r = lax.broadcasted_iota(I32, (n_lat, cols), 1) % BLOCK
    lo = jnp.maximum(r, BLOCK - BLOCK * blk)
    hi = jnp.minimum(r + 2 * WINDOW, DEC_SEQ + BLOCK - 1 - BLOCK * blk)
    lat_mask = jnp.logical_and(j >= lo, j <= hi)
    k_lat = jnp.concatenate([kvp_ref[:, :kvw], kvc_ref[:, :kvw], kvn_ref[:, :kvw]], axis=0).astype(BF16)
    v_lat_t = jnp.concatenate([kvp_ref[:, kvw:].T, kvc_ref[:, kvw:].T, kvn_ref[:, kvw:].T], axis=1).astype(BF16)
    outs = []
    for kvh in range(N_KV_HEADS):
        hs = slice(kvh * HEAD_DIM, (kvh + 1) * HEAD_DIM)
        q4 = _group_queries(q_ref, kvh)
        s_ctx = lax.dot_general(kc_ref[:, hs], q4, _NT, preferred_element_type=F32)
        s_lat = lax.dot_general(k_lat[:, hs], q4, _NT, preferred_element_type=F32)
        s_lat = jnp.where(lat_mask, s_lat, NEG)
        sink = _sink_row(sink_ref, layer, kvh, BLOCK)
        m = jnp.maximum(jnp.maximum(jnp.max(s_ctx, axis=0, keepdims=True),
                                    jnp.max(s_lat, axis=0, keepdims=True)), sink)
        p_ctx = jnp.exp2(s_ctx - m)
        p_lat = jnp.exp2(s_lat - m)
        denom = (jnp.sum(p_ctx, axis=0, keepdims=True) + jnp.sum(p_lat, axis=0, keepdims=True)
                 + jnp.exp2(sink - m))
        o_t = (jnp.dot(vct_ref[hs, :], p_ctx.astype(BF16), preferred_element_type=F32)
               + jnp.dot(v_lat_t[hs], p_lat.astype(BF16), preferred_element_type=F32)) / denom
        outs += [o_t[:, g * BLOCK:(g + 1) * BLOCK] for g in range(GROUP)]
    o_ref[...] = jnp.concatenate(outs, axis=0).T.astype(BF16)


def _lat_attention(q, kv, cache_k4, cache_v4, sink_flat, layer):
    nb = DEC_SEQ // BLOCK
    base = N_PROMPT // BLOCK
    kvc = 2 * N_KV_HEADS * HEAD_DIM

    def row(b, i):
        return base + b * nb + i

    kern = functools.partial(_lat_attn_kernel, layer=layer)
    return pl.pallas_call(
        kern,
        grid_spec=pltpu.PrefetchScalarGridSpec(
            num_scalar_prefetch=1,
            grid=(DEC_BATCH, nb),
            in_specs=[pl.BlockSpec((BLOCK, D_MODEL), lambda b, i, s: (row(b, i), 0)),
                      pl.BlockSpec((BLOCK, kvc), lambda b, i, s: (row(b, jnp.maximum(i - 1, 0)), 0)),
                      pl.BlockSpec((BLOCK, kvc), lambda b, i, s: (row(b, i), 0)),
                      pl.BlockSpec((BLOCK, kvc), lambda b, i, s: (row(b, jnp.minimum(i + 1, nb - 1)), 0)),
                      pl.BlockSpec((1, 1, PAST_LEN, N_KV_HEADS * HEAD_DIM), lambda b, i, s: (b, layer, 0, 0)),
                      pl.BlockSpec((1, 1, PAST_LEN, N_KV_HEADS * HEAD_DIM), lambda b, i, s: (b, layer, 0, 0))],
            out_specs=pl.BlockSpec((BLOCK, D_MODEL), lambda b, i, s: (b * nb + i, 0)),
            scratch_shapes=[pltpu.VMEM((PAST_LEN, N_KV_HEADS * HEAD_DIM), BF16),
                            pltpu.VMEM((N_KV_HEADS * HEAD_DIM, PAST_LEN), BF16)]),
        out_shape=jax.ShapeDtypeStruct((N_SAMPLE, D_MODEL), BF16),
        compiler_params=_cparams(("arbitrary", "arbitrary")),
        name="lat_attention",
    )(sink_flat, q, kv, kv, kv, cache_k4, cache_v4)


RET_SUB = 2
RET_ROWS = RET_SUB * BLOCK
RET_STEPS = N_TOK // RET_ROWS
PROMPT_STEPS = N_PROMPT // RET_ROWS
STEPS_PER_SAMPLE = DEC_SEQ // RET_ROWS
assert SEQ == RET_ROWS


def _ret_kernel(lg_ref, q_ref, k_ref, v_ref, gate_ref, s0_ref, *rest, backward, layer):
    part_ref = rest[0] if backward else None
    o_ref, sout_ref, state_ref = rest[-3:]
    step = pl.program_id(0)
    blk = (RET_STEPS - 1 - step) if backward else step
    in_prompt = blk < PROMPT_STEPS
    first_of_sample = (blk - PROMPT_STEPS) % STEPS_PER_SAMPLE == (STEPS_PER_SAMPLE - 1 if backward else 0)

    @pl.when(in_prompt)
    def _():
        state_ref[...] = jnp.zeros_like(state_ref)

    @pl.when(jnp.logical_and(jnp.logical_not(in_prompt), first_of_sample))
    def _():
        state_ref[...] = s0_ref[0, 0]

    ii = lax.broadcasted_iota(I32, (BLOCK, BLOCK), 0).astype(F32)
    jj = lax.broadcasted_iota(I32, (BLOCK, BLOCK), 1).astype(F32)
    dist = (jj - ii) if backward else (ii - jj)
    ti = lax.broadcasted_iota(I32, (BLOCK, 1), 0).astype(F32)
    q_pow = (BLOCK - ti) if backward else (ti + 1.0)
    k_pow = ti if backward else (BLOCK - 1.0 - ti)

    for h in range(RET_HEADS):
        lg = lg_ref[(layer * 2 + (1 if backward else 0)) * RET_HEADS + h]
        intra = jnp.where(dist >= 0, jnp.exp(lg * jnp.maximum(dist, 0.0)), 0.0)
        q_dec = jnp.exp(lg * q_pow)
        k_dec = jnp.exp(lg * k_pow)
        c_dec = jnp.exp(lg * BLOCK)
        for sub in (reversed(range(RET_SUB)) if backward else range(RET_SUB)):
            rows = pl.ds(sub * BLOCK, BLOCK)
            q = q_ref[rows, h * RET_DK:(h + 1) * RET_DK]
            k = k_ref[rows, h * RET_DK:(h + 1) * RET_DK]
            v = v_ref[rows, h * RET_DV:(h + 1) * RET_DV]
            s = state_ref[h]
            a = lax.dot_general(q, k, (((1,), (1,)), ((), ())), preferred_element_type=F32) * intra
            o = (jnp.dot(a.astype(BF16), v, preferred_element_type=F32)
                 + q_dec * jnp.dot(q, s.astype(BF16), preferred_element_type=F32))
            kd = (k.astype(F32) * k_dec).astype(BF16)
            state_ref[h] = c_dec * s + lax.dot_general(kd, v, (((0,), (0,)), ((), ())), preferred_element_type=F32)
            gate = gate_ref[rows, h * RET_DV:(h + 1) * RET_DV].astype(F32)
            on = o * lax.rsqrt(jnp.mean(o * o, axis=-1, keepdims=True) + EPS)
            res = on * gate
            if backward:
                res = res + part_ref[rows, h * RET_DV:(h + 1) * RET_DV].astype(F32)
            o_ref[rows, h * RET_DV:(h + 1) * RET_DV] = res.astype(BF16)

    @pl.when(in_prompt)
    def _():
        if sout_ref.shape[1] == 1:
            sout_ref[0, 0] = state_ref[...]
        else:
            for l in range(sout_ref.shape[1]):
                sout_ref[0, l] = state_ref[...] if l == layer else jnp.zeros_like(state_ref)


def _retention_pass(proj, s0, log_gamma_flat, layer, backward, partial=None, states=None):
    def blk(i):
        return (RET_STEPS - 1 - i) if backward else i

    def s0_map(i, lg):
        return (jnp.clip((blk(i) - PROMPT_STEPS) // STEPS_PER_SAMPLE, 0, DEC_BATCH - 1), layer, 0, 0, 0)

    sout_layers = N_ODD if states is None else 1

    def sout_map(i, lg):
        return (jnp.minimum(blk(i), BATCH - 1), 0 if states is None else layer, 0, 0, 0)

    gate_block = 3 if backward else 2
    in_specs = [pl.BlockSpec((RET_ROWS, RET_HK), lambda i, lg: (blk(i), 0)),
                pl.BlockSpec((RET_ROWS, RET_HK), lambda i, lg: (blk(i), 1)),
                pl.BlockSpec((RET_ROWS, RET_HV), lambda i, lg: (blk(i), 1)),
                pl.BlockSpec((RET_ROWS, RET_HV), lambda i, lg: (blk(i), gate_block)),
                pl.BlockSpec((1, 1, RET_HEADS, RET_DK, RET_DV), s0_map)]
    args = [proj, proj, proj, proj, s0]
    if backward:
        in_specs.append(pl.BlockSpec((RET_ROWS, RET_HV), lambda i, lg: (blk(i), 0)))
        args.append(partial)
    aliases = {}
    if states is not None:
        in_specs.append(pl.BlockSpec(memory_space=pl.ANY))
        args.append(states)
        aliases = {len(args): 1}
    kern = functools.partial(_ret_kernel, backward=backward, layer=layer)
    return pl.pallas_call(
        kern,
        grid_spec=pltpu.PrefetchScalarGridSpec(
            num_scalar_prefetch=1,
            grid=(RET_STEPS,),
            in_specs=in_specs,
            out_specs=[pl.BlockSpec((RET_ROWS, RET_HV), lambda i, lg: (blk(i), 0)),
                       pl.BlockSpec((1, sout_layers, RET_HEADS, RET_DK, RET_DV), sout_map)],
            scratch_shapes=[pltpu.VMEM((RET_HEADS, RET_DK, RET_DV), F32)]),
        out_shape=[jax.ShapeDtypeStruct((N_TOK, RET_HV), BF16),
                   jax.ShapeDtypeStruct((BATCH, N_ODD, RET_HEADS, RET_DK, RET_DV), F32)],
        input_output_aliases=aliases,
        compiler_params=_cparams(("arbitrary",)),
        name="retention_bwd" if backward else "retention_fwd",
    )(log_gamma_flat, *args)


def _dispatch_kernel(slot_ref, pend_ref, h_ref, o_hbm, zbuf, sem):
    i = pl.program_id(0)

    @pl.when(i == 0)
    def _():
        zbuf[...] = jnp.zeros_like(zbuf)

        def tile_fill(start):
            return pltpu.make_async_copy(zbuf, o_hbm.at[pl.ds(pl.multiple_of(start, MOE_TILE), MOE_TILE)], sem)

        def nonempty(e):
            return pend_ref[e] > (pend_ref[e - 1] if e else 0)

        def start_unused(t, carry):
            tile_fill(t * MOE_TILE).start()
            return carry

        def wait_unused(t, carry):
            tile_fill(t * MOE_TILE).wait()
            return carry

        first_unused = pend_ref[N_EXPERTS - 1] // MOE_TILE
        for e in range(N_EXPERTS):
            pl.when(nonempty(e))(lambda e=e: tile_fill(pend_ref[e] - MOE_TILE).start())
        lax.fori_loop(first_unused, MOE_NT, start_unused, 0)
        for e in range(N_EXPERTS):
            pl.when(nonempty(e))(lambda e=e: tile_fill(pend_ref[e] - MOE_TILE).wait())
        lax.fori_loop(first_unused, MOE_NT, wait_unused, 0)

    base = i * DISPATCH_ROWS * TOP_K

    def start(r, carry):
        for k in range(TOP_K):
            pltpu.make_async_copy(h_ref.at[pl.ds(r, 1)], o_hbm.at[pl.ds(slot_ref[base + r * TOP_K + k], 1)],
                                  sem).start()
        return carry

    lax.fori_loop(0, DISPATCH_ROWS, start, 0, unroll=8)
    for k in range(TOP_K):
        pltpu.make_async_copy(h_ref, o_hbm.at[pl.ds(0, DISPATCH_ROWS)], sem).wait()


def _dispatch_rows(slot, pend, h):
    return pl.pallas_call(
        _dispatch_kernel,
        grid_spec=pltpu.PrefetchScalarGridSpec(
            num_scalar_prefetch=2,
            grid=(N_TOK // DISPATCH_ROWS,),
            in_specs=[pl.BlockSpec((DISPATCH_ROWS, D_MODEL), lambda i, s, p: (i, 0))],
            out_specs=pl.BlockSpec(memory_space=pl.ANY),
            scratch_shapes=[pltpu.VMEM((MOE_TILE, D_MODEL), F32), pltpu.SemaphoreType.DMA(())]),
        out_shape=jax.ShapeDtypeStruct((MOE_SLOTS, D_MODEL), F32),
        compiler_params=_cparams(("arbitrary",)),
        name="moe_dispatch",
    )(slot, pend, h)


def _for_occupied_chunks(n_rows, o_ref, chunk):
    n_chunks = MOE_TILE // MOE_CHUNK
    occupied = (n_rows + MOE_CHUNK - 1) // MOE_CHUNK

    def path(k):
        for c in range(n_chunks):
            rows = pl.ds(c * MOE_CHUNK, MOE_CHUNK)
            if c < k:
                chunk(rows)
            else:
                o_ref[rows, :] = jnp.zeros((MOE_CHUNK, o_ref.shape[1]), o_ref.dtype)

    for k in range(n_chunks + 1):
        pl.when(occupied == k)(functools.partial(path, k))


def _moe_up_kernel(te_ref, nv_ref, tr_ref, h_ref, wg_ref, wu_ref, o_ref, wgb_ref, wub_ref):
    m = pl.program_id(1)
    new_w = jnp.logical_or(m == 0, te_ref[m] != te_ref[jnp.maximum(m - 1, 0)])

    @pl.when(new_w)
    def _():
        wgb_ref[...] = wg_ref[0, 0].astype(BF16)
        wub_ref[...] = wu_ref[0, 0].astype(BF16)

    def chunk(rows):
        h = h_ref[rows, :].astype(BF16)
        g = jnp.dot(h, wgb_ref[...], preferred_element_type=F32)
        u = jnp.dot(h, wub_ref[...], preferred_element_type=F32)
        o_ref[rows, :] = (g * jax.nn.sigmoid(g) * u).astype(BF16)

    _for_occupied_chunks(tr_ref[m], o_ref, chunk)


def _moe_up(tile_expert, n_valid, tile_rows, hs, w_gu, layer):
    tf = 1792
    nf = D_FF_EXPERT // tf
    return pl.pallas_call(
        _moe_up_kernel,
        grid_spec=pltpu.PrefetchScalarGridSpec(
            num_scalar_prefetch=3,
            grid=(nf, MOE_NT),
            in_specs=[pl.BlockSpec((MOE_TILE, D_MODEL), lambda f, m, te, nv, tr: (jnp.minimum(m, nv[0] - 1), 0)),
                      pl.BlockSpec((1, 1, D_MODEL, tf), lambda f, m, te, nv, tr: (layer, te[m], 0, f)),
                      pl.BlockSpec((1, 1, D_MODEL, tf), lambda f, m, te, nv, tr: (layer, te[m], 0, nf + f))],
            out_specs=pl.BlockSpec((MOE_TILE, tf), lambda f, m, te, nv, tr: (m, f)),
            scratch_shapes=[pltpu.VMEM((D_MODEL, tf), BF16), pltpu.VMEM((D_MODEL, tf), BF16)]),
        out_shape=jax.ShapeDtypeStruct((MOE_SLOTS, D_FF_EXPERT), BF16),
        compiler_params=_cparams(("arbitrary", "arbitrary")),
        name="moe_up",
    )(tile_expert, n_valid, tile_rows, hs, w_gu, w_gu)


def _moe_down_kernel(te_ref, nv_ref, tr_ref, a_ref, w_ref, o_ref, wb_ref):
    m = pl.program_id(1)
    new_w = jnp.logical_or(m == 0, te_ref[m] != te_ref[jnp.maximum(m - 1, 0)])

    @pl.when(new_w)
    def _():
        wb_ref[...] = w_ref[0, 0].astype(BF16)

    def chunk(rows):
        o_ref[rows, :] = jnp.dot(a_ref[rows, :], wb_ref[...], preferred_element_type=F32)

    _for_occupied_chunks(tr_ref[m], o_ref, chunk)


def _moe_down(tile_expert, n_valid, tile_rows, act, w_down, layer):
    tn = D_MODEL
    return pl.pallas_call(
        _moe_down_kernel,
        grid_spec=pltpu.PrefetchScalarGridSpec(
            num_scalar_prefetch=3,
            grid=(D_MODEL // tn, MOE_NT),
            in_specs=[pl.BlockSpec((MOE_TILE, D_FF_EXPERT), lambda n, m, te, nv, tr: (jnp.minimum(m, nv[0] - 1), 0)),
                      pl.BlockSpec((1, 1, D_FF_EXPERT, tn), lambda n, m, te, nv, tr: (layer, te[m], 0, n))],
            out_specs=pl.BlockSpec((MOE_TILE, tn), lambda n, m, te, nv, tr: (m, n)),
            scratch_shapes=[pltpu.VMEM((D_FF_EXPERT, tn), BF16)]),
        out_shape=jax.ShapeDtypeStruct((MOE_SLOTS, D_MODEL), F32),
        compiler_params=_cparams(("arbitrary", "arbitrary")),
        name="moe_down",
    )(tile_expert, n_valid, tile_rows, act, w_down)


def _combine_kernel(slot_ref, y_hbm, x_ref, rt_ref, pk_ref, pkn_ref, fin_ref, *rest, tok0, final):
    if final:
        out_ref, ybuf, sems = rest
    else:
        xo_ref, h_ref, ybuf, sems = rest
    tc = TC_COMBINE
    i = pl.program_id(0)
    n = pl.num_programs(0)
    cur = i % 2
    nxt = 1 - cur

    def issue_rows(tile, buf, r0):
        base = (tok0 + tile * tc) * TOP_K
        for r in range(COMBINE_GROUP):
            for k in range(TOP_K):
                pltpu.make_async_copy(y_hbm.at[pl.ds(slot_ref[base + (r0 + r) * TOP_K + k], 1)],
                                      ybuf.at[buf, k, pl.ds(r0 + r, 1)], sems.at[buf]).start()

    def wait_tile(buf):
        for k in range(TOP_K):
            pltpu.make_async_copy(y_hbm.at[pl.ds(0, tc)], ybuf.at[buf, k], sems.at[buf]).wait()

    def group_rows(g):
        return pl.multiple_of(g * COMBINE_GROUP, COMBINE_GROUP)

    @pl.when(i == 0)
    def _():
        def first(g, carry):
            issue_rows(0, 0, group_rows(g))
            return carry

        lax.fori_loop(0, tc // COMBINE_GROUP, first, 0)

    wait_tile(cur)
    nxt_tile = jnp.minimum(i + 1, n - 1)
    pk = pk_ref[0, 0]
    pkn = pkn_ref[0, 0]

    def group(g, carry):
        r0 = group_rows(g)
        issue_rows(nxt_tile, nxt, r0)
        rows = pl.ds(r0, COMBINE_GROUP)
        rt = rt_ref[rows, :]
        moe = rt[:, 2:3] * ybuf[cur, 0, rows, :] + rt[:, 3:4] * ybuf[cur, 1, rows, :]
        x_new = x_ref[rows, :] + pk[R_GATE_FFN:R_GATE_FFN + 1] * moe
        if final:
            out_ref[rows, :] = _epilogue(x_new, None, None, fin_ref, True)
        else:
            xo_ref[rows, :] = x_new
            h_ref[rows, :] = _epilogue(x_new, pkn, (R_G_MIX, R_SCALE_MIX, R_SHIFT_MIX), None, False).astype(BF16)
        return carry

    lax.fori_loop(0, tc // COMBINE_GROUP, group, 0, unroll=True)

    @pl.when(i == n - 1)
    def _():
        wait_tile(nxt)


def _combine(slot, y, x, route, pack, layer, next_layer, final_norm, tok0, n_rows, final):
    tc = TC_COMBINE
    b0 = tok0 // tc

    def tmap(m, s):
        return (b0 + m, 0)

    def pmap(l):
        return lambda m, s: (l, (tok0 + m * tc) // GROUP_TOKENS, 0, 0)

    if final:
        out_specs = pl.BlockSpec((tc, D_MODEL), lambda m, s: (m, 0))
        out_shape = jax.ShapeDtypeStruct((n_rows, D_MODEL), F32)
    else:
        out_specs = [pl.BlockSpec((tc, D_MODEL), lambda m, s: (m, 0)),
                     pl.BlockSpec((tc, D_MODEL), lambda m, s: (m, 0))]
        out_shape = [jax.ShapeDtypeStruct((n_rows, D_MODEL), F32),
                     jax.ShapeDtypeStruct((n_rows, D_MODEL), BF16)]
    kern = functools.partial(_combine_kernel, tok0=tok0, final=final)
    return pl.pallas_call(
        kern,
        grid_spec=pltpu.PrefetchScalarGridSpec(
            num_scalar_prefetch=1,
            grid=(n_rows // tc,),
            in_specs=[pl.BlockSpec(memory_space=pl.ANY),
                      pl.BlockSpec((tc, D_MODEL), tmap),
                      pl.BlockSpec((tc, LANES), tmap),
                      pl.BlockSpec((1, 1, 8, D_MODEL), pmap(layer)),
                      pl.BlockSpec((1, 1, 8, D_MODEL), pmap(next_layer)),
                      pl.BlockSpec((1, D_MODEL), lambda m, s: (0, 0))],
            out_specs=out_specs,
            scratch_shapes=[pltpu.VMEM((2, TOP_K, tc, D_MODEL), F32), pltpu.SemaphoreType.DMA((2,))]),
        out_shape=out_shape,
        compiler_params=_cparams(("arbitrary",)),
        name="moe_combine",
    )(slot, y, x, route, pack, pack, final_norm.reshape(1, D_MODEL))


def _routing_tables(route):
    e_flat = route[:, :TOP_K].astype(I32).reshape(-1)
    onehot = (e_flat[:, None] == jnp.arange(N_EXPERTS, dtype=I32)[None, :]).astype(I32)
    csum = jnp.cumsum(onehot, axis=0)
    rank = jnp.sum(onehot * (csum - 1), axis=1)
    count = csum[-1]
    padded = ((count + MOE_TILE - 1) // MOE_TILE) * MOE_TILE
    pend = jnp.cumsum(padded)
    poff = pend - padded
    slot = jnp.sum(onehot * poff[None, :], axis=1) + rank
    n_valid = pend[-1] // MOE_TILE
    tile_start = jnp.arange(MOE_NT, dtype=I32) * MOE_TILE
    te_raw = jnp.minimum(jnp.sum((tile_start[:, None] >= pend[None, :]).astype(I32), axis=1), N_EXPERTS - 1)
    last_e = jnp.max(jnp.where(tile_start < pend[-1], te_raw, 0))
    tile_expert = jnp.minimum(te_raw, last_e)
    group_rows_end = jnp.sum((te_raw[:, None] == jnp.arange(N_EXPERTS, dtype=I32)[None, :]) * (poff + count)[None, :],
                             axis=1)
    tile_rows = jnp.where(tile_start < pend[-1], jnp.clip(group_rows_end - tile_start, 0, MOE_TILE), 0)
    return (slot.astype(I32), pend.astype(I32), tile_expert.astype(I32), n_valid.astype(I32).reshape(1),
            tile_rows.astype(I32))


def _rope_tables(dh):
    nf = dh // 4
    t = jnp.arange(DEC_SEQ)
    freqs = ROPE_BASE ** (-jnp.arange(nf, dtype=F32) / nf)
    row = (t // GRID_W).astype(F32)
    col = (t % GRID_W).astype(F32)
    ang = jnp.concatenate([row[:, None] * freqs, col[:, None] * freqs], axis=-1)
    return jnp.cos(ang), jnp.sin(ang)


def kernel(x_prompt, x_sample, cache_k, cache_v, state_fwd, state_bwd, c, c_ctx, norm_mix, norm_ffn, w_mod, b_mod,
           w_qkv, w_attn_o, attn_sink, w_ret_in, ret_decay, w_ret_out, w_ffn_gu, w_ffn_down, w_router, w_exp_gu,
           w_exp_down, final_norm):
    x = (x_prompt.reshape(N_PROMPT, D_MODEL), x_sample.reshape(N_SAMPLE, D_MODEL))

    cond8 = jnp.concatenate([c_ctx[None, :], c, jnp.zeros((8 - 1 - DEC_BATCH, D_MODEL), F32)], axis=0)
    mods = _modulations(cond8, w_mod, b_mod)
    m3 = mods[:, :N_GROUPS].reshape(DEPTH, N_GROUPS, 6, D_MODEL)
    pack = jnp.concatenate([
        m3,
        jnp.broadcast_to(norm_mix[:, None, None, :], (DEPTH, N_GROUPS, 1, D_MODEL)),
        jnp.broadcast_to(norm_ffn[:, None, None, :], (DEPTH, N_GROUPS, 1, D_MODEL))], axis=2)

    cos_a, sin_a = _rope_tables(HEAD_DIM)
    cos_attn = jnp.tile(cos_a, (1, LANES // (HEAD_DIM // 2)))
    sin_attn = jnp.tile(jnp.concatenate([-sin_a, sin_a], axis=1), (1, LANES // HEAD_DIM))
    cos_ret, sin_ret = _rope_tables(RET_DK)

    kvw = N_KV_HEADS * HEAD_DIM
    cache_k4 = cache_k.reshape(DEC_BATCH, N_EVEN, PAST_LEN, kvw)
    cache_v4 = cache_v.reshape(DEC_BATCH, N_EVEN, PAST_LEN, kvw)
    sink_flat = attn_sink.reshape(-1)
    log_gamma = jax.nn.log_sigmoid(ret_decay.astype(F32)).reshape(-1)
    w_router_pad = jnp.pad(w_router, ((0, 0), (0, 0), (0, LANES - N_EXPERTS)))
    w_router_hi = w_router_pad.astype(BF16)
    w_router_lo = (w_router_pad - w_router_hi.astype(F32)).astype(BF16)

    mix_rows = (R_G_MIX, R_SCALE_MIX, R_SHIFT_MIX)
    ffn_rows = (R_G_FFN, R_SCALE_FFN, R_SHIFT_FFN)

    new_k, new_v = [], []
    new_sf = new_sb = None
    h = _pre(x, pack)
    y_prompt = y_sample = None
    for i in range(DEPTH):
        j = i // 2
        if i % 2 == 0:
            q = _attn_proj(h, w_qkv, j, cos_attn, sin_attn, col0=0, ncols=N_HEADS * HEAD_DIM,
                           rope_cols=N_HEADS * HEAD_DIM, scale=HEAD_DIM ** -0.5 * LOG2E, out_dtype=BF16)
            kv = _attn_proj(h, w_qkv, j, cos_attn, sin_attn, col0=N_HEADS * HEAD_DIM, ncols=2 * kvw,
                            rope_cols=kvw, scale=1.0, out_dtype=F32)
            new_k.append(kv[:N_PROMPT, :kvw].reshape(BATCH, SEQ, N_KV_HEADS, HEAD_DIM))
            new_v.append(kv[:N_PROMPT, kvw:].reshape(BATCH, SEQ, N_KV_HEADS, HEAD_DIM))
            o_ctx = _ctx_attention(q, kv, sink_flat, j)
            o_lat = _lat_attention(q, kv, cache_k4, cache_v4, sink_flat, j)
            x, h = _down([o_ctx, o_lat], w_attn_o, j, x, pack, i, R_GATE_MIX, i, ffn_rows)
            act = _swiglu_up(h, w_ffn_gu, j)
            x, h = _down([act], w_ffn_down, j, x, pack, i, R_GATE_FFN, i + 1, mix_rows)
        else:
            proj = _ret_proj(h, w_ret_in, j, cos_ret, sin_ret)
            part, new_sf = _retention_pass(proj, state_fwd, log_gamma, j, False, states=new_sf)
            o_ret, new_sb = _retention_pass(proj, state_bwd, log_gamma, j, True, partial=part, states=new_sb)
            x, h, hf, route = _down([o_ret], w_ret_out, j, x, pack, i, R_GATE_MIX, i, ffn_rows,
                                    w_router=(w_router_hi, w_router_lo), router_layer=j)
            slot, pend, tile_expert, n_valid, tile_rows = _routing_tables(route)
            hs = _dispatch_rows(slot, pend, hf)
            act = _moe_up(tile_expert, n_valid, tile_rows, hs, w_exp_gu, j)
            ys = _moe_down(tile_expert, n_valid, tile_rows, act, w_exp_down, j)
            if i == DEPTH - 1:
                y_prompt = _combine(slot, ys, x, route, pack, i, i, final_norm, 0, N_PROMPT, True)
                y_sample = _combine(slot, ys, x, route, pack, i, i, final_norm, N_PROMPT, N_SAMPLE, True)
            else:
                x, h = _combine(slot, ys, x, route, pack, i, i + 1, final_norm, 0, N_TOK, False)

    return (y_prompt.reshape(BATCH, SEQ, D_MODEL), y_sample.reshape(DEC_BATCH, DEC_SEQ, D_MODEL),
            jnp.stack(new_k, axis=1), jnp.stack(new_v, axis=1), new_sf, new_sb)
```

```python
import functools

import jax
import jax.numpy as jnp
from jax import lax
from jax.experimental import pallas as pl
from jax.experimental.pallas import tpu as pltpu

F32 = jnp.float32
BF16 = jnp.bfloat16
I32 = jnp.int32

D_MODEL = 1024
BATCH = 16
SEQ = 256
DEPTH = 4
DEC_BATCH = 2
DEC_SEQ = 4096
PAST_LEN = 256
GRID_W = 64
BLOCK = 128
EPS = 1e-6
N_HEADS = 16
N_KV_HEADS = 4
HEAD_DIM = 64
GROUP = N_HEADS // N_KV_HEADS
WINDOW = 128
ROPE_BASE = 10000.0
RET_HEADS = 4
RET_DK = 256
RET_DV = 512
RET_HK = RET_HEADS * RET_DK
RET_HV = RET_HEADS * RET_DV
D_FF = 2816
N_EXPERTS = 8
TOP_K = 2
D_FF_EXPERT = 3584
N_EVEN = 2
N_ODD = 2
NEG = -1e30
LOG2E = 1.4426950408889634

GROUP_TOKENS = 4096
N_PROMPT = BATCH * SEQ
N_SAMPLE = DEC_BATCH * DEC_SEQ
N_TOK = N_PROMPT + N_SAMPLE
N_GROUPS = N_TOK // GROUP_TOKENS

R_SHIFT_MIX, R_SCALE_MIX, R_GATE_MIX, R_SHIFT_FFN, R_SCALE_FFN, R_GATE_FFN, R_G_MIX, R_G_FFN = range(8)

VMEM_LIMIT_BYTES = 56 * 1024 * 1024
LANES = 128

TM = 2048
UP_CHUNK = 512
TM_DOWN = 512
DOWN_CHUNK = 128
MOE_TILE = 512
MOE_CHUNK = 256
MOE_SLOTS = TOP_K * N_TOK + N_EXPERTS * MOE_TILE
MOE_NT = MOE_SLOTS // MOE_TILE
TC_COMBINE = 256
COMBINE_GROUP = 16
DISPATCH_ROWS = 256


def _cparams(sem):
    return pltpu.CompilerParams(dimension_semantics=sem, vmem_limit_bytes=VMEM_LIMIT_BYTES)


def _normmod(x, g, scale, shift):
    ms = jnp.mean(x * x, axis=-1, keepdims=True)
    return (x * lax.rsqrt(ms + EPS) * g) * (1.0 + scale) + shift


def _mod_kernel(cond_ref, w_ref, b_ref, o_ref):
    c = cond_ref[...]
    s = c * jax.nn.sigmoid(c)
    o_ref[0] = jnp.dot(s.astype(BF16), w_ref[0].astype(BF16), preferred_element_type=F32) + b_ref[0]


def _modulations(cond8, w_mod, b_mod):
    tn = 2048
    n6 = 6 * D_MODEL
    return pl.pallas_call(
        _mod_kernel,
        grid=(DEPTH, n6 // tn),
        in_specs=[pl.BlockSpec((8, D_MODEL), lambda l, n: (0, 0)),
                  pl.BlockSpec((1, D_MODEL, tn), lambda l, n: (l, 0, n)),
                  pl.BlockSpec((1, 1, tn), lambda l, n: (l, 0, n))],
        out_specs=pl.BlockSpec((1, 8, tn), lambda l, n: (l, 0, n)),
        out_shape=jax.ShapeDtypeStruct((DEPTH, 8, n6), F32),
        compiler_params=_cparams(("arbitrary", "arbitrary")),
        name="modulations",
    )(cond8, w_mod, b_mod.reshape(DEPTH, 1, n6))


def _split_specs(tm, width):
    fst = N_PROMPT // tm
    return [pl.BlockSpec((tm, width), lambda m: (jnp.minimum(m, fst - 1), 0)),
            pl.BlockSpec((tm, width), lambda m: (jnp.maximum(m - fst, 0), 0))]


def _pre_kernel(xp_ref, xs_ref, pk_ref, h_ref, *, first_sample_tile):
    def norm(x_ref):
        pk = pk_ref[0, 0]
        h = _normmod(x_ref[...], pk[R_G_MIX:R_G_MIX + 1], pk[R_SCALE_MIX:R_SCALE_MIX + 1],
                     pk[R_SHIFT_MIX:R_SHIFT_MIX + 1])
        h_ref[...] = h.astype(BF16)

    m = pl.program_id(0)
    pl.when(m < first_sample_tile)(lambda: norm(xp_ref))
    pl.when(m >= first_sample_tile)(lambda: norm(xs_ref))


def _pre(x_pair, pack):
    tm = TM
    return pl.pallas_call(
        functools.partial(_pre_kernel, first_sample_tile=N_PROMPT // tm),
        grid=(N_TOK // tm,),
        in_specs=_split_specs(tm, D_MODEL) + [
            pl.BlockSpec((1, 1, 8, D_MODEL), lambda m: (0, (m * tm) // GROUP_TOKENS, 0, 0))],
        out_specs=pl.BlockSpec((tm, D_MODEL), lambda m: (m, 0)),
        out_shape=jax.ShapeDtypeStruct((N_TOK, D_MODEL), BF16),
        compiler_params=_cparams(("arbitrary",)),
        name="pre_norm",
    )(*x_pair, pack)


def _row_chunks(n_rows):
    return [pl.ds(c * UP_CHUNK, UP_CHUNK) for c in range(n_rows // UP_CHUNK)]


def _table_map(tm):
    first_sample_tile = N_PROMPT // tm
    tiles_per_seq = DEC_SEQ // tm
    return lambda m: (jnp.maximum(m - first_sample_tile, 0) % tiles_per_seq, 0)


def _rope64(y, cos, sin_signed):
    width = y.shape[-1]
    lane = lax.broadcasted_iota(I32, y.shape, 1)
    first = (lane % HEAD_DIM) < (HEAD_DIM // 2)
    swapped = jnp.where(first, pltpu.roll(y, width - HEAD_DIM // 2, 1), pltpu.roll(y, HEAD_DIM // 2, 1))
    reps = width // LANES
    c = jnp.concatenate([cos] * reps, axis=1) if reps > 1 else cos
    s = jnp.concatenate([sin_signed] * reps, axis=1) if reps > 1 else sin_signed
    return y * c + swapped * s


def _attn_proj_kernel(h_ref, w_ref, cos_ref, sin_ref, o_ref, wb_ref, *, scale, rope_cols, first_sample_tile):
    m = pl.program_id(0)

    @pl.when(m == 0)
    def _():
        wb_ref[...] = w_ref[0].astype(BF16)

    def project(rope):
        for rows in _row_chunks(h_ref.shape[0]):
            y = jnp.dot(h_ref[rows, :], wb_ref[...], preferred_element_type=F32)
            if scale != 1.0:
                y = y * scale
            if rope:
                roped = _rope64(y[:, :rope_cols], cos_ref[rows, :], sin_ref[rows, :])
                y = roped if rope_cols == y.shape[1] else jnp.concatenate([roped, y[:, rope_cols:]], axis=1)
            o_ref[rows, :] = y.astype(o_ref.dtype)

    pl.when(m >= first_sample_tile)(lambda: project(True))
    pl.when(m < first_sample_tile)(lambda: project(False))


def _attn_proj(h, w_qkv, layer, cos, sin, *, col0, ncols, rope_cols, scale, out_dtype):
    tm = TM
    kern = functools.partial(_attn_proj_kernel, scale=scale, rope_cols=rope_cols, first_sample_tile=N_PROMPT // tm)
    return pl.pallas_call(
        kern,
        grid=(N_TOK // tm,),
        in_specs=[pl.BlockSpec((tm, D_MODEL), lambda m: (m, 0)),
                  pl.BlockSpec((1, D_MODEL, ncols), lambda m: (layer, 0, col0 // ncols)),
                  pl.BlockSpec((tm, LANES), _table_map(tm)),
                  pl.BlockSpec((tm, LANES), _table_map(tm))],
        out_specs=pl.BlockSpec((tm, ncols), lambda m: (m, 0)),
        out_shape=jax.ShapeDtypeStruct((N_TOK, ncols), out_dtype),
        scratch_shapes=[pltpu.VMEM((D_MODEL, ncols), BF16)],
        compiler_params=_cparams(("arbitrary",)),
        name="attn_proj",
    )(h, w_qkv, cos, sin)


RET_TN = RET_HK


def _ret_proj_kernel(h_ref, w_ref, cos_ref, sin_ref, o_ref, wb_ref, *, first_sample_tile):
    n = pl.program_id(0)
    m = pl.program_id(1)

    @pl.when(m == 0)
    def _():
        wb_ref[...] = w_ref[0].astype(BF16)

    k_scale = jnp.where(n == 1, RET_DK ** -0.5, 1.0).astype(F32)
    half = RET_DK // 2

    def project(rope, gate=False):
        for rows in _row_chunks(h_ref.shape[0]):
            y = jnp.dot(h_ref[rows, :], wb_ref[...], preferred_element_type=F32) * k_scale
            if gate:
                y = y * jax.nn.sigmoid(y)
            if rope:
                c = cos_ref[rows, :]
                s = sin_ref[rows, :]
                parts = []
                for hh in range(RET_TN // RET_DK):
                    x1 = y[:, hh * RET_DK:hh * RET_DK + half]
                    x2 = y[:, hh * RET_DK + half:(hh + 1) * RET_DK]
                    parts.append(x1 * c - x2 * s)
                    parts.append(x1 * s + x2 * c)
                y = jnp.concatenate(parts, axis=1)
            o_ref[rows, :] = y.astype(BF16)

    gate_tile0 = (2 * RET_HK + RET_HV) // RET_TN
    do_rope = jnp.logical_and(m >= first_sample_tile, n < 2)
    is_gate = n >= gate_tile0
    pl.when(do_rope)(lambda: project(True))
    pl.when(is_gate)(lambda: project(False, gate=True))
    pl.when(jnp.logical_not(jnp.logical_or(do_rope, is_gate)))(lambda: project(False))


def _ret_proj(h, w_ret_in, layer, cos, sin):
    tm = TM
    ncols = 2 * RET_HK + 3 * RET_HV
    tmap = _table_map(tm)
    kern = functools.partial(_ret_proj_kernel, first_sample_tile=N_PROMPT // tm)
    return pl.pallas_call(
        kern,
        grid=(ncols // RET_TN, N_TOK // tm),
        in_specs=[pl.BlockSpec((tm, D_MODEL), lambda n, m: (m, 0)),
                  pl.BlockSpec((1, D_MODEL, RET_TN), lambda n, m: (layer, 0, n)),
                  pl.BlockSpec((tm, LANES), lambda n, m: tmap(m)),
                  pl.BlockSpec((tm, LANES), lambda n, m: tmap(m))],
        out_specs=pl.BlockSpec((tm, RET_TN), lambda n, m: (m, n)),
        out_shape=jax.ShapeDtypeStruct((N_TOK, ncols), BF16),
        scratch_shapes=[pltpu.VMEM((D_MODEL, RET_TN), BF16)],
        compiler_params=_cparams(("arbitrary", "arbitrary")),
        name="ret_proj",
    )(h, w_ret_in, cos, sin)


def _swiglu_kernel(h_ref, wg_ref, wu_ref, o_ref, wgb_ref, wub_ref):
    m = pl.program_id(1)

    @pl.when(m == 0)
    def _():
        wgb_ref[...] = wg_ref[0].astype(BF16)
        wub_ref[...] = wu_ref[0].astype(BF16)

    for rows in _row_chunks(h_ref.shape[0]):
        h = h_ref[rows, :]
        g = jnp.dot(h, wgb_ref[...], preferred_element_type=F32)
        u = jnp.dot(h, wub_ref[...], preferred_element_type=F32)
        o_ref[rows, :] = (g * jax.nn.sigmoid(g) * u).astype(BF16)


def _swiglu_up(h, w_gu, layer):
    tm, tf = 1024, D_FF // 2
    nf = D_FF // tf
    return pl.pallas_call(
        _swiglu_kernel,
        grid=(nf, N_TOK // tm),
        in_specs=[pl.BlockSpec((tm, D_MODEL), lambda f, m: (m, 0)),
                  pl.BlockSpec((1, D_MODEL, tf), lambda f, m: (layer, 0, f)),
                  pl.BlockSpec((1, D_MODEL, tf), lambda f, m: (layer, 0, nf + f))],
        out_specs=pl.BlockSpec((tm, tf), lambda f, m: (m, f)),
        out_shape=jax.ShapeDtypeStruct((N_TOK, D_FF), BF16),
        scratch_shapes=[pltpu.VMEM((D_MODEL, tf), BF16), pltpu.VMEM((D_MODEL, tf), BF16)],
        compiler_params=_cparams(("arbitrary", "arbitrary")),
        name="swiglu_up",
    )(h, w_gu, w_gu)


def _route(logits):
    lane = lax.broadcasted_iota(I32, logits.shape, 1).astype(F32)
    lg = jnp.where(lane < N_EXPERTS, logits, -jnp.inf)
    m1 = jnp.max(lg, axis=-1, keepdims=True)
    i1 = jnp.min(jnp.where(lg == m1, lane, float(LANES)), axis=-1, keepdims=True)
    lg2 = jnp.where(lane == i1, -jnp.inf, lg)
    m2 = jnp.max(lg2, axis=-1, keepdims=True)
    i2 = jnp.min(jnp.where(lg2 == m2, lane, float(LANES)), axis=-1, keepdims=True)
    e2 = jnp.exp(m2 - m1)
    p1 = 1.0 / (1.0 + e2)
    p2 = e2 / (1.0 + e2)
    out = jnp.where(lane == 0, i1, 0.0)
    out = jnp.where(lane == 1, i2, out)
    out = jnp.where(lane == 2, p1, out)
    out = jnp.where(lane == 3, p2, out)
    return out


def _epilogue(x_new, pk_next, rows, fin_ref, final):
    if final:
        ms = jnp.mean(x_new * x_new, axis=-1, keepdims=True)
        return x_new * lax.rsqrt(ms + EPS) * fin_ref[...]
    g_row, sc_row, sh_row = rows
    return _normmod(x_new, pk_next[g_row:g_row + 1], pk_next[sc_row:sc_row + 1], pk_next[sh_row:sh_row + 1])


def _down_kernel(*refs, n_a, n_x, first_sample_tile, gate_row, next_rows, router):
    a_refs = refs[:n_a]
    w_ref = refs[n_a]
    x_refs = refs[n_a + 1:n_a + 1 + n_x]
    pk_ref, pkn_ref = refs[n_a + 1 + n_x:n_a + 3 + n_x]
    pos = n_a + 3 + n_x
    if router:
        wrh_ref, wrl_ref = refs[pos:pos + 2]
        pos += 2
        xo_ref, h_ref, hf_ref, rt_ref, wb_ref = refs[pos:pos + 5]
    else:
        xo_ref, h_ref, wb_ref = refs[pos:pos + 3]
    m = pl.program_id(0)

    @pl.when(m == 0)
    def _():
        wb_ref[...] = w_ref[0].astype(BF16)

    def finish(a_ref, x_ref):
        pk = pk_ref[0, 0]
        pkn = pkn_ref[0, 0]
        for c in range(a_ref.shape[0] // DOWN_CHUNK):
            rows = pl.ds(c * DOWN_CHUNK, DOWN_CHUNK)
            y = jnp.dot(a_ref[rows, :], wb_ref[...], preferred_element_type=F32)
            x_new = x_ref[rows, :] + pk[gate_row:gate_row + 1] * y
            xo_ref[rows, :] = x_new
            hn = _epilogue(x_new, pkn, next_rows, None, False)
            hi = hn.astype(BF16)
            h_ref[rows, :] = hi
            if router:
                hf_ref[rows, :] = hn
        if router:
            hi = h_ref[...]
            lo = (hf_ref[...] - hi.astype(F32)).astype(BF16)
            logits = (jnp.dot(hi, wrh_ref[0], preferred_element_type=F32)
                      + jnp.dot(lo, wrh_ref[0], preferred_element_type=F32)
                      + jnp.dot(hi, wrl_ref[0], preferred_element_type=F32))
            rt_ref[...] = _route(logits)

    if n_a == 1 and n_x == 1:
        finish(a_refs[0], x_refs[0])
    else:
        pl.when(m < first_sample_tile)(lambda: finish(a_refs[0], x_refs[0]))
        pl.when(m >= first_sample_tile)(lambda: finish(a_refs[-1], x_refs[-1]))


def _down(a_list, w, layer, x, pack, pack_layer, gate_row, next_layer, next_rows, w_router=None, router_layer=0):
    tm = TM_DOWN
    kd = w.shape[1]
    n_a = len(a_list)
    fst = N_PROMPT // tm
    router = w_router is not None
    x_list = list(x) if isinstance(x, (list, tuple)) else [x]
    n_x = len(x_list)
    a_specs = [pl.BlockSpec((tm, kd), lambda m: (m, 0))] if n_a == 1 else _split_specs(tm, kd)
    x_specs = [pl.BlockSpec((tm, D_MODEL), lambda m: (m, 0))] if n_x == 1 else _split_specs(tm, D_MODEL)
    in_specs = a_specs + [pl.BlockSpec((1, kd, D_MODEL), lambda m: (layer, 0, 0))] + x_specs + [
        pl.BlockSpec((1, 1, 8, D_MODEL), lambda m: (pack_layer, (m * tm) // GROUP_TOKENS, 0, 0)),
        pl.BlockSpec((1, 1, 8, D_MODEL), lambda m: (next_layer, (m * tm) // GROUP_TOKENS, 0, 0)),
    ]
    args = list(a_list) + [w] + list(x_list) + [pack, pack]
    out_specs = [pl.BlockSpec((tm, D_MODEL), lambda m: (m, 0)),
                 pl.BlockSpec((tm, D_MODEL), lambda m: (m, 0))]
    out_shape = [jax.ShapeDtypeStruct((N_TOK, D_MODEL), F32),
                 jax.ShapeDtypeStruct((N_TOK, D_MODEL), BF16)]
    if router:
        in_specs += [pl.BlockSpec((1, D_MODEL, LANES), lambda m: (router_layer, 0, 0)),
                     pl.BlockSpec((1, D_MODEL, LANES), lambda m: (router_layer, 0, 0))]
        args += list(w_router)
        out_specs += [pl.BlockSpec((tm, D_MODEL), lambda m: (m, 0)),
                      pl.BlockSpec((tm, LANES), lambda m: (m, 0))]
        out_shape += [jax.ShapeDtypeStruct((N_TOK, D_MODEL), F32),
                      jax.ShapeDtypeStruct((N_TOK, LANES), F32)]
    kern = functools.partial(_down_kernel, n_a=n_a, n_x=n_x, first_sample_tile=fst, gate_row=gate_row,
                             next_rows=next_rows, router=router)
    return pl.pallas_call(
        kern,
        grid=(N_TOK // tm,),
        in_specs=in_specs,
        out_specs=out_specs,
        out_shape=out_shape,
        scratch_shapes=[pltpu.VMEM((kd, D_MODEL), BF16)],
        compiler_params=_cparams(("arbitrary",)),
        name="down_proj",
    )(*args)


def _sink_row(sink_ref, layer, kvh, width):
    return jnp.concatenate(
        [jnp.full((1, width), sink_ref[layer * N_HEADS + kvh * GROUP + g] * LOG2E, F32) for g in range(GROUP)],
        axis=1)


def _group_queries(q_ref, kvh):
    return jnp.concatenate(
        [q_ref[:, (kvh * GROUP + g) * HEAD_DIM:(kvh * GROUP + g + 1) * HEAD_DIM] for g in range(GROUP)], axis=0)


_NT = (((1,), (1,)), ((), ()))


def _ctx_attn_kernel(sink_ref, q_ref, kv_ref, o_ref, *, layer):
    kvw = N_KV_HEADS * HEAD_DIM
    k_all = kv_ref[:, :kvw].astype(BF16)
    v_t = kv_ref[:, kvw:].T.astype(BF16)
    outs = []
    for kvh in range(N_KV_HEADS):
        hs = slice(kvh * HEAD_DIM, (kvh + 1) * HEAD_DIM)
        s_t = lax.dot_general(k_all[:, hs], _group_queries(q_ref, kvh), _NT, preferred_element_type=F32)
        sink = _sink_row(sink_ref, layer, kvh, SEQ)
        m = jnp.maximum(jnp.max(s_t, axis=0, keepdims=True), sink)
        p = jnp.exp2(s_t - m)
        denom = jnp.sum(p, axis=0, keepdims=True) + jnp.exp2(sink - m)
        o_t = jnp.dot(v_t[hs], p.astype(BF16), preferred_element_type=F32) / denom
        outs += [o_t[:, g * SEQ:(g + 1) * SEQ] for g in range(GROUP)]
    o_ref[...] = jnp.concatenate(outs, axis=0).T.astype(BF16)


def _ctx_attention(q, kv, sink_flat, layer):
    kern = functools.partial(_ctx_attn_kernel, layer=layer)
    return pl.pallas_call(
        kern,
        grid_spec=pltpu.PrefetchScalarGridSpec(
            num_scalar_prefetch=1,
            grid=(BATCH,),
            in_specs=[pl.BlockSpec((SEQ, D_MODEL), lambda b, s: (b, 0)),
                      pl.BlockSpec((SEQ, 2 * N_KV_HEADS * HEAD_DIM), lambda b, s: (b, 0))],
            out_specs=pl.BlockSpec((SEQ, D_MODEL), lambda b, s: (b, 0))),
        out_shape=jax.ShapeDtypeStruct((N_PROMPT, D_MODEL), BF16),
        compiler_params=_cparams(("arbitrary",)),
        name="ctx_attention",
    )(sink_flat, q, kv)


def _lat_attn_kernel(sink_ref, q_ref, kvp_ref, kvc_ref, kvn_ref, ck_ref, cv_ref, o_ref, kc_ref, vct_ref, *, layer):
    blk = pl.program_id(1)
    kvw = N_KV_HEADS * HEAD_DIM
    cols = GROUP * BLOCK
    n_lat = 3 * BLOCK

    @pl.when(blk == 0)
    def _():
        kc_ref[...] = ck_ref[0, 0].astype(BF16)
        vct_ref[...] = cv_ref[0, 0].T.astype(BF16)

    j = lax.broadcasted_iota(I32, (n_lat, cols), 0)
    r = lax.broadcasted_iota(I32, (n_lat, cols), 1) % BLOCK
    lo = jnp.maximum(r, BLOCK - BLOCK * blk)
    hi = jnp.minimum(r + 2 * WINDOW, DEC_SEQ + BLOCK - 1 - BLOCK * blk)
    lat_cap = jnp.where(jnp.logical_and(j >= lo, j <= hi), jnp.inf, NEG).astype(F32)
    k_lat = jnp.concatenate([kvp_ref[:, :kvw], kvc_ref[:, :kvw], kvn_ref[:, :kvw]], axis=0).astype(BF16)
    v_lat_t = jnp.concatenate([kvp_ref[:, kvw:].T, kvc_ref[:, kvw:].T, kvn_ref[:, kvw:].T], axis=1).astype(BF16)
    outs = []
    for kvh in range(N_KV_HEADS):
        hs = slice(kvh * HEAD_DIM, (kvh + 1) * HEAD_DIM)
        q4 = _group_queries(q_ref, kvh)
        s_ctx = lax.dot_general(kc_ref[:, hs], q4, _NT, preferred_element_type=F32)
        s_lat = lax.dot_general(k_lat[:, hs], q4, _NT, preferred_element_type=F32)
        s_lat = jnp.minimum(s_lat, lat_cap)
        sink = _sink_row(sink_ref, layer, kvh, BLOCK)
        m = jnp.maximum(jnp.maximum(jnp.max(s_ctx, axis=0, keepdims=True),
                                    jnp.max(s_lat, axis=0, keepdims=True)), sink)
        p_ctx = jnp.exp2(s_ctx - m)
        p_lat = jnp.exp2(s_lat - m)
        denom = (jnp.sum(p_ctx, axis=0, keepdims=True) + jnp.sum(p_lat, axis=0, keepdims=True)
                 + jnp.exp2(sink - m))
        o_t = (jnp.dot(vct_ref[hs, :], p_ctx.astype(BF16), preferred_element_type=F32)
               + jnp.dot(v_lat_t[hs], p_lat.astype(BF16), preferred_element_type=F32)) / denom
        outs += [o_t[:, g * BLOCK:(g + 1) * BLOCK] for g in range(GROUP)]
    o_ref[...] = jnp.concatenate(outs, axis=0).T.astype(BF16)


def _lat_attention(q, kv, cache_k4, cache_v4, sink_flat, layer):
    nb = DEC_SEQ // BLOCK
    base = N_PROMPT // BLOCK
    kvc = 2 * N_KV_HEADS * HEAD_DIM

    def row(b, i):
        return base + b * nb + i

    kern = functools.partial(_lat_attn_kernel, layer=layer)
    return pl.pallas_call(
        kern,
        grid_spec=pltpu.PrefetchScalarGridSpec(
            num_scalar_prefetch=1,
            grid=(DEC_BATCH, nb),
            in_specs=[pl.BlockSpec((BLOCK, D_MODEL), lambda b, i, s: (row(b, i), 0)),
                      pl.BlockSpec((BLOCK, kvc), lambda b, i, s: (row(b, jnp.maximum(i - 1, 0)), 0)),
                      pl.BlockSpec((BLOCK, kvc), lambda b, i, s: (row(b, i), 0)),
                      pl.BlockSpec((BLOCK, kvc), lambda b, i, s: (row(b, jnp.minimum(i + 1, nb - 1)), 0)),
                      pl.BlockSpec((1, 1, PAST_LEN, N_KV_HEADS * HEAD_DIM), lambda b, i, s: (b, layer, 0, 0)),
                      pl.BlockSpec((1, 1, PAST_LEN, N_KV_HEADS * HEAD_DIM), lambda b, i, s: (b, layer, 0, 0))],
            out_specs=pl.BlockSpec((BLOCK, D_MODEL), lambda b, i, s: (b * nb + i, 0)),
            scratch_shapes=[pltpu.VMEM((PAST_LEN, N_KV_HEADS * HEAD_DIM), BF16),
                            pltpu.VMEM((N_KV_HEADS * HEAD_DIM, PAST_LEN), BF16)]),
        out_shape=jax.ShapeDtypeStruct((N_SAMPLE, D_MODEL), BF16),
        compiler_params=_cparams(("arbitrary", "arbitrary")),
        name="lat_attention",
    )(sink_flat, q, kv, kv, kv, cache_k4, cache_v4)


RET_SUB = 2
RET_ROWS = RET_SUB * BLOCK
RET_STEPS = N_TOK // RET_ROWS
PROMPT_STEPS = N_PROMPT // RET_ROWS
STEPS_PER_SAMPLE = DEC_SEQ // RET_ROWS
assert SEQ == RET_ROWS


def _ret_kernel(lg_ref, q_ref, k_ref, v_ref, gate_ref, s0_ref, *rest, backward, layer):
    part_ref = rest[0] if backward else None
    o_ref, sout_ref, state_ref = rest[-3:]
    step = pl.program_id(0)
    blk = (RET_STEPS - 1 - step) if backward else step
    in_prompt = blk < PROMPT_STEPS
    first_of_sample = (blk - PROMPT_STEPS) % STEPS_PER_SAMPLE == (STEPS_PER_SAMPLE - 1 if backward else 0)

    @pl.when(in_prompt)
    def _():
        state_ref[...] = jnp.zeros_like(state_ref)

    @pl.when(jnp.logical_and(jnp.logical_not(in_prompt), first_of_sample))
    def _():
        state_ref[...] = s0_ref[0, 0]

    ii = lax.broadcasted_iota(I32, (BLOCK, BLOCK), 0).astype(F32)
    jj = lax.broadcasted_iota(I32, (BLOCK, BLOCK), 1).astype(F32)
    dist = (jj - ii) if backward else (ii - jj)
    ti = lax.broadcasted_iota(I32, (BLOCK, 1), 0).astype(F32)
    q_pow = (BLOCK - ti) if backward else (ti + 1.0)
    k_pow = ti if backward else (BLOCK - 1.0 - ti)

    for h in range(RET_HEADS):
        lg = lg_ref[(layer * 2 + (1 if backward else 0)) * RET_HEADS + h]
        intra = jnp.where(dist >= 0, jnp.exp(lg * jnp.maximum(dist, 0.0)), 0.0)
        q_dec = jnp.exp(lg * q_pow)
        k_dec = jnp.exp(lg * k_pow)
        c_dec = jnp.exp(lg * BLOCK)
        for sub in (reversed(range(RET_SUB)) if backward else range(RET_SUB)):
            rows = pl.ds(sub * BLOCK, BLOCK)
            q = q_ref[rows, h * RET_DK:(h + 1) * RET_DK]
            k = k_ref[rows, h * RET_DK:(h + 1) * RET_DK]
            v = v_ref[rows, h * RET_DV:(h + 1) * RET_DV]
            s = state_ref[h]
            a = lax.dot_general(q, k, (((1,), (1,)), ((), ())), preferred_element_type=F32) * intra
            o = (jnp.dot(a.astype(BF16), v, preferred_element_type=F32)
                 + q_dec * jnp.dot(q, s.astype(BF16), preferred_element_type=F32))
            kd = (k.astype(F32) * k_dec).astype(BF16)
            state_ref[h] = c_dec * s + lax.dot_general(kd, v, (((0,), (0,)), ((), ())), preferred_element_type=F32)
            gate = gate_ref[rows, h * RET_DV:(h + 1) * RET_DV].astype(F32)
            on = o * lax.rsqrt(jnp.mean(o * o, axis=-1, keepdims=True) + EPS)
            res = on * gate
            if backward:
                res = res + part_ref[rows, h * RET_DV:(h + 1) * RET_DV].astype(F32)
            o_ref[rows, h * RET_DV:(h + 1) * RET_DV] = res.astype(BF16)

    @pl.when(in_prompt)
    def _():
        if sout_ref.shape[1] == 1:
            sout_ref[0, 0] = state_ref[...]
        else:
            for l in range(sout_ref.shape[1]):
                sout_ref[0, l] = state_ref[...] if l == layer else jnp.zeros_like(state_ref)


def _retention_pass(proj, s0, log_gamma_flat, layer, backward, partial=None, states=None):
    def blk(i):
        return (RET_STEPS - 1 - i) if backward else i

    def s0_map(i, lg):
        return (jnp.clip((blk(i) - PROMPT_STEPS) // STEPS_PER_SAMPLE, 0, DEC_BATCH - 1), layer, 0, 0, 0)

    sout_layers = N_ODD if states is None else 1

    def sout_map(i, lg):
        return (jnp.minimum(blk(i), BATCH - 1), 0 if states is None else layer, 0, 0, 0)

    gate_block = 3 if backward else 2
    in_specs = [pl.BlockSpec((RET_ROWS, RET_HK), lambda i, lg: (blk(i), 0)),
                pl.BlockSpec((RET_ROWS, RET_HK), lambda i, lg: (blk(i), 1)),
                pl.BlockSpec((RET_ROWS, RET_HV), lambda i, lg: (blk(i), 1)),
                pl.BlockSpec((RET_ROWS, RET_HV), lambda i, lg: (blk(i), gate_block)),
                pl.BlockSpec((1, 1, RET_HEADS, RET_DK, RET_DV), s0_map)]
    args = [proj, proj, proj, proj, s0]
    if backward:
        in_specs.append(pl.BlockSpec((RET_ROWS, RET_HV), lambda i, lg: (blk(i), 0)))
        args.append(partial)
    aliases = {}
    if states is not None:
        in_specs.append(pl.BlockSpec(memory_space=pl.ANY))
        args.append(states)
        aliases = {len(args): 1}
    kern = functools.partial(_ret_kernel, backward=backward, layer=layer)
    return pl.pallas_call(
        kern,
        grid_spec=pltpu.PrefetchScalarGridSpec(
            num_scalar_prefetch=1,
            grid=(RET_STEPS,),
            in_specs=in_specs,
            out_specs=[pl.BlockSpec((RET_ROWS, RET_HV), lambda i, lg: (blk(i), 0)),
                       pl.BlockSpec((1, sout_layers, RET_HEADS, RET_DK, RET_DV), sout_map)],
            scratch_shapes=[pltpu.VMEM((RET_HEADS, RET_DK, RET_DV), F32)]),
        out_shape=[jax.ShapeDtypeStruct((N_TOK, RET_HV), BF16),
                   jax.ShapeDtypeStruct((BATCH, N_ODD, RET_HEADS, RET_DK, RET_DV), F32)],
        input_output_aliases=aliases,
        compiler_params=_cparams(("arbitrary",)),
        name="retention_bwd" if backward else "retention_fwd",
    )(log_gamma_flat, *args)


def _dispatch_kernel(slot_ref, pend_ref, h_ref, o_hbm, zbuf, sem):
    i = pl.program_id(0)

    @pl.when(i == 0)
    def _():
        zbuf[...] = jnp.zeros_like(zbuf)

        def tile_fill(start):
            return pltpu.make_async_copy(zbuf, o_hbm.at[pl.ds(pl.multiple_of(start, MOE_TILE), MOE_TILE)], sem)

        def nonempty(e):
            return pend_ref[e] > (pend_ref[e - 1] if e else 0)

        def start_unused(t, carry):
            tile_fill(t * MOE_TILE).start()
            return carry

        def wait_unused(t, carry):
            tile_fill(t * MOE_TILE).wait()
            return carry

        first_unused = pend_ref[N_EXPERTS - 1] // MOE_TILE
        for e in range(N_EXPERTS):
            pl.when(nonempty(e))(lambda e=e: tile_fill(pend_ref[e] - MOE_TILE).start())
        lax.fori_loop(first_unused, MOE_NT, start_unused, 0)
        for e in range(N_EXPERTS):
            pl.when(nonempty(e))(lambda e=e: tile_fill(pend_ref[e] - MOE_TILE).wait())
        lax.fori_loop(first_unused, MOE_NT, wait_unused, 0)

    base = i * DISPATCH_ROWS * TOP_K

    def start(r, carry):
        for k in range(TOP_K):
            pltpu.make_async_copy(h_ref.at[pl.ds(r, 1)], o_hbm.at[pl.ds(slot_ref[base + r * TOP_K + k], 1)],
                                  sem).start()
        return carry

    lax.fori_loop(0, DISPATCH_ROWS, start, 0, unroll=8)
    for k in range(TOP_K):
        pltpu.make_async_copy(h_ref, o_hbm.at[pl.ds(0, DISPATCH_ROWS)], sem).wait()


def _dispatch_rows(slot, pend, h):
    return pl.pallas_call(
        _dispatch_kernel,
        grid_spec=pltpu.PrefetchScalarGridSpec(
            num_scalar_prefetch=2,
            grid=(N_TOK // DISPATCH_ROWS,),
            in_specs=[pl.BlockSpec((DISPATCH_ROWS, D_MODEL), lambda i, s, p: (i, 0))],
            out_specs=pl.BlockSpec(memory_space=pl.ANY),
            scratch_shapes=[pltpu.VMEM((MOE_TILE, D_MODEL), F32), pltpu.SemaphoreType.DMA(())]),
        out_shape=jax.ShapeDtypeStruct((MOE_SLOTS, D_MODEL), F32),
        compiler_params=_cparams(("arbitrary",)),
        name="moe_dispatch",
    )(slot, pend, h)


def _for_occupied_chunks(n_rows, o_ref, chunk):
    n_chunks = MOE_TILE // MOE_CHUNK
    occupied = (n_rows + MOE_CHUNK - 1) // MOE_CHUNK

    def path(k):
        for c in range(n_chunks):
            rows = pl.ds(c * MOE_CHUNK, MOE_CHUNK)
            if c < k:
                chunk(rows)
            else:
                o_ref[rows, :] = jnp.zeros((MOE_CHUNK, o_ref.shape[1]), o_ref.dtype)

    for k in range(n_chunks + 1):
        pl.when(occupied == k)(functools.partial(path, k))


def _moe_up_kernel(te_ref, nv_ref, tr_ref, h_ref, wg_ref, wu_ref, o_ref, wgb_ref, wub_ref):
    m = pl.program_id(1)
    new_w = jnp.logical_or(m == 0, te_ref[m] != te_ref[jnp.maximum(m - 1, 0)])

    @pl.when(new_w)
    def _():
        wgb_ref[...] = wg_ref[0, 0].astype(BF16)
        wub_ref[...] = wu_ref[0, 0].astype(BF16)

    def chunk(rows):
        h = h_ref[rows, :].astype(BF16)
        g = jnp.dot(h, wgb_ref[...], preferred_element_type=F32)
        u = jnp.dot(h, wub_ref[...], preferred_element_type=F32)
        o_ref[rows, :] = (g * jax.nn.sigmoid(g) * u).astype(BF16)

    _for_occupied_chunks(tr_ref[m], o_ref, chunk)


def _moe_up(tile_expert, n_valid, tile_rows, hs, w_gu, layer):
    tf = 1792
    nf = D_FF_EXPERT // tf
    return pl.pallas_call(
        _moe_up_kernel,
        grid_spec=pltpu.PrefetchScalarGridSpec(
            num_scalar_prefetch=3,
            grid=(nf, MOE_NT),
            in_specs=[pl.BlockSpec((MOE_TILE, D_MODEL), lambda f, m, te, nv, tr: (jnp.minimum(m, nv[0] - 1), 0)),
                      pl.BlockSpec((1, 1, D_MODEL, tf), lambda f, m, te, nv, tr: (layer, te[m], 0, f)),
                      pl.BlockSpec((1, 1, D_MODEL, tf), lambda f, m, te, nv, tr: (layer, te[m], 0, nf + f))],
            out_specs=pl.BlockSpec((MOE_TILE, tf), lambda f, m, te, nv, tr: (m, f)),
            scratch_shapes=[pltpu.VMEM((D_MODEL, tf), BF16), pltpu.VMEM((D_MODEL, tf), BF16)]),
        out_shape=jax.ShapeDtypeStruct((MOE_SLOTS, D_FF_EXPERT), BF16),
        compiler_params=_cparams(("arbitrary", "arbitrary")),
        name="moe_up",
    )(tile_expert, n_valid, tile_rows, hs, w_gu, w_gu)


def _moe_down_kernel(te_ref, nv_ref, tr_ref, a_ref, w_ref, o_ref, wb_ref):
    m = pl.program_id(1)
    new_w = jnp.logical_or(m == 0, te_ref[m] != te_ref[jnp.maximum(m - 1, 0)])

    @pl.when(new_w)
    def _():
        wb_ref[...] = w_ref[0, 0].astype(BF16)

    def chunk(rows):
        o_ref[rows, :] = jnp.dot(a_ref[rows, :], wb_ref[...], preferred_element_type=F32)

    _for_occupied_chunks(tr_ref[m], o_ref, chunk)


def _moe_down(tile_expert, n_valid, tile_rows, act, w_down, layer):
    tn = D_MODEL
    return pl.pallas_call(
        _moe_down_kernel,
        grid_spec=pltpu.PrefetchScalarGridSpec(
            num_scalar_prefetch=3,
            grid=(D_MODEL // tn, MOE_NT),
            in_specs=[pl.BlockSpec((MOE_TILE, D_FF_EXPERT), lambda n, m, te, nv, tr: (jnp.minimum(m, nv[0] - 1), 0)),
                      pl.BlockSpec((1, 1, D_FF_EXPERT, tn), lambda n, m, te, nv, tr: (layer, te[m], 0, n))],
            out_specs=pl.BlockSpec((MOE_TILE, tn), lambda n, m, te, nv, tr: (m, n)),
            scratch_shapes=[pltpu.VMEM((D_FF_EXPERT, tn), BF16)]),
        out_shape=jax.ShapeDtypeStruct((MOE_SLOTS, D_MODEL), F32),
        compiler_params=_cparams(("arbitrary", "arbitrary")),
        name="moe_down",
    )(tile_expert, n_valid, tile_rows, act, w_down)


def _combine_kernel(slot_ref, y_hbm, x_ref, rt_ref, pk_ref, pkn_ref, fin_ref, *rest, tok0, final):
    if final:
        out_ref, ybuf, sems = rest
    else:
        xo_ref, h_ref, ybuf, sems = rest
    tc = TC_COMBINE
    i = pl.program_id(0)
    n = pl.num_programs(0)
    cur = i % 2
    nxt = 1 - cur

    def issue_rows(tile, buf, r0):
        base = (tok0 + tile * tc) * TOP_K
        for r in range(COMBINE_GROUP):
            for k in range(TOP_K):
                pltpu.make_async_copy(y_hbm.at[pl.ds(slot_ref[base + (r0 + r) * TOP_K + k], 1)],
                                      ybuf.at[buf, k, pl.ds(r0 + r, 1)], sems.at[buf]).start()

    def wait_tile(buf):
        for k in range(TOP_K):
            pltpu.make_async_copy(y_hbm.at[pl.ds(0, tc)], ybuf.at[buf, k], sems.at[buf]).wait()

    def group_rows(g):
        return pl.multiple_of(g * COMBINE_GROUP, COMBINE_GROUP)

    @pl.when(i == 0)
    def _():
        def first(g, carry):
            issue_rows(0, 0, group_rows(g))
            return carry

        lax.fori_loop(0, tc // COMBINE_GROUP, first, 0)

    wait_tile(cur)
    nxt_tile = jnp.minimum(i + 1, n - 1)
    pk = pk_ref[0, 0]
    pkn = pkn_ref[0, 0]

    def group(g, carry):
        r0 = group_rows(g)
        issue_rows(nxt_tile, nxt, r0)
        rows = pl.ds(r0, COMBINE_GROUP)
        rt = rt_ref[rows, :]
        moe = rt[:, 2:3] * ybuf[cur, 0, rows, :] + rt[:, 3:4] * ybuf[cur, 1, rows, :]
        x_new = x_ref[rows, :] + pk[R_GATE_FFN:R_GATE_FFN + 1] * moe
        if final:
            out_ref[rows, :] = _epilogue(x_new, None, None, fin_ref, True)
        else:
            xo_ref[rows, :] = x_new
            h_ref[rows, :] = _epilogue(x_new, pkn, (R_G_MIX, R_SCALE_MIX, R_SHIFT_MIX), None, False).astype(BF16)
        return carry

    lax.fori_loop(0, tc // COMBINE_GROUP, group, 0, unroll=True)

    @pl.when(i == n - 1)
    def _():
        wait_tile(nxt)


def _combine(slot, y, x, route, pack, layer, next_layer, final_norm, tok0, n_rows, final):
    tc = TC_COMBINE
    b0 = tok0 // tc

    def tmap(m, s):
        return (b0 + m, 0)

    def pmap(l):
        return lambda m, s: (l, (tok0 + m * tc) // GROUP_TOKENS, 0, 0)

    if final:
        out_specs = pl.BlockSpec((tc, D_MODEL), lambda m, s: (m, 0))
        out_shape = jax.ShapeDtypeStruct((n_rows, D_MODEL), F32)
    else:
        out_specs = [pl.BlockSpec((tc, D_MODEL), lambda m, s: (m, 0)),
                     pl.BlockSpec((tc, D_MODEL), lambda m, s: (m, 0))]
        out_shape = [jax.ShapeDtypeStruct((n_rows, D_MODEL), F32),
                     jax.ShapeDtypeStruct((n_rows, D_MODEL), BF16)]
    kern = functools.partial(_combine_kernel, tok0=tok0, final=final)
    return pl.pallas_call(
        kern,
        grid_spec=pltpu.PrefetchScalarGridSpec(
            num_scalar_prefetch=1,
            grid=(n_rows // tc,),
            in_specs=[pl.BlockSpec(memory_space=pl.ANY),
                      pl.BlockSpec((tc, D_MODEL), tmap),
                      pl.BlockSpec((tc, LANES), tmap),
                      pl.BlockSpec((1, 1, 8, D_MODEL), pmap(layer)),
                      pl.BlockSpec((1, 1, 8, D_MODEL), pmap(next_layer)),
                      pl.BlockSpec((1, D_MODEL), lambda m, s: (0, 0))],
            out_specs=out_specs,
            scratch_shapes=[pltpu.VMEM((2, TOP_K, tc, D_MODEL), F32), pltpu.SemaphoreType.DMA((2,))]),
        out_shape=out_shape,
        compiler_params=_cparams(("arbitrary",)),
        name="moe_combine",
    )(slot, y, x, route, pack, pack, final_norm.reshape(1, D_MODEL))


def _routing_tables(route):
    e_flat = route[:, :TOP_K].astype(I32).reshape(-1)
    onehot = (e_flat[:, None] == jnp.arange(N_EXPERTS, dtype=I32)[None, :]).astype(I32)
    csum = jnp.cumsum(onehot, axis=0)
    rank = jnp.sum(onehot * (csum - 1), axis=1)
    count = csum[-1]
    padded = ((count + MOE_TILE - 1) // MOE_TILE) * MOE_TILE
    pend = jnp.cumsum(padded)
    poff = pend - padded
    slot = jnp.sum(onehot * poff[None, :], axis=1) + rank
    n_valid = pend[-1] // MOE_TILE
    tile_start = jnp.arange(MOE_NT, dtype=I32) * MOE_TILE
    te_raw = jnp.minimum(jnp.sum((tile_start[:, None] >= pend[None, :]).astype(I32), axis=1), N_EXPERTS - 1)
    last_e = jnp.max(jnp.where(tile_start < pend[-1], te_raw, 0))
    tile_expert = jnp.minimum(te_raw, last_e)
    group_rows_end = jnp.sum((te_raw[:, None] == jnp.arange(N_EXPERTS, dtype=I32)[None, :]) * (poff + count)[None, :],
                             axis=1)
    tile_rows = jnp.where(tile_start < pend[-1], jnp.clip(group_rows_end - tile_start, 0, MOE_TILE), 0)
    return (slot.astype(I32), pend.astype(I32), tile_expert.astype(I32), n_valid.astype(I32).reshape(1),
            tile_rows.astype(I32))


def _rope_tables(dh):
    nf = dh // 4
    t = jnp.arange(DEC_SEQ)
    freqs = ROPE_BASE ** (-jnp.arange(nf, dtype=F32) / nf)
    row = (t // GRID_W).astype(F32)
    col = (t % GRID_W).astype(F32)
    ang = jnp.concatenate([row[:, None] * freqs, col[:, None] * freqs], axis=-1)
    return jnp.cos(ang), jnp.sin(ang)


def kernel(x_prompt, x_sample, cache_k, cache_v, state_fwd, state_bwd, c, c_ctx, norm_mix, norm_ffn, w_mod, b_mod,
           w_qkv, w_attn_o, attn_sink, w_ret_in, ret_decay, w_ret_out, w_ffn_gu, w_ffn_down, w_router, w_exp_gu,
           w_exp_down, final_norm):
    x = (x_prompt.reshape(N_PROMPT, D_MODEL), x_sample.reshape(N_SAMPLE, D_MODEL))

    cond8 = jnp.concatenate([c_ctx[None, :], c, jnp.zeros((8 - 1 - DEC_BATCH, D_MODEL), F32)], axis=0)
    mods = _modulations(cond8, w_mod, b_mod)
    m3 = mods[:, :N_GROUPS].reshape(DEPTH, N_GROUPS, 6, D_MODEL)
    pack = jnp.concatenate([
        m3,
        jnp.broadcast_to(norm_mix[:, None, None, :], (DEPTH, N_GROUPS, 1, D_MODEL)),
        jnp.broadcast_to(norm_ffn[:, None, None, :], (DEPTH, N_GROUPS, 1, D_MODEL))], axis=2)

    cos_a, sin_a = _rope_tables(HEAD_DIM)
    cos_attn = jnp.tile(cos_a, (1, LANES // (HEAD_DIM // 2)))
    sin_attn = jnp.tile(jnp.concatenate([-sin_a, sin_a], axis=1), (1, LANES // HEAD_DIM))
    cos_ret, sin_ret = _rope_tables(RET_DK)

    kvw = N_KV_HEADS * HEAD_DIM
    cache_k4 = cache_k.reshape(DEC_BATCH, N_EVEN, PAST_LEN, kvw)
    cache_v4 = cache_v.reshape(DEC_BATCH, N_EVEN, PAST_LEN, kvw)
    sink_flat = attn_sink.reshape(-1)
    log_gamma = jax.nn.log_sigmoid(ret_decay.astype(F32)).reshape(-1)
    w_router_pad = jnp.pad(w_router, ((0, 0), (0, 0), (0, LANES - N_EXPERTS)))
    w_router_hi = w_router_pad.astype(BF16)
    w_router_lo = (w_router_pad - w_router_hi.astype(F32)).astype(BF16)

    mix_rows = (R_G_MIX, R_SCALE_MIX, R_SHIFT_MIX)
    ffn_rows = (R_G_FFN, R_SCALE_FFN, R_SHIFT_FFN)

    new_k, new_v = [], []
    new_sf = new_sb = None
    h = _pre(x, pack)
    y_prompt = y_sample = None
    for i in range(DEPTH):
        j = i // 2
        if i % 2 == 0:
            q = _attn_proj(h, w_qkv, j, cos_attn, sin_attn, col0=0, ncols=N_HEADS * HEAD_DIM,
                           rope_cols=N_HEADS * HEAD_DIM, scale=HEAD_DIM ** -0.5 * LOG2E, out_dtype=BF16)
            kv = _attn_proj(h, w_qkv, j, cos_attn, sin_attn, col0=N_HEADS * HEAD_DIM, ncols=2 * kvw,
                            rope_cols=kvw, scale=1.0, out_dtype=F32)
            new_k.append(kv[:N_PROMPT, :kvw].reshape(BATCH, SEQ, N_KV_HEADS, HEAD_DIM))
            new_v.append(kv[:N_PROMPT, kvw:].reshape(BATCH, SEQ, N_KV_HEADS, HEAD_DIM))
            o_ctx = _ctx_attention(q, kv, sink_flat, j)
            o_lat = _lat_attention(q, kv, cache_k4, cache_v4, sink_flat, j)
            x, h = _down([o_ctx, o_lat], w_attn_o, j, x, pack, i, R_GATE_MIX, i, ffn_rows)
            act = _swiglu_up(h, w_ffn_gu, j)
            x, h = _down([act], w_ffn_down, j, x, pack, i, R_GATE_FFN, i + 1, mix_rows)
        else:
            proj = _ret_proj(h, w_ret_in, j, cos_ret, sin_ret)
            part, new_sf = _retention_pass(proj, state_fwd, log_gamma, j, False, states=new_sf)
            o_ret, new_sb = _retention_pass(proj, state_bwd, log_gamma, j, True, partial=part, states=new_sb)
            x, h, hf, route = _down([o_ret], w_ret_out, j, x, pack, i, R_GATE_MIX, i, ffn_rows,
                                    w_router=(w_router_hi, w_router_lo), router_layer=j)
            slot, pend, tile_expert, n_valid, tile_rows = _routing_tables(route)
            hs = _dispatch_rows(slot, pend, hf)
            act = _moe_up(tile_expert, n_valid, tile_rows, hs, w_exp_gu, j)
            ys = _moe_down(tile_expert, n_valid, tile_rows, act, w_exp_down, j)
            if i == DEPTH - 1:
                y_prompt = _combine(slot, ys, x, route, pack, i, i, final_norm, 0, N_PROMPT, True)
                y_sample = _combine(slot, ys, x, route, pack, i, i, final_norm, N_PROMPT, N_SAMPLE, True)
            else:
                x, h = _combine(slot, ys, x, route, pack, i, i + 1, final_norm, 0, N_TOK, False)

    return (y_prompt.reshape(BATCH, SEQ, D_MODEL), y_sample.reshape(DEC_BATCH, DEC_SEQ, D_MODEL),
            jnp.stack(new_k, axis=1), jnp.stack(new_v, axis=1), new_sf, new_sb)
```

```python
import functools

import jax
import jax.numpy as jnp
from jax import lax
from jax.experimental import pallas as pl
from jax.experimental.pallas import tpu as pltpu

F32 = jnp.float32
BF16 = jnp.bfloat16
I32 = jnp.int32

D_MODEL = 1024
BATCH = 16
SEQ = 256
DEPTH = 4
DEC_BATCH = 2
DEC_SEQ = 4096
PAST_LEN = 256
GRID_W = 64
BLOCK = 128
EPS = 1e-6
N_HEADS = 16
N_KV_HEADS = 4
HEAD_DIM = 64
GROUP = N_HEADS // N_KV_HEADS
WINDOW = 128
ROPE_BASE = 10000.0
RET_HEADS = 4
RET_DK = 256
RET_DV = 512
RET_HK = RET_HEADS * RET_DK
RET_HV = RET_HEADS * RET_DV
D_FF = 2816
N_EXPERTS = 8
TOP_K = 2
D_FF_EXPERT = 3584
N_EVEN = 2
N_ODD = 2
NEG = -1e30
LOG2E = 1.4426950408889634

GROUP_TOKENS = 4096
N_PROMPT = BATCH * SEQ
N_SAMPLE = DEC_BATCH * DEC_SEQ
N_TOK = N_PROMPT + N_SAMPLE
N_GROUPS = N_TOK // GROUP_TOKENS

R_SHIFT_MIX, R_SCALE_MIX, R_GATE_MIX, R_SHIFT_FFN, R_SCALE_FFN, R_GATE_FFN, R_G_MIX, R_G_FFN = range(8)

VMEM_LIMIT_BYTES = 56 * 1024 * 1024
LANES = 128

TM = 2048
UP_CHUNK = 512
TM_DOWN = 512
DOWN_CHUNK = 128
MOE_TILE = 512
MOE_CHUNK = 256
MOE_SLOTS = TOP_K * N_TOK + N_EXPERTS * MOE_TILE
MOE_NT = MOE_SLOTS // MOE_TILE
TC_COMBINE = 256
COMBINE_GROUP = 16
DISPATCH_ROWS = 256


def _cparams(sem):
    return pltpu.CompilerParams(dimension_semantics=sem, vmem_limit_bytes=VMEM_LIMIT_BYTES)


def _normmod(x, g, scale, shift):
    ms = jnp.mean(x * x, axis=-1, keepdims=True)
    return (x * lax.rsqrt(ms + EPS) * g) * (1.0 + scale) + shift


def _mod_kernel(cond_ref, w_ref, b_ref, o_ref):
    c = cond_ref[...]
    s = c * jax.nn.sigmoid(c)
    o_ref[0] = jnp.dot(s.astype(BF16), w_ref[0].astype(BF16), preferred_element_type=F32) + b_ref[0]


def _modulations(cond8, w_mod, b_mod):
    tn = 2048
    n6 = 6 * D_MODEL
    return pl.pallas_call(
        _mod_kernel,
        grid=(DEPTH, n6 // tn),
        in_specs=[pl.BlockSpec((8, D_MODEL), lambda l, n: (0, 0)),
                  pl.BlockSpec((1, D_MODEL, tn), lambda l, n: (l, 0, n)),
                  pl.BlockSpec((1, 1, tn), lambda l, n: (l, 0, n))],
        out_specs=pl.BlockSpec((1, 8, tn), lambda l, n: (l, 0, n)),
        out_shape=jax.ShapeDtypeStruct((DEPTH, 8, n6), F32),
        compiler_params=_cparams(("arbitrary", "arbitrary")),
        name="modulations",
    )(cond8, w_mod, b_mod.reshape(DEPTH, 1, n6))


def _split_specs(tm, width):
    fst = N_PROMPT // tm
    return [pl.BlockSpec((tm, width), lambda m: (jnp.minimum(m, fst - 1), 0)),
            pl.BlockSpec((tm, width), lambda m: (jnp.maximum(m - fst, 0), 0))]


def _pre_kernel(xp_ref, xs_ref, pk_ref, h_ref, *, first_sample_tile):
    def norm(x_ref):
        pk = pk_ref[0, 0]
        h = _normmod(x_ref[...], pk[R_G_MIX:R_G_MIX + 1], pk[R_SCALE_MIX:R_SCALE_MIX + 1],
                     pk[R_SHIFT_MIX:R_SHIFT_MIX + 1])
        h_ref[...] = h.astype(BF16)

    m = pl.program_id(0)
    pl.when(m < first_sample_tile)(lambda: norm(xp_ref))
    pl.when(m >= first_sample_tile)(lambda: norm(xs_ref))


def _pre(x_pair, pack):
    tm = TM
    return pl.pallas_call(
        functools.partial(_pre_kernel, first_sample_tile=N_PROMPT // tm),
        grid=(N_TOK // tm,),
        in_specs=_split_specs(tm, D_MODEL) + [
            pl.BlockSpec((1, 1, 8, D_MODEL), lambda m: (0, (m * tm) // GROUP_TOKENS, 0, 0))],
        out_specs=pl.BlockSpec((tm, D_MODEL), lambda m: (m, 0)),
        out_shape=jax.ShapeDtypeStruct((N_TOK, D_MODEL), BF16),
        compiler_params=_cparams(("arbitrary",)),
        name="pre_norm",
    )(*x_pair, pack)


def _row_chunks(n_rows):
    return [pl.ds(c * UP_CHUNK, UP_CHUNK) for c in range(n_rows // UP_CHUNK)]


def _table_map(tm):
    first_sample_tile = N_PROMPT // tm
    tiles_per_seq = DEC_SEQ // tm
    return lambda m: (jnp.maximum(m - first_sample_tile, 0) % tiles_per_seq, 0)


def _rope64(y, cos, sin_signed):
    width = y.shape[-1]
    lane = lax.broadcasted_iota(I32, y.shape, 1)
    first = (lane % HEAD_DIM) < (HEAD_DIM // 2)
    swapped = jnp.where(first, pltpu.roll(y, width - HEAD_DIM // 2, 1), pltpu.roll(y, HEAD_DIM // 2, 1))
    reps = width // LANES
    c = jnp.concatenate([cos] * reps, axis=1) if reps > 1 else cos
    s = jnp.concatenate([sin_signed] * reps, axis=1) if reps > 1 else sin_signed
    return y * c + swapped * s


def _attn_proj_kernel(h_ref, w_ref, cos_ref, sin_ref, o_ref, wb_ref, *, scale, rope_cols, first_sample_tile):
    m = pl.program_id(0)

    @pl.when(m == 0)
    def _():
        wb_ref[...] = w_ref[0].astype(BF16)

    def project(rope):
        for rows in _row_chunks(h_ref.shape[0]):
            y = jnp.dot(h_ref[rows, :], wb_ref[...], preferred_element_type=F32)
            if scale != 1.0:
                y = y * scale
            if rope:
                roped = _rope64(y[:, :rope_cols], cos_ref[rows, :], sin_ref[rows, :])
                y = roped if rope_cols == y.shape[1] else jnp.concatenate([roped, y[:, rope_cols:]], axis=1)
            o_ref[rows, :] = y.astype(o_ref.dtype)

    pl.when(m >= first_sample_tile)(lambda: project(True))
    pl.when(m < first_sample_tile)(lambda: project(False))


def _attn_proj(h, w_qkv, layer, cos, sin, *, col0, ncols, rope_cols, scale, out_dtype):
    tm = TM
    kern = functools.partial(_attn_proj_kernel, scale=scale, rope_cols=rope_cols, first_sample_tile=N_PROMPT // tm)
    return pl.pallas_call(
        kern,
        grid=(N_TOK // tm,),
        in_specs=[pl.BlockSpec((tm, D_MODEL), lambda m: (m, 0)),
                  pl.BlockSpec((1, D_MODEL, ncols), lambda m: (layer, 0, col0 // ncols)),
                  pl.BlockSpec((tm, LANES), _table_map(tm)),
                  pl.BlockSpec((tm, LANES), _table_map(tm))],
        out_specs=pl.BlockSpec((tm, ncols), lambda m: (m, 0)),
        out_shape=jax.ShapeDtypeStruct((N_TOK, ncols), out_dtype),
        scratch_shapes=[pltpu.VMEM((D_MODEL, ncols), BF16)],
        compiler_params=_cparams(("arbitrary",)),
        name="attn_proj",
    )(h, w_qkv, cos, sin)


RET_TN = RET_HK


def _ret_proj_kernel(h_ref, w_ref, cos_ref, sin_ref, o_ref, wb_ref, *, first_sample_tile):
    n = pl.program_id(0)
    m = pl.program_id(1)

    @pl.when(m == 0)
    def _():
        wb_ref[...] = w_ref[0].astype(BF16)

    k_scale = jnp.where(n == 1, RET_DK ** -0.5, 1.0).astype(F32)
    half = RET_DK // 2

    def project(rope, gate=False):
        for rows in _row_chunks(h_ref.shape[0]):
            y = jnp.dot(h_ref[rows, :], wb_ref[...], preferred_element_type=F32) * k_scale
            if gate:
                y = y * jax.nn.sigmoid(y)
            if rope:
                c = cos_ref[rows, :]
                s = sin_ref[rows, :]
                parts = []
                for hh in range(RET_TN // RET_DK):
                    x1 = y[:, hh * RET_DK:hh * RET_DK + half]
                    x2 = y[:, hh * RET_DK + half:(hh + 1) * RET_DK]
                    parts.append(x1 * c - x2 * s)
                    parts.append(x1 * s + x2 * c)
                y = jnp.concatenate(parts, axis=1)
            o_ref[rows, :] = y.astype(BF16)

    gate_tile0 = (2 * RET_HK + RET_HV) // RET_TN
    do_rope = jnp.logical_and(m >= first_sample_tile, n < 2)
    is_gate = n >= gate_tile0
    pl.when(do_rope)(lambda: project(True))
    pl.when(is_gate)(lambda: project(False, gate=True))
    pl.when(jnp.logical_not(jnp.logical_or(do_rope, is_gate)))(lambda: project(False))


def _ret_proj(h, w_ret_in, layer, cos, sin):
    tm = TM
    ncols = 2 * RET_HK + 3 * RET_HV
    tmap = _table_map(tm)
    kern = functools.partial(_ret_proj_kernel, first_sample_tile=N_PROMPT // tm)
    return pl.pallas_call(
        kern,
        grid=(ncols // RET_TN, N_TOK // tm),
        in_specs=[pl.BlockSpec((tm, D_MODEL), lambda n, m: (m, 0)),
                  pl.BlockSpec((1, D_MODEL, RET_TN), lambda n, m: (layer, 0, n)),
                  pl.BlockSpec((tm, LANES), lambda n, m: tmap(m)),
                  pl.BlockSpec((tm, LANES), lambda n, m: tmap(m))],
        out_specs=pl.BlockSpec((tm, RET_TN), lambda n, m: (m, n)),
        out_shape=jax.ShapeDtypeStruct((N_TOK, ncols), BF16),
        scratch_shapes=[pltpu.VMEM((D_MODEL, RET_TN), BF16)],
        compiler_params=_cparams(("arbitrary", "arbitrary")),
        name="ret_proj",
    )(h, w_ret_in, cos, sin)


def _swiglu_kernel(h_ref, wg_ref, wu_ref, o_ref, wgb_ref, wub_ref):
    m = pl.program_id(1)

    @pl.when(m == 0)
    def _():
        wgb_ref[...] = wg_ref[0].astype(BF16)
        wub_ref[...] = wu_ref[0].astype(BF16)

    for rows in _row_chunks(h_ref.shape[0]):
        h = h_ref[rows, :]
        g = jnp.dot(h, wgb_ref[...], preferred_element_type=F32)
        u = jnp.dot(h, wub_ref[...], preferred_element_type=F32)
        o_ref[rows, :] = (g * jax.nn.sigmoid(g) * u).astype(BF16)


def _swiglu_up(h, w_gu, layer):
    tm, tf = 1024, D_FF // 2
    nf = D_FF // tf
    return pl.pallas_call(
        _swiglu_kernel,
        grid=(nf, N_TOK // tm),
        in_specs=[pl.BlockSpec((tm, D_MODEL), lambda f, m: (m, 0)),
                  pl.BlockSpec((1, D_MODEL, tf), lambda f, m: (layer, 0, f)),
                  pl.BlockSpec((1, D_MODEL, tf), lambda f, m: (layer, 0, nf + f))],
        out_specs=pl.BlockSpec((tm, tf), lambda f, m: (m, f)),
        out_shape=jax.ShapeDtypeStruct((N_TOK, D_FF), BF16),
        scratch_shapes=[pltpu.VMEM((D_MODEL, tf), BF16), pltpu.VMEM((D_MODEL, tf), BF16)],
        compiler_params=_cparams(("arbitrary", "arbitrary")),
        name="swiglu_up",
    )(h, w_gu, w_gu)


def _route(logits):
    lane = lax.broadcasted_iota(I32, logits.shape, 1).astype(F32)
    lg = jnp.where(lane < N_EXPERTS, logits, -jnp.inf)
    m1 = jnp.max(lg, axis=-1, keepdims=True)
    i1 = jnp.min(jnp.where(lg == m1, lane, float(LANES)), axis=-1, keepdims=True)
    lg2 = jnp.where(lane == i1, -jnp.inf, lg)
    m2 = jnp.max(lg2, axis=-1, keepdims=True)
    i2 = jnp.min(jnp.where(lg2 == m2, lane, float(LANES)), axis=-1, keepdims=True)
    e2 = jnp.exp(m2 - m1)
    p1 = 1.0 / (1.0 + e2)
    p2 = e2 / (1.0 + e2)
    out = jnp.where(lane == 0, i1, 0.0)
    out = jnp.where(lane == 1, i2, out)
    out = jnp.where(lane == 2, p1, out)
    out = jnp.where(lane == 3, p2, out)
    return out


def _epilogue(x_new, pk_next, rows, fin_ref, final):
    if final:
        ms = jnp.mean(x_new * x_new, axis=-1, keepdims=True)
        return x_new * lax.rsqrt(ms + EPS) * fin_ref[...]
    g_row, sc_row, sh_row = rows
    return _normmod(x_new, pk_next[g_row:g_row + 1], pk_next[sc_row:sc_row + 1], pk_next[sh_row:sh_row + 1])


def _down_kernel(*refs, n_a, n_x, first_sample_tile, gate_row, next_rows, router):
    a_refs = refs[:n_a]
    w_ref = refs[n_a]
    x_refs = refs[n_a + 1:n_a + 1 + n_x]
    pk_ref, pkn_ref = refs[n_a + 1 + n_x:n_a + 3 + n_x]
    pos = n_a + 3 + n_x
    if router:
        wrh_ref, wrl_ref = refs[pos:pos + 2]
        pos += 2
        xo_ref, h_ref, hf_ref, rt_ref, wb_ref = refs[pos:pos + 5]
    else:
        xo_ref, h_ref, wb_ref = refs[pos:pos + 3]
    m = pl.program_id(0)

    @pl.when(m == 0)
    def _():
        wb_ref[...] = w_ref[0].astype(BF16)

    def finish(a_ref, x_ref):
        pk = pk_ref[0, 0]
        pkn = pkn_ref[0, 0]
        for c in range(a_ref.shape[0] // DOWN_CHUNK):
            rows = pl.ds(c * DOWN_CHUNK, DOWN_CHUNK)
            y = jnp.dot(a_ref[rows, :], wb_ref[...], preferred_element_type=F32)
            x_new = x_ref[rows, :] + pk[gate_row:gate_row + 1] * y
            xo_ref[rows, :] = x_new
            hn = _epilogue(x_new, pkn, next_rows, None, False)
            hi = hn.astype(BF16)
            h_ref[rows, :] = hi
            if router:
                hf_ref[rows, :] = hn
        if router:
            hi = h_ref[...]
            lo = (hf_ref[...] - hi.astype(F32)).astype(BF16)
            logits = (jnp.dot(hi, wrh_ref[0], preferred_element_type=F32)
                      + jnp.dot(lo, wrh_ref[0], preferred_element_type=F32)
                      + jnp.dot(hi, wrl_ref[0], preferred_element_type=F32))
            rt_ref[...] = _route(logits)

    if n_a == 1 and n_x == 1:
        finish(a_refs[0], x_refs[0])
    else:
        pl.when(m < first_sample_tile)(lambda: finish(a_refs[0], x_refs[0]))
        pl.when(m >= first_sample_tile)(lambda: finish(a_refs[-1], x_refs[-1]))


def _down(a_list, w, layer, x, pack, pack_layer, gate_row, next_layer, next_rows, w_router=None, router_layer=0):
    tm = TM_DOWN
    kd = w.shape[1]
    n_a = len(a_list)
    fst = N_PROMPT // tm
    router = w_router is not None
    x_list = list(x) if isinstance(x, (list, tuple)) else [x]
    n_x = len(x_list)
    a_specs = [pl.BlockSpec((tm, kd), lambda m: (m, 0))] if n_a == 1 else _split_specs(tm, kd)
    x_specs = [pl.BlockSpec((tm, D_MODEL), lambda m: (m, 0))] if n_x == 1 else _split_specs(tm, D_MODEL)
    in_specs = a_specs + [pl.BlockSpec((1, kd, D_MODEL), lambda m: (layer, 0, 0))] + x_specs + [
        pl.BlockSpec((1, 1, 8, D_MODEL), lambda m: (pack_layer, (m * tm) // GROUP_TOKENS, 0, 0)),
        pl.BlockSpec((1, 1, 8, D_MODEL), lambda m: (next_layer, (m * tm) // GROUP_TOKENS, 0, 0)),
    ]
    args = list(a_list) + [w] + list(x_list) + [pack, pack]
    out_specs = [pl.BlockSpec((tm, D_MODEL), lambda m: (m, 0)),
                 pl.BlockSpec((tm, D_MODEL), lambda m: (m, 0))]
    out_shape = [jax.ShapeDtypeStruct((N_TOK, D_MODEL), F32),
                 jax.ShapeDtypeStruct((N_TOK, D_MODEL), BF16)]
    if router:
        in_specs += [pl.BlockSpec((1, D_MODEL, LANES), lambda m: (router_layer, 0, 0)),
                     pl.BlockSpec((1, D_MODEL, LANES), lambda m: (router_layer, 0, 0))]
        args += list(w_router)
        out_specs += [pl.BlockSpec((tm, D_MODEL), lambda m: (m, 0)),
                      pl.BlockSpec((tm, LANES), lambda m: (m, 0))]
        out_shape += [jax.ShapeDtypeStruct((N_TOK, D_MODEL), F32),
                      jax.ShapeDtypeStruct((N_TOK, LANES), F32)]
    kern = functools.partial(_down_kernel, n_a=n_a, n_x=n_x, first_sample_tile=fst, gate_row=gate_row,
                             next_rows=next_rows, router=router)
    return pl.pallas_call(
        kern,
        grid=(N_TOK // tm,),
        in_specs=in_specs,
        out_specs=out_specs,
        out_shape=out_shape,
        scratch_shapes=[pltpu.VMEM((kd, D_MODEL), BF16)],
        compiler_params=_cparams(("arbitrary",)),
        name="down_proj",
    )(*args)


def _sink_row(sink_ref, layer, kvh, width):
    return jnp.concatenate(
        [jnp.full((1, width), sink_ref[layer * N_HEADS + kvh * GROUP + g] * LOG2E, F32) for g in range(GROUP)],
        axis=1)


def _group_queries(q_ref, kvh):
    return jnp.concatenate(
        [q_ref[:, (kvh * GROUP + g) * HEAD_DIM:(kvh * GROUP + g + 1) * HEAD_DIM] for g in range(GROUP)], axis=0)


_NT = (((1,), (1,)), ((), ()))


def _ctx_attn_kernel(sink_ref, q_ref, kv_ref, o_ref, *, layer):
    kvw = N_KV_HEADS * HEAD_DIM
    k_all = kv_ref[:, :kvw].astype(BF16)
    v_t = kv_ref[:, kvw:].T.astype(BF16)
    outs = []
    for kvh in range(N_KV_HEADS):
        hs = slice(kvh * HEAD_DIM, (kvh + 1) * HEAD_DIM)
        s_t = lax.dot_general(k_all[:, hs], _group_queries(q_ref, kvh), _NT, preferred_element_type=F32)
        sink = _sink_row(sink_ref, layer, kvh, SEQ)
        m = jnp.maximum(jnp.max(s_t, axis=0, keepdims=True), sink)
        p = jnp.exp2(s_t - m)
        denom = jnp.sum(p, axis=0, keepdims=True) + jnp.exp2(sink - m)
        o_t = jnp.dot(v_t[hs], p.astype(BF16), preferred_element_type=F32) / denom
        outs += [o_t[:, g * SEQ:(g + 1) * SEQ] for g in range(GROUP)]
    o_ref[...] = jnp.concatenate(outs, axis=0).T.astype(BF16)


def _ctx_attention(q, kv, sink_flat, layer):
    kern = functools.partial(_ctx_attn_kernel, layer=layer)
    return pl.pallas_call(
        kern,
        grid_spec=pltpu.PrefetchScalarGridSpec(
            num_scalar_prefetch=1,
            grid=(BATCH,),
            in_specs=[pl.BlockSpec((SEQ, D_MODEL), lambda b, s: (b, 0)),
                      pl.BlockSpec((SEQ, 2 * N_KV_HEADS * HEAD_DIM), lambda b, s: (b, 0))],
            out_specs=pl.BlockSpec((SEQ, D_MODEL), lambda b, s: (b, 0))),
        out_shape=jax.ShapeDtypeStruct((N_PROMPT, D_MODEL), BF16),
        compiler_params=_cparams(("arbitrary",)),
        name="ctx_attention",
    )(sink_flat, q, kv)


def _lat_attn_kernel(sink_ref, q_ref, kvp_ref, kvc_ref, kvn_ref, ck_ref, cv_ref, o_ref, kc_ref, vct_ref, *, layer):
    blk = pl.program_id(1)
    kvw = N_KV_HEADS * HEAD_DIM
    cols = GROUP * BLOCK
    n_lat = 3 * BLOCK

    @pl.when(blk == 0)
    def _():
        kc_ref[...] = ck_ref[0, 0].astype(BF16)
        vct_ref[...] = cv_ref[0, 0].T.astype(BF16)

    j = lax.broadcasted_iota(I32, (n_lat, cols), 0)
    r = lax.broadcasted_iota(I32, (n_lat, cols), 1) % BLOCK
    lo = jnp.maximum(r, BLOCK - BLOCK * blk)
    hi = jnp.minimum(r + 2 * WINDOW, DEC_SEQ + BLOCK - 1 - BLOCK * blk)
    lat_cap = jnp.where(jnp.logical_and(j >= lo, j <= hi), jnp.inf, NEG).astype(F32)
    k_lat = jnp.concatenate([kvp_ref[:, :kvw], kvc_ref[:, :kvw], kvn_ref[:, :kvw]], axis=0).astype(BF16)
    v_lat_t = jnp.concatenate([kvp_ref[:, kvw:].T, kvc_ref[:, kvw:].T, kvn_ref[:, kvw:].T], axis=1).astype(BF16)
    outs = []
    for kvh in range(N_KV_HEADS):
        hs = slice(kvh * HEAD_DIM, (kvh + 1) * HEAD_DIM)
        q4 = _group_queries(q_ref, kvh)
        s_ctx = lax.dot_general(kc_ref[:, hs], q4, _NT, preferred_element_type=F32)
        s_lat = lax.dot_general(k_lat[:, hs], q4, _NT, preferred_element_type=F32)
        s_lat = jnp.minimum(s_lat, lat_cap)
        sink = _sink_row(sink_ref, layer, kvh, BLOCK)
        m = jnp.maximum(jnp.maximum(jnp.max(s_ctx, axis=0, keepdims=True),
                                    jnp.max(s_lat, axis=0, keepdims=True)), sink)
        p_ctx = jnp.exp2(s_ctx - m)
        p_lat = jnp.exp2(s_lat - m)
        denom = (jnp.sum(p_ctx, axis=0, keepdims=True) + jnp.sum(p_lat, axis=0, keepdims=True)
                 + jnp.exp2(sink - m))
        o_t = (jnp.dot(vct_ref[hs, :], p_ctx.astype(BF16), preferred_element_type=F32)
               + jnp.dot(v_lat_t[hs], p_lat.astype(BF16), preferred_element_type=F32)) / denom
        outs += [o_t[:, g * BLOCK:(g + 1) * BLOCK] for g in range(GROUP)]
    o_ref[...] = jnp.concatenate(outs, axis=0).T.astype(BF16)


def _lat_attention(q, kv, cache_k4, cache_v4, sink_flat, layer):
    nb = DEC_SEQ // BLOCK
    base = N_PROMPT // BLOCK
    kvc = 2 * N_KV_HEADS * HEAD_DIM

    def row(b, i):
        return base + b * nb + i

    kern = functools.partial(_lat_attn_kernel, layer=layer)
    return pl.pallas_call(
        kern,
        grid_spec=pltpu.PrefetchScalarGridSpec(
            num_scalar_prefetch=1,
            grid=(DEC_BATCH, nb),
            in_specs=[pl.BlockSpec((BLOCK, D_MODEL), lambda b, i, s: (row(b, i), 0)),
                      pl.BlockSpec((BLOCK, kvc), lambda b, i, s: (row(b, jnp.maximum(i - 1, 0)), 0)),
                      pl.BlockSpec((BLOCK, kvc), lambda b, i, s: (row(b, i), 0)),
                      pl.BlockSpec((BLOCK, kvc), lambda b, i, s: (row(b, jnp.minimum(i + 1, nb - 1)), 0)),
                      pl.BlockSpec((1, 1, PAST_LEN, N_KV_HEADS * HEAD_DIM), lambda b, i, s: (b, layer, 0, 0)),
                      pl.BlockSpec((1, 1, PAST_LEN, N_KV_HEADS * HEAD_DIM), lambda b, i, s: (b, layer, 0, 0))],
            out_specs=pl.BlockSpec((BLOCK, D_MODEL), lambda b, i, s: (b * nb + i, 0)),
            scratch_shapes=[pltpu.VMEM((PAST_LEN, N_KV_HEADS * HEAD_DIM), BF16),
                            pltpu.VMEM((N_KV_HEADS * HEAD_DIM, PAST_LEN), BF16)]),
        out_shape=jax.ShapeDtypeStruct((N_SAMPLE, D_MODEL), BF16),
        compiler_params=_cparams(("arbitrary", "arbitrary")),
        name="lat_attention",
    )(sink_flat, q, kv, kv, kv, cache_k4, cache_v4)


RET_SUB = 2
RET_ROWS = RET_SUB * BLOCK
RET_STEPS = N_TOK // RET_ROWS
PROMPT_STEPS = N_PROMPT // RET_ROWS
STEPS_PER_SAMPLE = DEC_SEQ // RET_ROWS
assert SEQ == RET_ROWS


def _ret_kernel(lg_ref, q_ref, k_ref, v_ref, gate_ref, s0_ref, *rest, backward, layer):
    part_ref = rest[0] if backward else None
    o_ref, sout_ref, state_ref = rest[-3:]
    step = pl.program_id(0)
    blk = (RET_STEPS - 1 - step) if backward else step
    in_prompt = blk < PROMPT_STEPS
    first_of_sample = (blk - PROMPT_STEPS) % STEPS_PER_SAMPLE == (STEPS_PER_SAMPLE - 1 if backward else 0)

    @pl.when(in_prompt)
    def _():
        state_ref[...] = jnp.zeros_like(state_ref)

    @pl.when(jnp.logical_and(jnp.logical_not(in_prompt), first_of_sample))
    def _():
        state_ref[...] = s0_ref[0, 0]

    ii = lax.broadcasted_iota(I32, (BLOCK, BLOCK), 0).astype(F32)
    jj = lax.broadcasted_iota(I32, (BLOCK, BLOCK), 1).astype(F32)
    dist = (jj - ii) if backward else (ii - jj)
    ti = lax.broadcasted_iota(I32, (BLOCK, 1), 0).astype(F32)
    q_pow = (BLOCK - ti) if backward else (ti + 1.0)
    k_pow = ti if backward else (BLOCK - 1.0 - ti)

    for h in range(RET_HEADS):
        lg = lg_ref[(layer * 2 + (1 if backward else 0)) * RET_HEADS + h]
        intra = jnp.where(dist >= 0, jnp.exp(lg * jnp.maximum(dist, 0.0)), 0.0)
        q_dec = jnp.exp(lg * q_pow)
        k_dec = jnp.exp(lg * k_pow)
        c_dec = jnp.exp(lg * BLOCK)
        for sub in (reversed(range(RET_SUB)) if backward else range(RET_SUB)):
            rows = pl.ds(sub * BLOCK, BLOCK)
            q = q_ref[rows, h * RET_DK:(h + 1) * RET_DK]
            k = k_ref[rows, h * RET_DK:(h + 1) * RET_DK]
            v = v_ref[rows, h * RET_DV:(h + 1) * RET_DV]
            s = state_ref[h]
            a = lax.dot_general(q, k, (((1,), (1,)), ((), ())), preferred_element_type=F32) * intra
            o = (jnp.dot(a.astype(BF16), v, preferred_element_type=F32)
                 + q_dec * jnp.dot(q, s.astype(BF16), preferred_element_type=F32))
            kd = (k.astype(F32) * k_dec).astype(BF16)
            state_ref[h] = c_dec * s + lax.dot_general(kd, v, (((0,), (0,)), ((), ())), preferred_element_type=F32)
            gate = gate_ref[rows, h * RET_DV:(h + 1) * RET_DV].astype(F32)
            on = o * lax.rsqrt(jnp.mean(o * o, axis=-1, keepdims=True) + EPS)
            res = on * gate
            if backward:
                res = res + part_ref[rows, h * RET_DV:(h + 1) * RET_DV].astype(F32)
            o_ref[rows, h * RET_DV:(h + 1) * RET_DV] = res.astype(BF16)

    @pl.when(in_prompt)
    def _():
        if sout_ref.shape[1] == 1:
            sout_ref[0, 0] = state_ref[...]
        else:
            for l in range(sout_ref.shape[1]):
                sout_ref[0, l] = state_ref[...] if l == layer else jnp.zeros_like(state_ref)


def _retention_pass(proj, s0, log_gamma_flat, layer, backward, partial=None, states=None):
    def blk(i):
        return (RET_STEPS - 1 - i) if backward else i

    def s0_map(i, lg):
        return (jnp.clip((blk(i) - PROMPT_STEPS) // STEPS_PER_SAMPLE, 0, DEC_BATCH - 1), layer, 0, 0, 0)

    sout_layers = N_ODD if states is None else 1

    def sout_map(i, lg):
        return (jnp.minimum(blk(i), BATCH - 1), 0 if states is None else layer, 0, 0, 0)

    gate_block = 3 if backward else 2
    in_specs = [pl.BlockSpec((RET_ROWS, RET_HK), lambda i, lg: (blk(i), 0)),
                pl.BlockSpec((RET_ROWS, RET_HK), lambda i, lg: (blk(i), 1)),
                pl.BlockSpec((RET_ROWS, RET_HV), lambda i, lg: (blk(i), 1)),
                pl.BlockSpec((RET_ROWS, RET_HV), lambda i, lg: (blk(i), gate_block)),
                pl.BlockSpec((1, 1, RET_HEADS, RET_DK, RET_DV), s0_map)]
    args = [proj, proj, proj, proj, s0]
    if backward:
        in_specs.append(pl.BlockSpec((RET_ROWS, RET_HV), lambda i, lg: (blk(i), 0)))
        args.append(partial)
    aliases = {}
    if states is not None:
        in_specs.append(pl.BlockSpec(memory_space=pl.ANY))
        args.append(states)
        aliases = {len(args): 1}
    kern = functools.partial(_ret_kernel, backward=backward, layer=layer)
    return pl.pallas_call(
        kern,
        grid_spec=pltpu.PrefetchScalarGridSpec(
            num_scalar_prefetch=1,
            grid=(RET_STEPS,),
            in_specs=in_specs,
            out_specs=[pl.BlockSpec((RET_ROWS, RET_HV), lambda i, lg: (blk(i), 0)),
                       pl.BlockSpec((1, sout_layers, RET_HEADS, RET_DK, RET_DV), sout_map)],
            scratch_shapes=[pltpu.VMEM((RET_HEADS, RET_DK, RET_DV), F32)]),
        out_shape=[jax.ShapeDtypeStruct((N_TOK, RET_HV), BF16),
                   jax.ShapeDtypeStruct((BATCH, N_ODD, RET_HEADS, RET_DK, RET_DV), F32)],
        input_output_aliases=aliases,
        compiler_params=_cparams(("arbitrary",)),
        name="retention_bwd" if backward else "retention_fwd",
    )(log_gamma_flat, *args)


def _dispatch_kernel(slot_ref, pend_ref, h_ref, o_hbm, zbuf, sem):
    i = pl.program_id(0)

    @pl.when(i == 0)
    def _():
        zbuf[...] = jnp.zeros_like(zbuf)

        def tile_fill(start):
            return pltpu.make_async_copy(zbuf, o_hbm.at[pl.ds(pl.multiple_of(start, MOE_TILE), MOE_TILE)], sem)

        def nonempty(e):
            return pend_ref[e] > (pend_ref[e - 1] if e else 0)

        def start_unused(t, carry):
            tile_fill(t * MOE_TILE).start()
            return carry

        def wait_unused(t, carry):
            tile_fill(t * MOE_TILE).wait()
            return carry

        first_unused = pend_ref[N_EXPERTS - 1] // MOE_TILE
        for e in range(N_EXPERTS):
            pl.when(nonempty(e))(lambda e=e: tile_fill(pend_ref[e] - MOE_TILE).start())
        lax.fori_loop(first_unused, MOE_NT, start_unused, 0)
        for e in range(N_EXPERTS):
            pl.when(nonempty(e))(lambda e=e: tile_fill(pend_ref[e] - MOE_TILE).wait())
        lax.fori_loop(first_unused, MOE_NT, wait_unused, 0)

    base = i * DISPATCH_ROWS * TOP_K

    def start(r, carry):
        for k in range(TOP_K):
            pltpu.make_async_copy(h_ref.at[pl.ds(r, 1)], o_hbm.at[pl.ds(slot_ref[base + r * TOP_K + k], 1)],
                                  sem).start(priority=k)
        return carry

    lax.fori_loop(0, DISPATCH_ROWS, start, 0, unroll=8)
    for k in range(TOP_K):
        pltpu.make_async_copy(h_ref, o_hbm.at[pl.ds(0, DISPATCH_ROWS)], sem).wait()


def _dispatch_rows(slot, pend, h):
    return pl.pallas_call(
        _dispatch_kernel,
        grid_spec=pltpu.PrefetchScalarGridSpec(
            num_scalar_prefetch=2,
            grid=(N_TOK // DISPATCH_ROWS,),
            in_specs=[pl.BlockSpec((DISPATCH_ROWS, D_MODEL), lambda i, s, p: (i, 0))],
            out_specs=pl.BlockSpec(memory_space=pl.ANY),
            scratch_shapes=[pltpu.VMEM((MOE_TILE, D_MODEL), F32), pltpu.SemaphoreType.DMA(())]),
        out_shape=jax.ShapeDtypeStruct((MOE_SLOTS, D_MODEL), F32),
        compiler_params=_cparams(("arbitrary",)),
        name="moe_dispatch",
    )(slot, pend, h)


def _for_occupied_chunks(n_rows, o_ref, chunk):
    n_chunks = MOE_TILE // MOE_CHUNK
    occupied = (n_rows + MOE_CHUNK - 1) // MOE_CHUNK

    def path(k):
        for c in range(n_chunks):
            rows = pl.ds(c * MOE_CHUNK, MOE_CHUNK)
            if c < k:
                chunk(rows)
            else:
                o_ref[rows, :] = jnp.zeros((MOE_CHUNK, o_ref.shape[1]), o_ref.dtype)

    for k in range(n_chunks + 1):
        pl.when(occupied == k)(functools.partial(path, k))


def _moe_up_kernel(te_ref, nv_ref, tr_ref, h_ref, wg_ref, wu_ref, o_ref, wgb_ref, wub_ref):
    m = pl.program_id(1)
    new_w = jnp.logical_or(m == 0, te_ref[m] != te_ref[jnp.maximum(m - 1, 0)])

    @pl.when(new_w)
    def _():
        wgb_ref[...] = wg_ref[0, 0].astype(BF16)
        wub_ref[...] = wu_ref[0, 0].astype(BF16)

    def chunk(rows):
        h = h_ref[rows, :].astype(BF16)
        g = jnp.dot(h, wgb_ref[...], preferred_element_type=F32)
        u = jnp.dot(h, wub_ref[...], preferred_element_type=F32)
        o_ref[rows, :] = (g * jax.nn.sigmoid(g) * u).astype(BF16)

    _for_occupied_chunks(tr_ref[m], o_ref, chunk)


def _moe_up(tile_expert, n_valid, tile_rows, hs, w_gu, layer):
    tf = 1792
    nf = D_FF_EXPERT // tf
    return pl.pallas_call(
        _moe_up_kernel,
        grid_spec=pltpu.PrefetchScalarGridSpec(
            num_scalar_prefetch=3,
            grid=(nf, MOE_NT),
            in_specs=[pl.BlockSpec((MOE_TILE, D_MODEL), lambda f, m, te, nv, tr: (jnp.minimum(m, nv[0] - 1), 0)),
                      pl.BlockSpec((1, 1, D_MODEL, tf), lambda f, m, te, nv, tr: (layer, te[m], 0, f)),
                      pl.BlockSpec((1, 1, D_MODEL, tf), lambda f, m, te, nv, tr: (layer, te[m], 0, nf + f))],
            out_specs=pl.BlockSpec((MOE_TILE, tf), lambda f, m, te, nv, tr: (m, f)),
            scratch_shapes=[pltpu.VMEM((D_MODEL, tf), BF16), pltpu.VMEM((D_MODEL, tf), BF16)]),
        out_shape=jax.ShapeDtypeStruct((MOE_SLOTS, D_FF_EXPERT), BF16),
        compiler_params=_cparams(("arbitrary", "arbitrary")),
        name="moe_up",
    )(tile_expert, n_valid, tile_rows, hs, w_gu, w_gu)


def _moe_down_kernel(te_ref, nv_ref, tr_ref, a_ref, w_ref, o_ref, wb_ref):
    m = pl.program_id(1)
    new_w = jnp.logical_or(m == 0, te_ref[m] != te_ref[jnp.maximum(m - 1, 0)])

    @pl.when(new_w)
    def _():
        wb_ref[...] = w_ref[0, 0].astype(BF16)

    def chunk(rows):
        o_ref[rows, :] = jnp.dot(a_ref[rows, :], wb_ref[...], preferred_element_type=F32)

    _for_occupied_chunks(tr_ref[m], o_ref, chunk)


def _moe_down(tile_expert, n_valid, tile_rows, act, w_down, layer):
    tn = D_MODEL
    return pl.pallas_call(
        _moe_down_kernel,
        grid_spec=pltpu.PrefetchScalarGridSpec(
            num_scalar_prefetch=3,
            grid=(D_MODEL // tn, MOE_NT),
            in_specs=[pl.BlockSpec((MOE_TILE, D_FF_EXPERT), lambda n, m, te, nv, tr: (jnp.minimum(m, nv[0] - 1), 0)),
                      pl.BlockSpec((1, 1, D_FF_EXPERT, tn), lambda n, m, te, nv, tr: (layer, te[m], 0, n))],
            out_specs=pl.BlockSpec((MOE_TILE, tn), lambda n, m, te, nv, tr: (m, n)),
            scratch_shapes=[pltpu.VMEM((D_FF_EXPERT, tn), BF16)]),
        out_shape=jax.ShapeDtypeStruct((MOE_SLOTS, D_MODEL), F32),
        compiler_params=_cparams(("arbitrary", "arbitrary")),
        name="moe_down",
    )(tile_expert, n_valid, tile_rows, act, w_down)


def _combine_kernel(slot_ref, y_hbm, x_ref, rt_ref, pk_ref, pkn_ref, fin_ref, *rest, tok0, final):
    if final:
        out_ref, ybuf, sems = rest
    else:
        xo_ref, h_ref, ybuf, sems = rest
    tc = TC_COMBINE
    i = pl.program_id(0)
    n = pl.num_programs(0)
    cur = i % 2
    nxt = 1 - cur

    def issue_rows(tile, buf, r0):
        base = (tok0 + tile * tc) * TOP_K
        for r in range(COMBINE_GROUP):
            for k in range(TOP_K):
                pltpu.make_async_copy(y_hbm.at[pl.ds(slot_ref[base + (r0 + r) * TOP_K + k], 1)],
                                      ybuf.at[buf, k, pl.ds(r0 + r, 1)], sems.at[buf]).start(priority=k)

    def wait_tile(buf):
        for k in range(TOP_K):
            pltpu.make_async_copy(y_hbm.at[pl.ds(0, tc)], ybuf.at[buf, k], sems.at[buf]).wait()

    def group_rows(g):
        return pl.multiple_of(g * COMBINE_GROUP, COMBINE_GROUP)

    @pl.when(i == 0)
    def _():
        def first(g, carry):
            issue_rows(0, 0, group_rows(g))
            return carry

        lax.fori_loop(0, tc // COMBINE_GROUP, first, 0)

    wait_tile(cur)
    nxt_tile = jnp.minimum(i + 1, n - 1)
    pk = pk_ref[0, 0]
    pkn = pkn_ref[0, 0]

    def group(g, carry):
        r0 = group_rows(g)
        issue_rows(nxt_tile, nxt, r0)
        rows = pl.ds(r0, COMBINE_GROUP)
        rt = rt_ref[rows, :]
        moe = rt[:, 2:3] * ybuf[cur, 0, rows, :] + rt[:, 3:4] * ybuf[cur, 1, rows, :]
        x_new = x_ref[rows, :] + pk[R_GATE_FFN:R_GATE_FFN + 1] * moe
        if final:
            out_ref[rows, :] = _epilogue(x_new, None, None, fin_ref, True)
        else:
            xo_ref[rows, :] = x_new
            h_ref[rows, :] = _epilogue(x_new, pkn, (R_G_MIX, R_SCALE_MIX, R_SHIFT_MIX), None, False).astype(BF16)
        return carry

    lax.fori_loop(0, tc // COMBINE_GROUP, group, 0, unroll=True)

    @pl.when(i == n - 1)
    def _():
        wait_tile(nxt)


def _combine(slot, y, x, route, pack, layer, next_layer, final_norm, tok0, n_rows, final):
    tc = TC_COMBINE
    b0 = tok0 // tc

    def tmap(m, s):
        return (b0 + m, 0)

    def pmap(l):
        return lambda m, s: (l, (tok0 + m * tc) // GROUP_TOKENS, 0, 0)

    if final:
        out_specs = pl.BlockSpec((tc, D_MODEL), lambda m, s: (m, 0))
        out_shape = jax.ShapeDtypeStruct((n_rows, D_MODEL), F32)
    else:
        out_specs = [pl.BlockSpec((tc, D_MODEL), lambda m, s: (m, 0)),
                     pl.BlockSpec((tc, D_MODEL), lambda m, s: (m, 0))]
        out_shape = [jax.ShapeDtypeStruct((n_rows, D_MODEL), F32),
                     jax.ShapeDtypeStruct((n_rows, D_MODEL), BF16)]
    kern = functools.partial(_combine_kernel, tok0=tok0, final=final)
    return pl.pallas_call(
        kern,
        grid_spec=pltpu.PrefetchScalarGridSpec(
            num_scalar_prefetch=1,
            grid=(n_rows // tc,),
            in_specs=[pl.BlockSpec(memory_space=pl.ANY),
                      pl.BlockSpec((tc, D_MODEL), tmap),
                      pl.BlockSpec((tc, LANES), tmap),
                      pl.BlockSpec((1, 1, 8, D_MODEL), pmap(layer)),
                      pl.BlockSpec((1, 1, 8, D_MODEL), pmap(next_layer)),
                      pl.BlockSpec((1, D_MODEL), lambda m, s: (0, 0))],
            out_specs=out_specs,
            scratch_shapes=[pltpu.VMEM((2, TOP_K, tc, D_MODEL), F32), pltpu.SemaphoreType.DMA((2,))]),
        out_shape=out_shape,
        compiler_params=_cparams(("arbitrary",)),
        name="moe_combine",
    )(slot, y, x, route, pack, pack, final_norm.reshape(1, D_MODEL))


def _routing_tables(route):
    e_flat = route[:, :TOP_K].astype(I32).reshape(-1)
    onehot = (e_flat[:, None] == jnp.arange(N_EXPERTS, dtype=I32)[None, :]).astype(I32)
    csum = jnp.cumsum(onehot, axis=0)
    rank = jnp.sum(onehot * (csum - 1), axis=1)
    count = csum[-1]
    padded = ((count + MOE_TILE - 1) // MOE_TILE) * MOE_TILE
    pend = jnp.cumsum(padded)
    poff = pend - padded
    slot = jnp.sum(onehot * poff[None, :], axis=1) + rank
    n_valid = pend[-1] // MOE_TILE
    tile_start = jnp.arange(MOE_NT, dtype=I32) * MOE_TILE
    te_raw = jnp.minimum(jnp.sum((tile_start[:, None] >= pend[None, :]).astype(I32), axis=1), N_EXPERTS - 1)
    last_e = jnp.max(jnp.where(tile_start < pend[-1], te_raw, 0))
    tile_expert = jnp.minimum(te_raw, last_e)
    group_rows_end = jnp.sum((te_raw[:, None] == jnp.arange(N_EXPERTS, dtype=I32)[None, :]) * (poff + count)[None, :],
                             axis=1)
    tile_rows = jnp.where(tile_start < pend[-1], jnp.clip(group_rows_end - tile_start, 0, MOE_TILE), 0)
    return (slot.astype(I32), pend.astype(I32), tile_expert.astype(I32), n_valid.astype(I32).reshape(1),
            tile_rows.astype(I32))


def _rope_tables(dh):
    nf = dh // 4
    t = jnp.arange(DEC_SEQ)
    freqs = ROPE_BASE ** (-jnp.arange(nf, dtype=F32) / nf)
    row = (t // GRID_W).astype(F32)
    col = (t % GRID_W).astype(F32)
    ang = jnp.concatenate([row[:, None] * freqs, col[:, None] * freqs], axis=-1)
    return jnp.cos(ang), jnp.sin(ang)


def kernel(x_prompt, x_sample, cache_k, cache_v, state_fwd, state_bwd, c, c_ctx, norm_mix, norm_ffn, w_mod, b_mod,
           w_qkv, w_attn_o, attn_sink, w_ret_in, ret_decay, w_ret_out, w_ffn_gu, w_ffn_down, w_router, w_exp_gu,
           w_exp_down, final_norm):
    x = (x_prompt.reshape(N_PROMPT, D_MODEL), x_sample.reshape(N_SAMPLE, D_MODEL))

    cond8 = jnp.concatenate([c_ctx[None, :], c, jnp.zeros((8 - 1 - DEC_BATCH, D_MODEL), F32)], axis=0)
    mods = _modulations(cond8, w_mod, b_mod)
    m3 = mods[:, :N_GROUPS].reshape(DEPTH, N_GROUPS, 6, D_MODEL)
    pack = jnp.concatenate([
        m3,
        jnp.broadcast_to(norm_mix[:, None, None, :], (DEPTH, N_GROUPS, 1, D_MODEL)),
        jnp.broadcast_to(norm_ffn[:, None, None, :], (DEPTH, N_GROUPS, 1, D_MODEL))], axis=2)

    cos_a, sin_a = _rope_tables(HEAD_DIM)
    cos_attn = jnp.tile(cos_a, (1, LANES // (HEAD_DIM // 2)))
    sin_attn = jnp.tile(jnp.concatenate([-sin_a, sin_a], axis=1), (1, LANES // HEAD_DIM))
    cos_ret, sin_ret = _rope_tables(RET_DK)

    kvw = N_KV_HEADS * HEAD_DIM
    cache_k4 = cache_k.reshape(DEC_BATCH, N_EVEN, PAST_LEN, kvw)
    cache_v4 = cache_v.reshape(DEC_BATCH, N_EVEN, PAST_LEN, kvw)
    sink_flat = attn_sink.reshape(-1)
    log_gamma = jax.nn.log_sigmoid(ret_decay.astype(F32)).reshape(-1)
    w_router_pad = jnp.pad(w_router, ((0, 0), (0, 0), (0, LANES - N_EXPERTS)))
    w_router_hi = w_router_pad.astype(BF16)
    w_router_lo = (w_router_pad - w_router_hi.astype(F32)).astype(BF16)

    mix_rows = (R_G_MIX, R_SCALE_MIX, R_SHIFT_MIX)
    ffn_rows = (R_G_FFN, R_SCALE_FFN, R_SHIFT_FFN)

    new_k, new_v = [], []
    new_sf = new_sb = None
    h = _pre(x, pack)
    y_prompt = y_sample = None
    for i in range(DEPTH):
        j = i // 2
        if i % 2 == 0:
            q = _attn_proj(h, w_qkv, j, cos_attn, sin_attn, col0=0, ncols=N_HEADS * HEAD_DIM,
                           rope_cols=N_HEADS * HEAD_DIM, scale=HEAD_DIM ** -0.5 * LOG2E, out_dtype=BF16)
            kv = _attn_proj(h, w_qkv, j, cos_attn, sin_attn, col0=N_HEADS * HEAD_DIM, ncols=2 * kvw,
                            rope_cols=kvw, scale=1.0, out_dtype=F32)
            new_k.append(kv[:N_PROMPT, :kvw].reshape(BATCH, SEQ, N_KV_HEADS, HEAD_DIM))
            new_v.append(kv[:N_PROMPT, kvw:].reshape(BATCH, SEQ, N_KV_HEADS, HEAD_DIM))
            o_ctx = _ctx_attention(q, kv, sink_flat, j)
            o_lat = _lat_attention(q, kv, cache_k4, cache_v4, sink_flat, j)
            x, h = _down([o_ctx, o_lat], w_attn_o, j, x, pack, i, R_GATE_MIX, i, ffn_rows)
            act = _swiglu_up(h, w_ffn_gu, j)
            x, h = _down([act], w_ffn_down, j, x, pack, i, R_GATE_FFN, i + 1, mix_rows)
        else:
            proj = _ret_proj(h, w_ret_in, j, cos_ret, sin_ret)
            part, new_sf = _retention_pass(proj, state_fwd, log_gamma, j, False, states=new_sf)
            o_ret, new_sb = _retention_pass(proj, state_bwd, log_gamma, j, True, partial=part, states=new_sb)
            x, h, hf, route = _down([o_ret], w_ret_out, j, x, pack, i, R_GATE_MIX, i, ffn_rows,
                                    w_router=(w_router_hi, w_router_lo), router_layer=j)
            slot, pend, tile_expert, n_valid, tile_rows = _routing_tables(route)
            hs = _dispatch_rows(slot, pend, hf)
            act = _moe_up(tile_expert, n_valid, tile_rows, hs, w_exp_gu, j)
            ys = _moe_down(tile_expert, n_valid, tile_rows, act, w_exp_down, j)
            if i == DEPTH - 1:
                y_prompt = _combine(slot, ys, x, route, pack, i, i, final_norm, 0, N_PROMPT, True)
                y_sample = _combine(slot, ys, x, route, pack, i, i, final_norm, N_PROMPT, N_SAMPLE, True)
            else:
                x, h = _combine(slot, ys, x, route, pack, i, i + 1, final_norm, 0, N_TOK, False)

    return (y_prompt.reshape(BATCH, SEQ, D_MODEL), y_sample.reshape(DEC_BATCH, DEC_SEQ, D_MODEL),
            jnp.stack(new_k, axis=1), jnp.stack(new_v, axis=1), new_sf, new_sb)
```

```python
import functools

import jax
import jax.numpy as jnp
from jax import lax
from jax.experimental import pallas as pl
from jax.experimental.pallas import tpu as pltpu

F32 = jnp.float32
BF16 = jnp.bfloat16
I32 = jnp.int32

D_MODEL = 1024
BATCH = 16
SEQ = 256
DEPTH = 4
DEC_BATCH = 2
DEC_SEQ = 4096
PAST_LEN = 256
GRID_W = 64
BLOCK = 128
EPS = 1e-6
N_HEADS = 16
N_KV_HEADS = 4
HEAD_DIM = 64
GROUP = N_HEADS // N_KV_HEADS
WINDOW = 128
ROPE_BASE = 10000.0
RET_HEADS = 4
RET_DK = 256
RET_DV = 512
RET_HK = RET_HEADS * RET_DK
RET_HV = RET_HEADS * RET_DV
D_FF = 2816
N_EXPERTS = 8
TOP_K = 2
D_FF_EXPERT = 3584
N_EVEN = 2
N_ODD = 2
NEG = -1e30
LOG2E = 1.4426950408889634

GROUP_TOKENS = 4096
N_PROMPT = BATCH * SEQ
N_SAMPLE = DEC_BATCH * DEC_SEQ
N_TOK = N_PROMPT + N_SAMPLE
N_GROUPS = N_TOK // GROUP_TOKENS

R_SHIFT_MIX, R_SCALE_MIX, R_GATE_MIX, R_SHIFT_FFN, R_SCALE_FFN, R_GATE_FFN, R_G_MIX, R_G_FFN = range(8)

VMEM_LIMIT_BYTES = 56 * 1024 * 1024
LANES = 128

TM = 2048
UP_CHUNK = 512
TM_DOWN = 512
DOWN_CHUNK = 128
MOE_TILE = 512
MOE_CHUNK = 256
MOE_SLOTS = TOP_K * N_TOK + N_EXPERTS * MOE_TILE
MOE_NT = MOE_SLOTS // MOE_TILE
TC_COMBINE = 256
COMBINE_GROUP = 16
DISPATCH_ROWS = 256


def _cparams(sem):
    return pltpu.CompilerParams(dimension_semantics=sem, vmem_limit_bytes=VMEM_LIMIT_BYTES)


def _normmod(x, g, scale, shift):
    ms = jnp.mean(x * x, axis=-1, keepdims=True)
    return (x * lax.rsqrt(ms + EPS) * g) * (1.0 + scale) + shift


def _mod_kernel(cond_ref, w_ref, b_ref, o_ref):
    c = cond_ref[...]
    s = c * jax.nn.sigmoid(c)
    o_ref[0] = jnp.dot(s.astype(BF16), w_ref[0].astype(BF16), preferred_element_type=F32) + b_ref[0]


def _modulations(cond8, w_mod, b_mod):
    tn = 2048
    n6 = 6 * D_MODEL
    return pl.pallas_call(
        _mod_kernel,
        grid=(DEPTH, n6 // tn),
        in_specs=[pl.BlockSpec((8, D_MODEL), lambda l, n: (0, 0)),
                  pl.BlockSpec((1, D_MODEL, tn), lambda l, n: (l, 0, n)),
                  pl.BlockSpec((1, 1, tn), lambda l, n: (l, 0, n))],
        out_specs=pl.BlockSpec((1, 8, tn), lambda l, n: (l, 0, n)),
        out_shape=jax.ShapeDtypeStruct((DEPTH, 8, n6), F32),
        compiler_params=_cparams(("arbitrary", "arbitrary")),
        name="modulations",
    )(cond8, w_mod, b_mod.reshape(DEPTH, 1, n6))


def _split_specs(tm, width):
    fst = N_PROMPT // tm
    return [pl.BlockSpec((tm, width), lambda m: (jnp.minimum(m, fst - 1), 0)),
            pl.BlockSpec((tm, width), lambda m: (jnp.maximum(m - fst, 0), 0))]


def _pre_kernel(xp_ref, xs_ref, pk_ref, h_ref, *, first_sample_tile):
    def norm(x_ref):
        pk = pk_ref[0, 0]
        h = _normmod(x_ref[...], pk[R_G_MIX:R_G_MIX + 1], pk[R_SCALE_MIX:R_SCALE_MIX + 1],
                     pk[R_SHIFT_MIX:R_SHIFT_MIX + 1])
        h_ref[...] = h.astype(BF16)

    m = pl.program_id(0)
    pl.when(m < first_sample_tile)(lambda: norm(xp_ref))
    pl.when(m >= first_sample_tile)(lambda: norm(xs_ref))


def _pre(x_pair, pack):
    tm = TM
    return pl.pallas_call(
        functools.partial(_pre_kernel, first_sample_tile=N_PROMPT // tm),
        grid=(N_TOK // tm,),
        in_specs=_split_specs(tm, D_MODEL) + [
            pl.BlockSpec((1, 1, 8, D_MODEL), lambda m: (0, (m * tm) // GROUP_TOKENS, 0, 0))],
        out_specs=pl.BlockSpec((tm, D_MODEL), lambda m: (m, 0)),
        out_shape=jax.ShapeDtypeStruct((N_TOK, D_MODEL), BF16),
        compiler_params=_cparams(("arbitrary",)),
        name="pre_norm",
    )(*x_pair, pack)


def _row_chunks(n_rows):
    return [pl.ds(c * UP_CHUNK, UP_CHUNK) for c in range(n_rows // UP_CHUNK)]


def _table_map(tm):
    first_sample_tile = N_PROMPT // tm
    tiles_per_seq = DEC_SEQ // tm
    return lambda m: (jnp.maximum(m - first_sample_tile, 0) % tiles_per_seq, 0)


def _rope64(y, cos, sin_signed):
    width = y.shape[-1]
    lane = lax.broadcasted_iota(I32, y.shape, 1)
    first = (lane % HEAD_DIM) < (HEAD_DIM // 2)
    swapped = jnp.where(first, pltpu.roll(y, width - HEAD_DIM // 2, 1), pltpu.roll(y, HEAD_DIM // 2, 1))
    reps = width // LANES
    c = jnp.concatenate([cos] * reps, axis=1) if reps > 1 else cos
    s = jnp.concatenate([sin_signed] * reps, axis=1) if reps > 1 else sin_signed
    return y * c + swapped * s


def _attn_proj_kernel(h_ref, w_ref, cos_ref, sin_ref, o_ref, wb_ref, *, scale, rope_cols, first_sample_tile):
    m = pl.program_id(0)

    @pl.when(m == 0)
    def _():
        wb_ref[...] = w_ref[0].astype(BF16)

    def project(rope):
        for rows in _row_chunks(h_ref.shape[0]):
            y = jnp.dot(h_ref[rows, :], wb_ref[...], preferred_element_type=F32)
            if scale != 1.0:
                y = y * scale
            if rope:
                roped = _rope64(y[:, :rope_cols], cos_ref[rows, :], sin_ref[rows, :])
                y = roped if rope_cols == y.shape[1] else jnp.concatenate([roped, y[:, rope_cols:]], axis=1)
            o_ref[rows, :] = y.astype(o_ref.dtype)

    pl.when(m >= first_sample_tile)(lambda: project(True))
    pl.when(m < first_sample_tile)(lambda: project(False))


def _attn_proj(h, w_qkv, layer, cos, sin, *, col0, ncols, rope_cols, scale, out_dtype):
    tm = TM
    kern = functools.partial(_attn_proj_kernel, scale=scale, rope_cols=rope_cols, first_sample_tile=N_PROMPT // tm)
    return pl.pallas_call(
        kern,
        grid=(N_TOK // tm,),
        in_specs=[pl.BlockSpec((tm, D_MODEL), lambda m: (m, 0)),
                  pl.BlockSpec((1, D_MODEL, ncols), lambda m: (layer, 0, col0 // ncols)),
                  pl.BlockSpec((tm, LANES), _table_map(tm)),
                  pl.BlockSpec((tm, LANES), _table_map(tm))],
        out_specs=pl.BlockSpec((tm, ncols), lambda m: (m, 0)),
        out_shape=jax.ShapeDtypeStruct((N_TOK, ncols), out_dtype),
        scratch_shapes=[pltpu.VMEM((D_MODEL, ncols), BF16)],
        compiler_params=_cparams(("arbitrary",)),
        name="attn_proj",
    )(h, w_qkv, cos, sin)


RET_TN = RET_HK


def _ret_proj_kernel(h_ref, w_ref, cos_ref, sin_ref, o_ref, wb_ref, *, first_sample_tile):
    n = pl.program_id(0)
    m = pl.program_id(1)

    @pl.when(m == 0)
    def _():
        wb_ref[...] = w_ref[0].astype(BF16)

    k_scale = jnp.where(n == 1, RET_DK ** -0.5, 1.0).astype(F32)
    half = RET_DK // 2

    def project(rope, gate=False):
        for rows in _row_chunks(h_ref.shape[0]):
            y = jnp.dot(h_ref[rows, :], wb_ref[...], preferred_element_type=F32) * k_scale
            if gate:
                y = y * jax.nn.sigmoid(y)
            if rope:
                c = cos_ref[rows, :]
                s = sin_ref[rows, :]
                parts = []
                for hh in range(RET_TN // RET_DK):
                    x1 = y[:, hh * RET_DK:hh * RET_DK + half]
                    x2 = y[:, hh * RET_DK + half:(hh + 1) * RET_DK]
                    parts.append(x1 * c - x2 * s)
                    parts.append(x1 * s + x2 * c)
                y = jnp.concatenate(parts, axis=1)
            o_ref[rows, :] = y.astype(BF16)

    gate_tile0 = (2 * RET_HK + RET_HV) // RET_TN
    do_rope = jnp.logical_and(m >= first_sample_tile, n < 2)
    is_gate = n >= gate_tile0
    pl.when(do_rope)(lambda: project(True))
    pl.when(is_gate)(lambda: project(False, gate=True))
    pl.when(jnp.logical_not(jnp.logical_or(do_rope, is_gate)))(lambda: project(False))


def _ret_proj(h, w_ret_in, layer, cos, sin):
    tm = TM
    ncols = 2 * RET_HK + 3 * RET_HV
    tmap = _table_map(tm)
    kern = functools.partial(_ret_proj_kernel, first_sample_tile=N_PROMPT // tm)
    return pl.pallas_call(
        kern,
        grid=(ncols // RET_TN, N_TOK // tm),
        in_specs=[pl.BlockSpec((tm, D_MODEL), lambda n, m: (m, 0)),
                  pl.BlockSpec((1, D_MODEL, RET_TN), lambda n, m: (layer, 0, n)),
                  pl.BlockSpec((tm, LANES), lambda n, m: tmap(m)),
                  pl.BlockSpec((tm, LANES), lambda n, m: tmap(m))],
        out_specs=pl.BlockSpec((tm, RET_TN), lambda n, m: (m, n)),
        out_shape=jax.ShapeDtypeStruct((N_TOK, ncols), BF16),
        scratch_shapes=[pltpu.VMEM((D_MODEL, RET_TN), BF16)],
        compiler_params=_cparams(("arbitrary", "arbitrary")),
        name="ret_proj",
    )(h, w_ret_in, cos, sin)


def _swiglu_kernel(h_ref, wg_ref, wu_ref, o_ref, wgb_ref, wub_ref):
    m = pl.program_id(1)

    @pl.when(m == 0)
    def _():
        wgb_ref[...] = wg_ref[0].astype(BF16)
        wub_ref[...] = wu_ref[0].astype(BF16)

    for rows in _row_chunks(h_ref.shape[0]):
        h = h_ref[rows, :]
        g = jnp.dot(h, wgb_ref[...], preferred_element_type=F32)
        u = jnp.dot(h, wub_ref[...], preferred_element_type=F32)
        o_ref[rows, :] = (g * jax.nn.sigmoid(g) * u).astype(BF16)


def _swiglu_up(h, w_gu, layer):
    tm, tf = 1024, D_FF // 2
    nf = D_FF // tf
    return pl.pallas_call(
        _swiglu_kernel,
        grid=(nf, N_TOK // tm),
        in_specs=[pl.BlockSpec((tm, D_MODEL), lambda f, m: (m, 0)),
                  pl.BlockSpec((1, D_MODEL, tf), lambda f, m: (layer, 0, f)),
                  pl.BlockSpec((1, D_MODEL, tf), lambda f, m: (layer, 0, nf + f))],
        out_specs=pl.BlockSpec((tm, tf), lambda f, m: (m, f)),
        out_shape=jax.ShapeDtypeStruct((N_TOK, D_FF), BF16),
        scratch_shapes=[pltpu.VMEM((D_MODEL, tf), BF16), pltpu.VMEM((D_MODEL, tf), BF16)],
        compiler_params=_cparams(("arbitrary", "arbitrary")),
        name="swiglu_up",
    )(h, w_gu, w_gu)


def _route(logits):
    lane = lax.broadcasted_iota(I32, logits.shape, 1).astype(F32)
    lg = jnp.where(lane < N_EXPERTS, logits, -jnp.inf)
    m1 = jnp.max(lg, axis=-1, keepdims=True)
    i1 = jnp.min(jnp.where(lg == m1, lane, float(LANES)), axis=-1, keepdims=True)
    lg2 = jnp.where(lane == i1, -jnp.inf, lg)
    m2 = jnp.max(lg2, axis=-1, keepdims=True)
    i2 = jnp.min(jnp.where(lg2 == m2, lane, float(LANES)), axis=-1, keepdims=True)
    e2 = jnp.exp(m2 - m1)
    p1 = 1.0 / (1.0 + e2)
    p2 = e2 / (1.0 + e2)
    out = jnp.where(lane == 0, i1, 0.0)
    out = jnp.where(lane == 1, i2, out)
    out = jnp.where(lane == 2, p1, out)
    out = jnp.where(lane == 3, p2, out)
    return out


def _epilogue(x_new, pk_next, rows, fin_ref, final):
    if final:
        ms = jnp.mean(x_new * x_new, axis=-1, keepdims=True)
        return x_new * lax.rsqrt(ms + EPS) * fin_ref[...]
    g_row, sc_row, sh_row = rows
    return _normmod(x_new, pk_next[g_row:g_row + 1], pk_next[sc_row:sc_row + 1], pk_next[sh_row:sh_row + 1])


def _down_kernel(*refs, n_a, n_x, first_sample_tile, gate_row, next_rows, router):
    a_refs = refs[:n_a]
    w_ref = refs[n_a]
    x_refs = refs[n_a + 1:n_a + 1 + n_x]
    pk_ref, pkn_ref = refs[n_a + 1 + n_x:n_a + 3 + n_x]
    pos = n_a + 3 + n_x
    if router:
        wrh_ref, wrl_ref = refs[pos:pos + 2]
        pos += 2
        xo_ref, h_ref, hf_ref, rt_ref, wb_ref = refs[pos:pos + 5]
    else:
        xo_ref, h_ref, wb_ref = refs[pos:pos + 3]
    m = pl.program_id(0)

    @pl.when(m == 0)
    def _():
        wb_ref[...] = w_ref[0].astype(BF16)

    def finish(a_ref, x_ref):
        pk = pk_ref[0, 0]
        pkn = pkn_ref[0, 0]
        for c in range(a_ref.shape[0] // DOWN_CHUNK):
            rows = pl.ds(c * DOWN_CHUNK, DOWN_CHUNK)
            y = jnp.dot(a_ref[rows, :], wb_ref[...], preferred_element_type=F32)
            x_new = x_ref[rows, :] + pk[gate_row:gate_row + 1] * y
            xo_ref[rows, :] = x_new
            hn = _epilogue(x_new, pkn, next_rows, None, False)
            hi = hn.astype(BF16)
            h_ref[rows, :] = hi
            if router:
                hf_ref[rows, :] = hn
        if router:
            hi = h_ref[...]
            lo = (hf_ref[...] - hi.astype(F32)).astype(BF16)
            logits = (jnp.dot(hi, wrh_ref[0], preferred_element_type=F32)
                      + jnp.dot(lo, wrh_ref[0], preferred_element_type=F32)
                      + jnp.dot(hi, wrl_ref[0], preferred_element_type=F32))
            rt_ref[...] = _route(logits)

    if n_a == 1 and n_x == 1:
        finish(a_refs[0], x_refs[0])
    else:
        pl.when(m < first_sample_tile)(lambda: finish(a_refs[0], x_refs[0]))
        pl.when(m >= first_sample_tile)(lambda: finish(a_refs[-1], x_refs[-1]))


def _down(a_list, w, layer, x, pack, pack_layer, gate_row, next_layer, next_rows, w_router=None, router_layer=0):
    tm = TM_DOWN
    kd = w.shape[1]
    n_a = len(a_list)
    fst = N_PROMPT // tm
    router = w_router is not None
    x_list = list(x) if isinstance(x, (list, tuple)) else [x]
    n_x = len(x_list)
    a_specs = [pl.BlockSpec((tm, kd), lambda m: (m, 0))] if n_a == 1 else _split_specs(tm, kd)
    x_specs = [pl.BlockSpec((tm, D_MODEL), lambda m: (m, 0))] if n_x == 1 else _split_specs(tm, D_MODEL)
    in_specs = a_specs + [pl.BlockSpec((1, kd, D_MODEL), lambda m: (layer, 0, 0))] + x_specs + [
        pl.BlockSpec((1, 1, 8, D_MODEL), lambda m: (pack_layer, (m * tm) // GROUP_TOKENS, 0, 0)),
        pl.BlockSpec((1, 1, 8, D_MODEL), lambda m: (next_layer, (m * tm) // GROUP_TOKENS, 0, 0)),
    ]
    args = list(a_list) + [w] + list(x_list) + [pack, pack]
    out_specs = [pl.BlockSpec((tm, D_MODEL), lambda m: (m, 0)),
                 pl.BlockSpec((tm, D_MODEL), lambda m: (m, 0))]
    out_shape = [jax.ShapeDtypeStruct((N_TOK, D_MODEL), F32),
                 jax.ShapeDtypeStruct((N_TOK, D_MODEL), BF16)]
    if router:
        in_specs += [pl.BlockSpec((1, D_MODEL, LANES), lambda m: (router_layer, 0, 0)),
                     pl.BlockSpec((1, D_MODEL, LANES), lambda m: (router_layer, 0, 0))]
        args += list(w_router)
        out_specs += [pl.BlockSpec((tm, D_MODEL), lambda m: (m, 0)),
                      pl.BlockSpec((tm, LANES), lambda m: (m, 0))]
        out_shape += [jax.ShapeDtypeStruct((N_TOK, D_MODEL), F32),
                      jax.ShapeDtypeStruct((N_TOK, LANES), F32)]
    kern = functools.partial(_down_kernel, n_a=n_a, n_x=n_x, first_sample_tile=fst, gate_row=gate_row,
                             next_rows=next_rows, router=router)
    return pl.pallas_call(
        kern,
        grid=(N_TOK // tm,),
        in_specs=in_specs,
        out_specs=out_specs,
        out_shape=out_shape,
        scratch_shapes=[pltpu.VMEM((kd, D_MODEL), BF16)],
        compiler_params=_cparams(("arbitrary",)),
        name="down_proj",
    )(*args)


def _sink_row(sink_ref, layer, kvh, width):
    return jnp.concatenate(
        [jnp.full((1, width), sink_ref[layer * N_HEADS + kvh * GROUP + g] * LOG2E, F32) for g in range(GROUP)],
        axis=1)


def _group_queries(q_ref, kvh):
    return jnp.concatenate(
        [q_ref[:, (kvh * GROUP + g) * HEAD_DIM:(kvh * GROUP + g + 1) * HEAD_DIM] for g in range(GROUP)], axis=0)


_NT = (((1,), (1,)), ((), ()))


def _ctx_attn_kernel(sink_ref, q_ref, kv_ref, o_ref, *, layer):
    kvw = N_KV_HEADS * HEAD_DIM
    k_all = kv_ref[:, :kvw].astype(BF16)
    v_t = kv_ref[:, kvw:].T.astype(BF16)
    outs = []
    for kvh in range(N_KV_HEADS):
        hs = slice(kvh * HEAD_DIM, (kvh + 1) * HEAD_DIM)
        s_t = lax.dot_general(k_all[:, hs], _group_queries(q_ref, kvh), _NT, preferred_element_type=F32)
        sink = _sink_row(sink_ref, layer, kvh, SEQ)
        m = jnp.maximum(jnp.max(s_t, axis=0, keepdims=True), sink)
        p = jnp.exp2(s_t - m)
        denom = jnp.sum(p, axis=0, keepdims=True) + jnp.exp2(sink - m)
        o_t = jnp.dot(v_t[hs], p.astype(BF16), preferred_element_type=F32) / denom
        outs += [o_t[:, g * SEQ:(g + 1) * SEQ] for g in range(GROUP)]
    o_ref[...] = jnp.concatenate(outs, axis=0).T.astype(BF16)


def _ctx_attention(q, kv, sink_flat, layer):
    kern = functools.partial(_ctx_attn_kernel, layer=layer)
    return pl.pallas_call(
        kern,
        grid_spec=pltpu.PrefetchScalarGridSpec(
            num_scalar_prefetch=1,
            grid=(BATCH,),
            in_specs=[pl.BlockSpec((SEQ, D_MODEL), lambda b, s: (b, 0)),
                      pl.BlockSpec((SEQ, 2 * N_KV_HEADS * HEAD_DIM), lambda b, s: (b, 0))],
            out_specs=pl.BlockSpec((SEQ, D_MODEL), lambda b, s: (b, 0))),
        out_shape=jax.ShapeDtypeStruct((N_PROMPT, D_MODEL), BF16),
        compiler_params=_cparams(("arbitrary",)),
        name="ctx_attention",
    )(sink_flat, q, kv)


def _lat_attn_kernel(sink_ref, q_ref, kvp_ref, kvc_ref, kvn_ref, ck_ref, cv_ref, o_ref, kc_ref, vct_ref, *, layer):
    blk = pl.program_id(1)
    kvw = N_KV_HEADS * HEAD_DIM
    cols = GROUP * BLOCK
    n_lat = 3 * BLOCK

    @pl.when(blk == 0)
    def _():
        kc_ref[...] = ck_ref[0, 0].astype(BF16)
        vct_ref[...] = cv_ref[0, 0].T.astype(BF16)

    j = lax.broadcasted_iota(I32, (n_lat, cols), 0)
    r = lax.broadcasted_iota(I32, (n_lat, cols), 1) % BLOCK
    lo = jnp.maximum(r, BLOCK - BLOCK * blk)
    hi = jnp.minimum(r + 2 * WINDOW, DEC_SEQ + BLOCK - 1 - BLOCK * blk)
    lat_cap = jnp.where(jnp.logical_and(j >= lo, j <= hi), jnp.inf, NEG).astype(F32)
    k_lat = jnp.concatenate([kvp_ref[:, :kvw], kvc_ref[:, :kvw], kvn_ref[:, :kvw]], axis=0).astype(BF16)
    v_lat_t = jnp.concatenate([kvp_ref[:, kvw:].T, kvc_ref[:, kvw:].T, kvn_ref[:, kvw:].T], axis=1).astype(BF16)
    outs = []
    for kvh in range(N_KV_HEADS):
        hs = slice(kvh * HEAD_DIM, (kvh + 1) * HEAD_DIM)
        q4 = _group_queries(q_ref, kvh)
        s_ctx = lax.dot_general(kc_ref[:, hs], q4, _NT, preferred_element_type=F32)
        s_lat = lax.dot_general(k_lat[:, hs], q4, _NT, preferred_element_type=F32)
        s_lat = jnp.minimum(s_lat, lat_cap)
        sink = _sink_row(sink_ref, layer, kvh, BLOCK)
        m = jnp.maximum(jnp.maximum(jnp.max(s_ctx, axis=0, keepdims=True),
                                    jnp.max(s_lat, axis=0, keepdims=True)), sink)
        p_ctx = jnp.exp2(s_ctx - m)
        p_lat = jnp.exp2(s_lat - m)
        denom = (jnp.sum(p_ctx, axis=0, keepdims=True) + jnp.sum(p_lat, axis=0, keepdims=True)
                 + jnp.exp2(sink - m))
        o_t = (jnp.dot(vct_ref[hs, :], p_ctx.astype(BF16), preferred_element_type=F32)
               + jnp.dot(v_lat_t[hs], p_lat.astype(BF16), preferred_element_type=F32)) / denom
        outs += [o_t[:, g * BLOCK:(g + 1) * BLOCK] for g in range(GROUP)]
    o_ref[...] = jnp.concatenate(outs, axis=0).T.astype(BF16)


def _lat_attention(q, kv, cache_k4, cache_v4, sink_flat, layer):
    nb = DEC_SEQ // BLOCK
    base = N_PROMPT // BLOCK
    kvc = 2 * N_KV_HEADS * HEAD_DIM

    def row(b, i):
        return base + b * nb + i

    kern = functools.partial(_lat_attn_kernel, layer=layer)
    return pl.pallas_call(
        kern,
        grid_spec=pltpu.PrefetchScalarGridSpec(
            num_scalar_prefetch=1,
            grid=(DEC_BATCH, nb),
            in_specs=[pl.BlockSpec((BLOCK, D_MODEL), lambda b, i, s: (row(b, i), 0)),
                      pl.BlockSpec((BLOCK, kvc), lambda b, i, s: (row(b, jnp.maximum(i - 1, 0)), 0)),
                      pl.BlockSpec((BLOCK, kvc), lambda b, i, s: (row(b, i), 0)),
                      pl.BlockSpec((BLOCK, kvc), lambda b, i, s: (row(b, jnp.minimum(i + 1, nb - 1)), 0)),
                      pl.BlockSpec((1, 1, PAST_LEN, N_KV_HEADS * HEAD_DIM), lambda b, i, s: (b, layer, 0, 0)),
                      pl.BlockSpec((1, 1, PAST_LEN, N_KV_HEADS * HEAD_DIM), lambda b, i, s: (b, layer, 0, 0))],
            out_specs=pl.BlockSpec((BLOCK, D_MODEL), lambda b, i, s: (b * nb + i, 0)),
            scratch_shapes=[pltpu.VMEM((PAST_LEN, N_KV_HEADS * HEAD_DIM), BF16),
                            pltpu.VMEM((N_KV_HEADS * HEAD_DIM, PAST_LEN), BF16)]),
        out_shape=jax.ShapeDtypeStruct((N_SAMPLE, D_MODEL), BF16),
        compiler_params=_cparams(("arbitrary", "arbitrary")),
        name="lat_attention",
    )(sink_flat, q, kv, kv, kv, cache_k4, cache_v4)


RET_SUB = 2
RET_ROWS = RET_SUB * BLOCK
RET_STEPS = N_TOK // RET_ROWS
PROMPT_STEPS = N_PROMPT // RET_ROWS
STEPS_PER_SAMPLE = DEC_SEQ // RET_ROWS
assert SEQ == RET_ROWS


def _ret_kernel(lg_ref, q_ref, k_ref, v_ref, gate_ref, s0_ref, *rest, backward, layer):
    part_ref = rest[0] if backward else None
    o_ref, sout_ref, state_ref = rest[-3:]
    step = pl.program_id(0)
    blk = (RET_STEPS - 1 - step) if backward else step
    in_prompt = blk < PROMPT_STEPS
    first_of_sample = (blk - PROMPT_STEPS) % STEPS_PER_SAMPLE == (STEPS_PER_SAMPLE - 1 if backward else 0)

    @pl.when(in_prompt)
    def _():
        state_ref[...] = jnp.zeros_like(state_ref)

    @pl.when(jnp.logical_and(jnp.logical_not(in_prompt), first_of_sample))
    def _():
        state_ref[...] = s0_ref[0, 0]

    ii = lax.broadcasted_iota(I32, (BLOCK, BLOCK), 0).astype(F32)
    jj = lax.broadcasted_iota(I32, (BLOCK, BLOCK), 1).astype(F32)
    dist = (jj - ii) if backward else (ii - jj)
    ti = lax.broadcasted_iota(I32, (BLOCK, 1), 0).astype(F32)
    q_pow = (BLOCK - ti) if backward else (ti + 1.0)
    k_pow = ti if backward else (BLOCK - 1.0 - ti)

    for h in range(RET_HEADS):
        lg = lg_ref[(layer * 2 + (1 if backward else 0)) * RET_HEADS + h]
        intra = jnp.where(dist >= 0, jnp.exp(lg * jnp.maximum(dist, 0.0)), 0.0)
        q_dec = jnp.exp(lg * q_pow)
        k_dec = jnp.exp(lg * k_pow)
        c_dec = jnp.exp(lg * BLOCK)
        for sub in (reversed(range(RET_SUB)) if backward else range(RET_SUB)):
            rows = pl.ds(sub * BLOCK, BLOCK)
            q = q_ref[rows, h * RET_DK:(h + 1) * RET_DK]
            k = k_ref[rows, h * RET_DK:(h + 1) * RET_DK]
            v = v_ref[rows, h * RET_DV:(h + 1) * RET_DV]
            s = state_ref[h]
            a = lax.dot_general(q, k, (((1,), (1,)), ((), ())), preferred_element_type=F32) * intra
            o = (jnp.dot(a.astype(BF16), v, preferred_element_type=F32)
                 + q_dec * jnp.dot(q, s.astype(BF16), preferred_element_type=F32))
            kd = (k.astype(F32) * k_dec).astype(BF16)
            state_ref[h] = c_dec * s + lax.dot_general(kd, v, (((0,), (0,)), ((), ())), preferred_element_type=F32)
            gate = gate_ref[rows, h * RET_DV:(h + 1) * RET_DV].astype(F32)
            on = o * lax.rsqrt(jnp.mean(o * o, axis=-1, keepdims=True) + EPS)
            res = on * gate
            if backward:
                res = res + part_ref[rows, h * RET_DV:(h + 1) * RET_DV].astype(F32)
            o_ref[rows, h * RET_DV:(h + 1) * RET_DV] = res.astype(BF16)

    @pl.when(in_prompt)
    def _():
        if sout_ref.shape[1] == 1:
            sout_ref[0, 0] = state_ref[...]
        else:
            for l in range(sout_ref.shape[1]):
                sout_ref[0, l] = state_ref[...] if l == layer else jnp.zeros_like(state_ref)


def _retention_pass(proj, s0, log_gamma_flat, layer, backward, partial=None, states=None):
    def blk(i):
        return (RET_STEPS - 1 - i) if backward else i

    def s0_map(i, lg):
        return (jnp.clip((blk(i) - PROMPT_STEPS) // STEPS_PER_SAMPLE, 0, DEC_BATCH - 1), layer, 0, 0, 0)

    sout_layers = N_ODD if states is None else 1

    def sout_map(i, lg):
        return (jnp.minimum(blk(i), BATCH - 1), 0 if states is None else layer, 0, 0, 0)

    gate_block = 3 if backward else 2
    in_specs = [pl.BlockSpec((RET_ROWS, RET_HK), lambda i, lg: (blk(i), 0)),
                pl.BlockSpec((RET_ROWS, RET_HK), lambda i, lg: (blk(i), 1)),
                pl.BlockSpec((RET_ROWS, RET_HV), lambda i, lg: (blk(i), 1)),
                pl.BlockSpec((RET_ROWS, RET_HV), lambda i, lg: (blk(i), gate_block)),
                pl.BlockSpec((1, 1, RET_HEADS, RET_DK, RET_DV), s0_map)]
    args = [proj, proj, proj, proj, s0]
    if backward:
        in_specs.append(pl.BlockSpec((RET_ROWS, RET_HV), lambda i, lg: (blk(i), 0)))
        args.append(partial)
    aliases = {}
    if states is not None:
        in_specs.append(pl.BlockSpec(memory_space=pl.ANY))
        args.append(states)
        aliases = {len(args): 1}
    kern = functools.partial(_ret_kernel, backward=backward, layer=layer)
    return pl.pallas_call(
        kern,
        grid_spec=pltpu.PrefetchScalarGridSpec(
            num_scalar_prefetch=1,
            grid=(RET_STEPS,),
            in_specs=in_specs,
            out_specs=[pl.BlockSpec((RET_ROWS, RET_HV), lambda i, lg: (blk(i), 0)),
                       pl.BlockSpec((1, sout_layers, RET_HEADS, RET_DK, RET_DV), sout_map)],
            scratch_shapes=[pltpu.VMEM((RET_HEADS, RET_DK, RET_DV), F32)]),
        out_shape=[jax.ShapeDtypeStruct((N_TOK, RET_HV), BF16),
                   jax.ShapeDtypeStruct((BATCH, N_ODD, RET_HEADS, RET_DK, RET_DV), F32)],
        input_output_aliases=aliases,
        compiler_params=_cparams(("arbitrary",)),
        name="retention_bwd" if backward else "retention_fwd",
    )(log_gamma_flat, *args)


def _dispatch_kernel(slot_ref, pend_ref, h_ref, o_hbm, zbuf, sem):
    i = pl.program_id(0)

    @pl.when(i == 0)
    def _():
        zbuf[...] = jnp.zeros_like(zbuf)

        def tile_fill(start):
            return pltpu.make_async_copy(zbuf, o_hbm.at[pl.ds(pl.multiple_of(start, MOE_TILE), MOE_TILE)], sem)

        def nonempty(e):
            return pend_ref[e] > (pend_ref[e - 1] if e else 0)

        def start_unused(t, carry):
            tile_fill(t * MOE_TILE).start()
            return carry

        def wait_unused(t, carry):
            tile_fill(t * MOE_TILE).wait()
            return carry

        first_unused = pend_ref[N_EXPERTS - 1] // MOE_TILE
        for e in range(N_EXPERTS):
            pl.when(nonempty(e))(lambda e=e: tile_fill(pend_ref[e] - MOE_TILE).start())
        lax.fori_loop(first_unused, MOE_NT, start_unused, 0)
        for e in range(N_EXPERTS):
            pl.when(nonempty(e))(lambda e=e: tile_fill(pend_ref[e] - MOE_TILE).wait())
        lax.fori_loop(first_unused, MOE_NT, wait_unused, 0)

    base = i * DISPATCH_ROWS * TOP_K

    def start(r, carry):
        for k in range(TOP_K):
            pltpu.make_async_copy(h_ref.at[pl.ds(r, 1)], o_hbm.at[pl.ds(slot_ref[base + r * TOP_K + k], 1)],
                                  sem).start()
        return carry

    lax.fori_loop(0, DISPATCH_ROWS, start, 0, unroll=8)
    for k in range(TOP_K):
        pltpu.make_async_copy(h_ref, o_hbm.at[pl.ds(0, DISPATCH_ROWS)], sem).wait()


def _dispatch_rows(slot, pend, h):
    return pl.pallas_call(
        _dispatch_kernel,
        grid_spec=pltpu.PrefetchScalarGridSpec(
            num_scalar_prefetch=2,
            grid=(N_TOK // DISPATCH_ROWS,),
            in_specs=[pl.BlockSpec((DISPATCH_ROWS, D_MODEL), lambda i, s, p: (i, 0))],
            out_specs=pl.BlockSpec(memory_space=pl.ANY),
            scratch_shapes=[pltpu.VMEM((MOE_TILE, D_MODEL), F32), pltpu.SemaphoreType.DMA(())]),
        out_shape=jax.ShapeDtypeStruct((MOE_SLOTS, D_MODEL), F32),
        compiler_params=_cparams(("arbitrary",)),
        name="moe_dispatch",
    )(slot, pend, h)


def _for_occupied_chunks(n_rows, o_ref, chunk):
    n_chunks = MOE_TILE // MOE_CHUNK
    occupied = (n_rows + MOE_CHUNK - 1) // MOE_CHUNK

    def path(k):
        for c in range(n_chunks):
            rows = pl.ds(c * MOE_CHUNK, MOE_CHUNK)
            if c < k:
                chunk(rows)
            elif len(o_ref.shape) == 2:
                o_ref[rows, :] = jnp.zeros((MOE_CHUNK, o_ref.shape[1]), o_ref.dtype)
            else:
                o_ref[rows, 0, :] = jnp.zeros((MOE_CHUNK, o_ref.shape[2]), o_ref.dtype)

    for k in range(n_chunks + 1):
        pl.when(occupied == k)(functools.partial(path, k))


def _moe_up_kernel(te_ref, nv_ref, tr_ref, h_ref, wg_ref, wu_ref, o_ref, wgb_ref, wub_ref):
    m = pl.program_id(1)
    new_w = jnp.logical_or(m == 0, te_ref[m] != te_ref[jnp.maximum(m - 1, 0)])

    @pl.when(new_w)
    def _():
        wgb_ref[...] = wg_ref[0, 0].astype(BF16)
        wub_ref[...] = wu_ref[0, 0].astype(BF16)

    def chunk(rows):
        h = h_ref[rows, :].astype(BF16)
        g = jnp.dot(h, wgb_ref[...], preferred_element_type=F32)
        u = jnp.dot(h, wub_ref[...], preferred_element_type=F32)
        o_ref[rows, :] = (g * jax.nn.sigmoid(g) * u).astype(BF16)

    _for_occupied_chunks(tr_ref[m], o_ref, chunk)


def _moe_up(tile_expert, n_valid, tile_rows, hs, w_gu, layer):
    tf = 1792
    nf = D_FF_EXPERT // tf
    return pl.pallas_call(
        _moe_up_kernel,
        grid_spec=pltpu.PrefetchScalarGridSpec(
            num_scalar_prefetch=3,
            grid=(nf, MOE_NT),
            in_specs=[pl.BlockSpec((MOE_TILE, D_MODEL), lambda f, m, te, nv, tr: (jnp.minimum(m, nv[0] - 1), 0)),
                      pl.BlockSpec((1, 1, D_MODEL, tf), lambda f, m, te, nv, tr: (layer, te[m], 0, f)),
                      pl.BlockSpec((1, 1, D_MODEL, tf), lambda f, m, te, nv, tr: (layer, te[m], 0, nf + f))],
            out_specs=pl.BlockSpec((MOE_TILE, tf), lambda f, m, te, nv, tr: (m, f)),
            scratch_shapes=[pltpu.VMEM((D_MODEL, tf), BF16), pltpu.VMEM((D_MODEL, tf), BF16)]),
        out_shape=jax.ShapeDtypeStruct((MOE_SLOTS, D_FF_EXPERT), BF16),
        compiler_params=_cparams(("arbitrary", "arbitrary")),
        name="moe_up",
    )(tile_expert, n_valid, tile_rows, hs, w_gu, w_gu)


def _moe_down_kernel(te_ref, nv_ref, tr_ref, a_ref, w_ref, o_ref, wb_ref):
    m = pl.program_id(1)
    new_w = jnp.logical_or(m == 0, te_ref[m] != te_ref[jnp.maximum(m - 1, 0)])

    @pl.when(new_w)
    def _():
        wb_ref[...] = w_ref[0, 0].astype(BF16)

    def chunk(rows):
        o_ref[rows, 0, :] = jnp.dot(a_ref[rows, :], wb_ref[...], preferred_element_type=F32)

    _for_occupied_chunks(tr_ref[m], o_ref, chunk)


def _moe_down(tile_expert, n_valid, tile_rows, act, w_down, layer):
    tn = D_MODEL
    return pl.pallas_call(
        _moe_down_kernel,
        grid_spec=pltpu.PrefetchScalarGridSpec(
            num_scalar_prefetch=3,
            grid=(D_MODEL // tn, MOE_NT),
            in_specs=[pl.BlockSpec((MOE_TILE, D_FF_EXPERT), lambda n, m, te, nv, tr: (jnp.minimum(m, nv[0] - 1), 0)),
                      pl.BlockSpec((1, 1, D_FF_EXPERT, tn), lambda n, m, te, nv, tr: (layer, te[m], 0, n))],
            out_specs=pl.BlockSpec((MOE_TILE, 1, tn), lambda n, m, te, nv, tr: (m, 0, n)),
            scratch_shapes=[pltpu.VMEM((D_FF_EXPERT, tn), BF16)]),
        out_shape=jax.ShapeDtypeStruct((MOE_SLOTS, 1, D_MODEL), F32),
        compiler_params=_cparams(("arbitrary", "arbitrary")),
        name="moe_down",
    )(tile_expert, n_valid, tile_rows, act, w_down)


def _combine_kernel(slot_ref, y_hbm, x_ref, rt_ref, pk_ref, pkn_ref, fin_ref, *rest, tok0, final):
    if final:
        out_ref, ybuf, sems = rest
    else:
        xo_ref, h_ref, ybuf, sems = rest
    tc = TC_COMBINE
    i = pl.program_id(0)
    n = pl.num_programs(0)
    cur = i % 2
    nxt = 1 - cur

    def issue_rows(tile, buf, r0):
        base = (tok0 + tile * tc) * TOP_K
        for r in range(COMBINE_GROUP):
            for k in range(TOP_K):
                pltpu.make_async_copy(y_hbm.at[pl.ds(slot_ref[base + (r0 + r) * TOP_K + k], 1)],
                                      ybuf.at[buf, k, pl.ds(r0 + r, 1)], sems.at[buf]).start()

    def wait_tile(buf):
        for k in range(TOP_K):
            pltpu.make_async_copy(y_hbm.at[pl.ds(0, tc)], ybuf.at[buf, k], sems.at[buf]).wait()

    def group_rows(g):
        return pl.multiple_of(g * COMBINE_GROUP, COMBINE_GROUP)

    @pl.when(i == 0)
    def _():
        def first(g, carry):
            issue_rows(0, 0, group_rows(g))
            return carry

        lax.fori_loop(0, tc // COMBINE_GROUP, first, 0)

    wait_tile(cur)
    nxt_tile = jnp.minimum(i + 1, n - 1)
    pk = pk_ref[0, 0]
    pkn = pkn_ref[0, 0]

    def group(g, carry):
        r0 = group_rows(g)
        issue_rows(nxt_tile, nxt, r0)
        rows = pl.ds(r0, COMBINE_GROUP)
        rt = rt_ref[rows, :]
        moe = rt[:, 2:3] * ybuf[cur, 0, rows, 0, :] + rt[:, 3:4] * ybuf[cur, 1, rows, 0, :]
        x_new = x_ref[rows, :] + pk[R_GATE_FFN:R_GATE_FFN + 1] * moe
        if final:
            out_ref[rows, :] = _epilogue(x_new, None, None, fin_ref, True)
        else:
            xo_ref[rows, :] = x_new
            h_ref[rows, :] = _epilogue(x_new, pkn, (R_G_MIX, R_SCALE_MIX, R_SHIFT_MIX), None, False).astype(BF16)
        return carry

    lax.fori_loop(0, tc // COMBINE_GROUP, group, 0, unroll=True)

    @pl.when(i == n - 1)
    def _():
        wait_tile(nxt)


def _combine(slot, y, x, route, pack, layer, next_layer, final_norm, tok0, n_rows, final):
    tc = TC_COMBINE
    b0 = tok0 // tc

    def tmap(m, s):
        return (b0 + m, 0)

    def pmap(l):
        return lambda m, s: (l, (tok0 + m * tc) // GROUP_TOKENS, 0, 0)

    if final:
        out_specs = pl.BlockSpec((tc, D_MODEL), lambda m, s: (m, 0))
        out_shape = jax.ShapeDtypeStruct((n_rows, D_MODEL), F32)
    else:
        out_specs = [pl.BlockSpec((tc, D_MODEL), lambda m, s: (m, 0)),
                     pl.BlockSpec((tc, D_MODEL), lambda m, s: (m, 0))]
        out_shape = [jax.ShapeDtypeStruct((n_rows, D_MODEL), F32),
                     jax.ShapeDtypeStruct((n_rows, D_MODEL), BF16)]
    kern = functools.partial(_combine_kernel, tok0=tok0, final=final)
    return pl.pallas_call(
        kern,
        grid_spec=pltpu.PrefetchScalarGridSpec(
            num_scalar_prefetch=1,
            grid=(n_rows // tc,),
            in_specs=[pl.BlockSpec(memory_space=pl.ANY),
                      pl.BlockSpec((tc, D_MODEL), tmap),
                      pl.BlockSpec((tc, LANES), tmap),
                      pl.BlockSpec((1, 1, 8, D_MODEL), pmap(layer)),
                      pl.BlockSpec((1, 1, 8, D_MODEL), pmap(next_layer)),
                      pl.BlockSpec((1, D_MODEL), lambda m, s: (0, 0))],
            out_specs=out_specs,
            scratch_shapes=[pltpu.VMEM((2, TOP_K, tc, 1, D_MODEL), F32), pltpu.SemaphoreType.DMA((2,))]),
        out_shape=out_shape,
        compiler_params=_cparams(("arbitrary",)),
        name="moe_combine",
    )(slot, y, x, route, pack, pack, final_norm.reshape(1, D_MODEL))


def _routing_tables(route):
    e_flat = route[:, :TOP_K].astype(I32).reshape(-1)
    onehot = (e_flat[:, None] == jnp.arange(N_EXPERTS, dtype=I32)[None, :]).astype(I32)
    csum = jnp.cumsum(onehot, axis=0)
    rank = jnp.sum(onehot * (csum - 1), axis=1)
    count = csum[-1]
    padded = ((count + MOE_TILE - 1) // MOE_TILE) * MOE_TILE
    pend = jnp.cumsum(padded)
    poff = pend - padded
    slot = jnp.sum(onehot * poff[None, :], axis=1) + rank
    n_valid = pend[-1] // MOE_TILE
    tile_start = jnp.arange(MOE_NT, dtype=I32) * MOE_TILE
    te_raw = jnp.minimum(jnp.sum((tile_start[:, None] >= pend[None, :]).astype(I32), axis=1), N_EXPERTS - 1)
    last_e = jnp.max(jnp.where(tile_start < pend[-1], te_raw, 0))
    tile_expert = jnp.minimum(te_raw, last_e)
    group_rows_end = jnp.sum((te_raw[:, None] == jnp.arange(N_EXPERTS, dtype=I32)[None, :]) * (poff + count)[None, :],
                             axis=1)
    tile_rows = jnp.where(tile_start < pend[-1], jnp.clip(group_rows_end - tile_start, 0, MOE_TILE), 0)
    return (slot.astype(I32), pend.astype(I32), tile_expert.astype(I32), n_valid.astype(I32).reshape(1),
            tile_rows.astype(I32))


def _rope_tables(dh):
    nf = dh // 4
    t = jnp.arange(DEC_SEQ)
    freqs = ROPE_BASE ** (-jnp.arange(nf, dtype=F32) / nf)
    row = (t // GRID_W).astype(F32)
    col = (t % GRID_W).astype(F32)
    ang = jnp.concatenate([row[:, None] * freqs, col[:, None] * freqs], axis=-1)
    return jnp.cos(ang), jnp.sin(ang)


def kernel(x_prompt, x_sample, cache_k, cache_v, state_fwd, state_bwd, c, c_ctx, norm_mix, norm_ffn, w_mod, b_mod,
           w_qkv, w_attn_o, attn_sink, w_ret_in, ret_decay, w_ret_out, w_ffn_gu, w_ffn_down, w_router, w_exp_gu,
           w_exp_down, final_norm):
    x = (x_prompt.reshape(N_PROMPT, D_MODEL), x_sample.reshape(N_SAMPLE, D_MODEL))

    cond8 = jnp.concatenate([c_ctx[None, :], c, jnp.zeros((8 - 1 - DEC_BATCH, D_MODEL), F32)], axis=0)
    mods = _modulations(cond8, w_mod, b_mod)
    m3 = mods[:, :N_GROUPS].reshape(DEPTH, N_GROUPS, 6, D_MODEL)
    pack = jnp.concatenate([
        m3,
        jnp.broadcast_to(norm_mix[:, None, None, :], (DEPTH, N_GROUPS, 1, D_MODEL)),
        jnp.broadcast_to(norm_ffn[:, None, None, :], (DEPTH, N_GROUPS, 1, D_MODEL))], axis=2)

    cos_a, sin_a = _rope_tables(HEAD_DIM)
    cos_attn = jnp.tile(cos_a, (1, LANES // (HEAD_DIM // 2)))
    sin_attn = jnp.tile(jnp.concatenate([-sin_a, sin_a], axis=1), (1, LANES // HEAD_DIM))
    cos_ret, sin_ret = _rope_tables(RET_DK)

    kvw = N_KV_HEADS * HEAD_DIM
    cache_k4 = cache_k.reshape(DEC_BATCH, N_EVEN, PAST_LEN, kvw)
    cache_v4 = cache_v.reshape(DEC_BATCH, N_EVEN, PAST_LEN, kvw)
    sink_flat = attn_sink.reshape(-1)
    log_gamma = jax.nn.log_sigmoid(ret_decay.astype(F32)).reshape(-1)
    w_router_pad = jnp.pad(w_router, ((0, 0), (0, 0), (0, LANES - N_EXPERTS)))
    w_router_hi = w_router_pad.astype(BF16)
    w_router_lo = (w_router_pad - w_router_hi.astype(F32)).astype(BF16)

    mix_rows = (R_G_MIX, R_SCALE_MIX, R_SHIFT_MIX)
    ffn_rows = (R_G_FFN, R_SCALE_FFN, R_SHIFT_FFN)

    new_k, new_v = [], []
    new_sf = new_sb = None
    h = _pre(x, pack)
    y_prompt = y_sample = None
    for i in range(DEPTH):
        j = i // 2
        if i % 2 == 0:
            q = _attn_proj(h, w_qkv, j, cos_attn, sin_attn, col0=0, ncols=N_HEADS * HEAD_DIM,
                           rope_cols=N_HEADS * HEAD_DIM, scale=HEAD_DIM ** -0.5 * LOG2E, out_dtype=BF16)
            kv = _attn_proj(h, w_qkv, j, cos_attn, sin_attn, col0=N_HEADS * HEAD_DIM, ncols=2 * kvw,
                            rope_cols=kvw, scale=1.0, out_dtype=F32)
            new_k.append(kv[:N_PROMPT, :kvw].reshape(BATCH, SEQ, N_KV_HEADS, HEAD_DIM))
            new_v.append(kv[:N_PROMPT, kvw:].reshape(BATCH, SEQ, N_KV_HEADS, HEAD_DIM))
            o_ctx = _ctx_attention(q, kv, sink_flat, j)
            o_lat = _lat_attention(q, kv, cache_k4, cache_v4, sink_flat, j)
            x, h = _down([o_ctx, o_lat], w_attn_o, j, x, pack, i, R_GATE_MIX, i, ffn_rows)
            act = _swiglu_up(h, w_ffn_gu, j)
            x, h = _down([act], w_ffn_down, j, x, pack, i, R_GATE_FFN, i + 1, mix_rows)
        else:
            proj = _ret_proj(h, w_ret_in, j, cos_ret, sin_ret)
            part, new_sf = _retention_pass(proj, state_fwd, log_gamma, j, False, states=new_sf)
            o_ret, new_sb = _retention_pass(proj, state_bwd, log_gamma, j, True, partial=part, states=new_sb)
            x, h, hf, route = _down([o_ret], w_ret_out, j, x, pack, i, R_GATE_MIX, i, ffn_rows,
                                    w_router=(w_router_hi, w_router_lo), router_layer=j)
            slot, pend, tile_expert, n_valid, tile_rows = _routing_tables(route)
            hs = _dispatch_rows(slot, pend, hf)
            act = _moe_up(tile_expert, n_valid, tile_rows, hs, w_exp_gu, j)
            ys = _moe_down(tile_expert, n_valid, tile_rows, act, w_exp_down, j)
            if i == DEPTH - 1:
                y_prompt = _combine(slot, ys, x, route, pack, i, i, final_norm, 0, N_PROMPT, True)
                y_sample = _combine(slot, ys, x, route, pack, i, i, final_norm, N_PROMPT, N_SAMPLE, True)
            else:
                x, h = _combine(slot, ys, x, route, pack, i, i + 1, final_norm, 0, N_TOK, False)

    return (y_prompt.reshape(BATCH, SEQ, D_MODEL), y_sample.reshape(DEC_BATCH, DEC_SEQ, D_MODEL),
            jnp.stack(new_k, axis=1), jnp.stack(new_v, axis=1), new_sf, new_sb)
```

```python
import functools

import jax
import jax.numpy as jnp
from jax import lax
from jax.experimental import pallas as pl
from jax.experimental.pallas import tpu as pltpu

F32 = jnp.float32
BF16 = jnp.bfloat16
I32 = jnp.int32

D_MODEL = 1024
BATCH = 16
SEQ = 256
DEPTH = 4
DEC_BATCH = 2
DEC_SEQ = 4096
PAST_LEN = 256
GRID_W = 64
BLOCK = 128
EPS = 1e-6
N_HEADS = 16
N_KV_HEADS = 4
HEAD_DIM = 64
GROUP = N_HEADS // N_KV_HEADS
WINDOW = 128
ROPE_BASE = 10000.0
RET_HEADS = 4
RET_DK = 256
RET_DV = 512
RET_HK = RET_HEADS * RET_DK
RET_HV = RET_HEADS * RET_DV
D_FF = 2816
N_EXPERTS = 8
TOP_K = 2
D_FF_EXPERT = 3584
N_EVEN = 2
N_ODD = 2
NEG = -1e30
LOG2E = 1.4426950408889634

GROUP_TOKENS = 4096
N_PROMPT = BATCH * SEQ
N_SAMPLE = DEC_BATCH * DEC_SEQ
N_TOK = N_PROMPT + N_SAMPLE
N_GROUPS = N_TOK // GROUP_TOKENS

R_SHIFT_MIX, R_SCALE_MIX, R_GATE_MIX, R_SHIFT_FFN, R_SCALE_FFN, R_GATE_FFN, R_G_MIX, R_G_FFN = range(8)

VMEM_LIMIT_BYTES = 56 * 1024 * 1024
LANES = 128

TM = 2048
UP_CHUNK = 512
TM_DOWN = 512
DOWN_CHUNK = 128
MOE_TILE = 512
MOE_CHUNK = 256
MOE_SLOTS = TOP_K * N_TOK + N_EXPERTS * MOE_TILE
MOE_NT = MOE_SLOTS // MOE_TILE
TC_COMBINE = 256
COMBINE_GROUP = 16
DISPATCH_ROWS = 256


def _cparams(sem):
    return pltpu.CompilerParams(dimension_semantics=sem, vmem_limit_bytes=VMEM_LIMIT_BYTES)


def _normmod(x, g, scale, shift):
    ms = jnp.mean(x * x, axis=-1, keepdims=True)
    return (x * lax.rsqrt(ms + EPS) * g) * (1.0 + scale) + shift


def _mod_kernel(cond_ref, w_ref, b_ref, o_ref):
    c = cond_ref[...]
    s = c * jax.nn.sigmoid(c)
    o_ref[0] = jnp.dot(s.astype(BF16), w_ref[0].astype(BF16), preferred_element_type=F32) + b_ref[0]


def _modulations(cond8, w_mod, b_mod):
    tn = 2048
    n6 = 6 * D_MODEL
    return pl.pallas_call(
        _mod_kernel,
        grid=(DEPTH, n6 // tn),
        in_specs=[pl.BlockSpec((8, D_MODEL), lambda l, n: (0, 0)),
                  pl.BlockSpec((1, D_MODEL, tn), lambda l, n: (l, 0, n)),
                  pl.BlockSpec((1, 1, tn), lambda l, n: (l, 0, n))],
        out_specs=pl.BlockSpec((1, 8, tn), lambda l, n: (l, 0, n)),
        out_shape=jax.ShapeDtypeStruct((DEPTH, 8, n6), F32),
        compiler_params=_cparams(("arbitrary", "arbitrary")),
        name="modulations",
    )(cond8, w_mod, b_mod.reshape(DEPTH, 1, n6))


def _split_specs(tm, width):
    fst = N_PROMPT // tm
    return [pl.BlockSpec((tm, width), lambda m: (jnp.minimum(m, fst - 1), 0)),
            pl.BlockSpec((tm, width), lambda m: (jnp.maximum(m - fst, 0), 0))]


def _pre_kernel(xp_ref, xs_ref, pk_ref, h_ref, *, first_sample_tile):
    def norm(x_ref):
        pk = pk_ref[0, 0]
        h = _normmod(x_ref[...], pk[R_G_MIX:R_G_MIX + 1], pk[R_SCALE_MIX:R_SCALE_MIX + 1],
                     pk[R_SHIFT_MIX:R_SHIFT_MIX + 1])
        h_ref[...] = h.astype(BF16)

    m = pl.program_id(0)
    pl.when(m < first_sample_tile)(lambda: norm(xp_ref))
    pl.when(m >= first_sample_tile)(lambda: norm(xs_ref))


def _pre(x_pair, pack):
    tm = TM
    return pl.pallas_call(
        functools.partial(_pre_kernel, first_sample_tile=N_PROMPT // tm),
        grid=(N_TOK // tm,),
        in_specs=_split_specs(tm, D_MODEL) + [
            pl.BlockSpec((1, 1, 8, D_MODEL), lambda m: (0, (m * tm) // GROUP_TOKENS, 0, 0))],
        out_specs=pl.BlockSpec((tm, D_MODEL), lambda m: (m, 0)),
        out_shape=jax.ShapeDtypeStruct((N_TOK, D_MODEL), BF16),
        compiler_params=_cparams(("arbitrary",)),
        name="pre_norm",
    )(*x_pair, pack)


def _row_chunks(n_rows):
    return [pl.ds(c * UP_CHUNK, UP_CHUNK) for c in range(n_rows // UP_CHUNK)]


def _table_map(tm):
    first_sample_tile = N_PROMPT // tm
    tiles_per_seq = DEC_SEQ // tm
    return lambda m: (jnp.maximum(m - first_sample_tile, 0) % tiles_per_seq, 0)


def _rope64(y, cos, sin_signed):
    width = y.shape[-1]
    lane = lax.broadcasted_iota(I32, y.shape, 1)
    first = (lane % HEAD_DIM) < (HEAD_DIM // 2)
    swapped = jnp.where(first, pltpu.roll(y, width - HEAD_DIM // 2, 1), pltpu.roll(y, HEAD_DIM // 2, 1))
    reps = width // LANES
    c = jnp.concatenate([cos] * reps, axis=1) if reps > 1 else cos
    s = jnp.concatenate([sin_signed] * reps, axis=1) if reps > 1 else sin_signed
    return y * c + swapped * s


def _attn_proj_kernel(h_ref, w_ref, cos_ref, sin_ref, o_ref, wb_ref, *, scale, rope_cols, first_sample_tile):
    m = pl.program_id(0)

    @pl.when(m == 0)
    def _():
        wb_ref[...] = w_ref[0].astype(BF16)

    def project(rope):
        for rows in _row_chunks(h_ref.shape[0]):
            y = jnp.dot(h_ref[rows, :], wb_ref[...], preferred_element_type=F32)
            if scale != 1.0:
                y = y * scale
            if rope:
                roped = _rope64(y[:, :rope_cols], cos_ref[rows, :], sin_ref[rows, :])
                y = roped if rope_cols == y.shape[1] else jnp.concatenate([roped, y[:, rope_cols:]], axis=1)
            o_ref[rows, :] = y.astype(o_ref.dtype)

    pl.when(m >= first_sample_tile)(lambda: project(True))
    pl.when(m < first_sample_tile)(lambda: project(False))


def _attn_proj(h, w_qkv, layer, cos, sin, *, col0, ncols, rope_cols, scale, out_dtype):
    tm = TM
    kern = functools.partial(_attn_proj_kernel, scale=scale, rope_cols=rope_cols, first_sample_tile=N_PROMPT // tm)
    return pl.pallas_call(
        kern,
        grid=(N_TOK // tm,),
        in_specs=[pl.BlockSpec((tm, D_MODEL), lambda m: (m, 0)),
                  pl.BlockSpec((1, D_MODEL, ncols), lambda m: (layer, 0, col0 // ncols)),
                  pl.BlockSpec((tm, LANES), _table_map(tm)),
                  pl.BlockSpec((tm, LANES), _table_map(tm))],
        out_specs=pl.BlockSpec((tm, ncols), lambda m: (m, 0)),
        out_shape=jax.ShapeDtypeStruct((N_TOK, ncols), out_dtype),
        scratch_shapes=[pltpu.VMEM((D_MODEL, ncols), BF16)],
        compiler_params=_cparams(("arbitrary",)),
        name="attn_proj",
    )(h, w_qkv, cos, sin)


RET_TN = RET_HK


def _ret_proj_kernel(h_ref, w_ref, cos_ref, sin_ref, o_ref, wb_ref, *, first_sample_tile):
    n = pl.program_id(0)
    m = pl.program_id(1)

    @pl.when(m == 0)
    def _():
        wb_ref[...] = w_ref[0].astype(BF16)

    k_scale = jnp.where(n == 1, RET_DK ** -0.5, 1.0).astype(F32)
    half = RET_DK // 2

    def project(rope, gate=False):
        for rows in _row_chunks(h_ref.shape[0]):
            y = jnp.dot(h_ref[rows, :], wb_ref[...], preferred_element_type=F32) * k_scale
            if gate:
                y = y * jax.nn.sigmoid(y)
            if rope:
                c = cos_ref[rows, :]
                s = sin_ref[rows, :]
                parts = []
                for hh in range(RET_TN // RET_DK):
                    x1 = y[:, hh * RET_DK:hh * RET_DK + half]
                    x2 = y[:, hh * RET_DK + half:(hh + 1) * RET_DK]
                    parts.append(x1 * c - x2 * s)
                    parts.append(x1 * s + x2 * c)
                y = jnp.concatenate(parts, axis=1)
            o_ref[rows, :] = y.astype(BF16)

    gate_tile0 = (2 * RET_HK + RET_HV) // RET_TN
    do_rope = jnp.logical_and(m >= first_sample_tile, n < 2)
    is_gate = n >= gate_tile0
    pl.when(do_rope)(lambda: project(True))
    pl.when(is_gate)(lambda: project(False, gate=True))
    pl.when(jnp.logical_not(jnp.logical_or(do_rope, is_gate)))(lambda: project(False))


def _ret_proj(h, w_ret_in, layer, cos, sin):
    tm = TM
    ncols = 2 * RET_HK + 3 * RET_HV
    tmap = _table_map(tm)
    kern = functools.partial(_ret_proj_kernel, first_sample_tile=N_PROMPT // tm)
    return pl.pallas_call(
        kern,
        grid=(ncols // RET_TN, N_TOK // tm),
        in_specs=[pl.BlockSpec((tm, D_MODEL), lambda n, m: (m, 0)),
                  pl.BlockSpec((1, D_MODEL, RET_TN), lambda n, m: (layer, 0, n)),
                  pl.BlockSpec((tm, LANES), lambda n, m: tmap(m)),
                  pl.BlockSpec((tm, LANES), lambda n, m: tmap(m))],
        out_specs=pl.BlockSpec((tm, RET_TN), lambda n, m: (m, n)),
        out_shape=jax.ShapeDtypeStruct((N_TOK, ncols), BF16),
        scratch_shapes=[pltpu.VMEM((D_MODEL, RET_TN), BF16)],
        compiler_params=_cparams(("arbitrary", "arbitrary")),
        name="ret_proj",
    )(h, w_ret_in, cos, sin)


def _swiglu_kernel(h_ref, wg_ref, wu_ref, o_ref, wgb_ref, wub_ref):
    m = pl.program_id(1)

    @pl.when(m == 0)
    def _():
        wgb_ref[...] = wg_ref[0].astype(BF16)
        wub_ref[...] = wu_ref[0].astype(BF16)

    for rows in _row_chunks(h_ref.shape[0]):
        h = h_ref[rows, :]
        g = jnp.dot(h, wgb_ref[...], preferred_element_type=F32)
        u = jnp.dot(h, wub_ref[...], preferred_element_type=F32)
        o_ref[rows, :] = (g * jax.nn.sigmoid(g) * u).astype(BF16)


def _swiglu_up(h, w_gu, layer):
    tm, tf = 1024, D_FF // 2
    nf = D_FF // tf
    return pl.pallas_call(
        _swiglu_kernel,
        grid=(nf, N_TOK // tm),
        in_specs=[pl.BlockSpec((tm, D_MODEL), lambda f, m: (m, 0)),
                  pl.BlockSpec((1, D_MODEL, tf), lambda f, m: (layer, 0, f)),
                  pl.BlockSpec((1, D_MODEL, tf), lambda f, m: (layer, 0, nf + f))],
        out_specs=pl.BlockSpec((tm, tf), lambda f, m: (m, f)),
        out_shape=jax.ShapeDtypeStruct((N_TOK, D_FF), BF16),
        scratch_shapes=[pltpu.VMEM((D_MODEL, tf), BF16), pltpu.VMEM((D_MODEL, tf), BF16)],
        compiler_params=_cparams(("arbitrary", "arbitrary")),
        name="swiglu_up",
    )(h, w_gu, w_gu)


def _route(logits):
    lane = lax.broadcasted_iota(I32, logits.shape, 1).astype(F32)
    lg = jnp.where(lane < N_EXPERTS, logits, -jnp.inf)
    m1 = jnp.max(lg, axis=-1, keepdims=True)
    i1 = jnp.min(jnp.where(lg == m1, lane, float(LANES)), axis=-1, keepdims=True)
    lg2 = jnp.where(lane == i1, -jnp.inf, lg)
    m2 = jnp.max(lg2, axis=-1, keepdims=True)
    i2 = jnp.min(jnp.where(lg2 == m2, lane, float(LANES)), axis=-1, keepdims=True)
    e2 = jnp.exp(m2 - m1)
    p1 = 1.0 / (1.0 + e2)
    p2 = e2 / (1.0 + e2)
    out = jnp.where(lane == 0, i1, 0.0)
    out = jnp.where(lane == 1, i2, out)
    out = jnp.where(lane == 2, p1, out)
    out = jnp.where(lane == 3, p2, out)
    return out


def _epilogue(x_new, pk_next, rows, fin_ref, final):
    if final:
        ms = jnp.mean(x_new * x_new, axis=-1, keepdims=True)
        return x_new * lax.rsqrt(ms + EPS) * fin_ref[...]
    g_row, sc_row, sh_row = rows
    return _normmod(x_new, pk_next[g_row:g_row + 1], pk_next[sc_row:sc_row + 1], pk_next[sh_row:sh_row + 1])


def _down_kernel(*refs, n_a, n_x, first_sample_tile, gate_row, next_rows, router):
    a_refs = refs[:n_a]
    w_ref = refs[n_a]
    x_refs = refs[n_a + 1:n_a + 1 + n_x]
    pk_ref, pkn_ref = refs[n_a + 1 + n_x:n_a + 3 + n_x]
    pos = n_a + 3 + n_x
    if router:
        wrh_ref, wrl_ref = refs[pos:pos + 2]
        pos += 2
        xo_ref, h_ref, hf_ref, rt_ref, wb_ref = refs[pos:pos + 5]
    else:
        xo_ref, h_ref, wb_ref = refs[pos:pos + 3]
    m = pl.program_id(0)

    @pl.when(m == 0)
    def _():
        wb_ref[...] = w_ref[0].astype(BF16)

    def finish(a_ref, x_ref):
        pk = pk_ref[0, 0]
        pkn = pkn_ref[0, 0]
        for c in range(a_ref.shape[0] // DOWN_CHUNK):
            rows = pl.ds(c * DOWN_CHUNK, DOWN_CHUNK)
            y = jnp.dot(a_ref[rows, :], wb_ref[...], preferred_element_type=F32)
            x_new = x_ref[rows, :] + pk[gate_row:gate_row + 1] * y
            xo_ref[rows, :] = x_new
            hn = _epilogue(x_new, pkn, next_rows, None, False)
            hi = hn.astype(BF16)
            h_ref[rows, :] = hi
            if router:
                hf_ref[rows, :] = hn
        if router:
            hi = h_ref[...]
            lo = (hf_ref[...] - hi.astype(F32)).astype(BF16)
            logits = (jnp.dot(hi, wrh_ref[0], preferred_element_type=F32)
                      + jnp.dot(lo, wrh_ref[0], preferred_element_type=F32)
                      + jnp.dot(hi, wrl_ref[0], preferred_element_type=F32))
            rt_ref[...] = _route(logits)

    if n_a == 1 and n_x == 1:
        finish(a_refs[0], x_refs[0])
    else:
        pl.when(m < first_sample_tile)(lambda: finish(a_refs[0], x_refs[0]))
        pl.when(m >= first_sample_tile)(lambda: finish(a_refs[-1], x_refs[-1]))


def _down(a_list, w, layer, x, pack, pack_layer, gate_row, next_layer, next_rows, w_router=None, router_layer=0):
    tm = TM_DOWN
    kd = w.shape[1]
    n_a = len(a_list)
    fst = N_PROMPT // tm
    router = w_router is not None
    x_list = list(x) if isinstance(x, (list, tuple)) else [x]
    n_x = len(x_list)
    a_specs = [pl.BlockSpec((tm, kd), lambda m: (m, 0))] if n_a == 1 else _split_specs(tm, kd)
    x_specs = [pl.BlockSpec((tm, D_MODEL), lambda m: (m, 0))] if n_x == 1 else _split_specs(tm, D_MODEL)
    in_specs = a_specs + [pl.BlockSpec((1, kd, D_MODEL), lambda m: (layer, 0, 0))] + x_specs + [
        pl.BlockSpec((1, 1, 8, D_MODEL), lambda m: (pack_layer, (m * tm) // GROUP_TOKENS, 0, 0)),
        pl.BlockSpec((1, 1, 8, D_MODEL), lambda m: (next_layer, (m * tm) // GROUP_TOKENS, 0, 0)),
    ]
    args = list(a_list) + [w] + list(x_list) + [pack, pack]
    out_specs = [pl.BlockSpec((tm, D_MODEL), lambda m: (m, 0)),
                 pl.BlockSpec((tm, D_MODEL), lambda m: (m, 0))]
    out_shape = [jax.ShapeDtypeStruct((N_TOK, D_MODEL), F32),
                 jax.ShapeDtypeStruct((N_TOK, D_MODEL), BF16)]
    if router:
        in_specs += [pl.BlockSpec((1, D_MODEL, LANES), lambda m: (router_layer, 0, 0)),
                     pl.BlockSpec((1, D_MODEL, LANES), lambda m: (router_layer, 0, 0))]
        args += list(w_router)
        out_specs += [pl.BlockSpec((tm, D_MODEL), lambda m: (m, 0)),
                      pl.BlockSpec((tm, LANES), lambda m: (m, 0))]
        out_shape += [jax.ShapeDtypeStruct((N_TOK, D_MODEL), F32),
                      jax.ShapeDtypeStruct((N_TOK, LANES), F32)]
    kern = functools.partial(_down_kernel, n_a=n_a, n_x=n_x, first_sample_tile=fst, gate_row=gate_row,
                             next_rows=next_rows, router=router)
    return pl.pallas_call(
        kern,
        grid=(N_TOK // tm,),
        in_specs=in_specs,
        out_specs=out_specs,
        out_shape=out_shape,
        scratch_shapes=[pltpu.VMEM((kd, D_MODEL), BF16)],
        compiler_params=_cparams(("arbitrary",)),
        name="down_proj",
    )(*args)


def _sink_row(sink_ref, layer, kvh, width):
    return jnp.concatenate(
        [jnp.full((1, width), sink_ref[layer * N_HEADS + kvh * GROUP + g] * LOG2E, F32) for g in range(GROUP)],
        axis=1)


def _group_queries(q_ref, kvh):
    return jnp.concatenate(
        [q_ref[:, (kvh * GROUP + g) * HEAD_DIM:(kvh * GROUP + g + 1) * HEAD_DIM] for g in range(GROUP)], axis=0)


_NT = (((1,), (1,)), ((), ()))


def _ctx_attn_kernel(sink_ref, q_ref, kv_ref, o_ref, *, layer):
    kvw = N_KV_HEADS * HEAD_DIM
    k_all = kv_ref[:, :kvw].astype(BF16)
    v_t = kv_ref[:, kvw:].T.astype(BF16)
    outs = []
    for kvh in range(N_KV_HEADS):
        hs = slice(kvh * HEAD_DIM, (kvh + 1) * HEAD_DIM)
        s_t = lax.dot_general(k_all[:, hs], _group_queries(q_ref, kvh), _NT, preferred_element_type=F32)
        sink = _sink_row(sink_ref, layer, kvh, SEQ)
        m = jnp.maximum(jnp.max(s_t, axis=0, keepdims=True), sink)
        p = jnp.exp2(s_t - m)
        denom = jnp.sum(p, axis=0, keepdims=True) + jnp.exp2(sink - m)
        o_t = jnp.dot(v_t[hs], p.astype(BF16), preferred_element_type=F32) / denom
        outs += [o_t[:, g * SEQ:(g + 1) * SEQ] for g in range(GROUP)]
    o_ref[...] = jnp.concatenate(outs, axis=0).T.astype(BF16)


def _ctx_attention(q, kv, sink_flat, layer):
    kern = functools.partial(_ctx_attn_kernel, layer=layer)
    return pl.pallas_call(
        kern,
        grid_spec=pltpu.PrefetchScalarGridSpec(
            num_scalar_prefetch=1,
            grid=(BATCH,),
            in_specs=[pl.BlockSpec((SEQ, D_MODEL), lambda b, s: (b, 0)),
                      pl.BlockSpec((SEQ, 2 * N_KV_HEADS * HEAD_DIM), lambda b, s: (b, 0))],
            out_specs=pl.BlockSpec((SEQ, D_MODEL), lambda b, s: (b, 0))),
        out_shape=jax.ShapeDtypeStruct((N_PROMPT, D_MODEL), BF16),
        compiler_params=_cparams(("arbitrary",)),
        name="ctx_attention",
    )(sink_flat, q, kv)


def _lat_attn_kernel(sink_ref, q_ref, kvp_ref, kvc_ref, kvn_ref, ck_ref, cv_ref, o_ref, kc_ref, vct_ref, *, layer):
    blk = pl.program_id(1)
    kvw = N_KV_HEADS * HEAD_DIM
    cols = GROUP * BLOCK
    n_lat = 3 * BLOCK

    @pl.when(blk == 0)
    def _():
        kc_ref[...] = ck_ref[0, 0].astype(BF16)
        vct_ref[...] = cv_ref[0, 0].T.astype(BF16)

    j = lax.broadcasted_iota(I32, (n_lat, cols), 0)
    r = lax.broadcasted_iota(I32, (n_lat, cols), 1) % BLOCK
    lo = jnp.maximum(r, BLOCK - BLOCK * blk)
    hi = jnp.minimum(r + 2 * WINDOW, DEC_SEQ + BLOCK - 1 - BLOCK * blk)
    lat_cap = jnp.where(jnp.logical_and(j >= lo, j <= hi), jnp.inf, NEG).astype(F32)
    k_lat = jnp.concatenate([kvp_ref[:, :kvw], kvc_ref[:, :kvw], kvn_ref[:, :kvw]], axis=0).astype(BF16)
    v_lat_t = jnp.concatenate([kvp_ref[:, kvw:].T, kvc_ref[:, kvw:].T, kvn_ref[:, kvw:].T], axis=1).astype(BF16)
    outs = []
    for kvh in range(N_KV_HEADS):
        hs = slice(kvh * HEAD_DIM, (kvh + 1) * HEAD_DIM)
        q4 = _group_queries(q_ref, kvh)
        s_ctx = lax.dot_general(kc_ref[:, hs], q4, _NT, preferred_element_type=F32)
        s_lat = lax.dot_general(k_lat[:, hs], q4, _NT, preferred_element_type=F32)
        s_lat = jnp.minimum(s_lat, lat_cap)
        sink = _sink_row(sink_ref, layer, kvh, BLOCK)
        m = jnp.maximum(jnp.maximum(jnp.max(s_ctx, axis=0, keepdims=True),
                                    jnp.max(s_lat, axis=0, keepdims=True)), sink)
        p_ctx = jnp.exp2(s_ctx - m)
        p_lat = jnp.exp2(s_lat - m)
        denom = (jnp.sum(p_ctx, axis=0, keepdims=True) + jnp.sum(p_lat, axis=0, keepdims=True)
                 + jnp.exp2(sink - m))
        o_t = (jnp.dot(vct_ref[hs, :], p_ctx.astype(BF16), preferred_element_type=F32)
               + jnp.dot(v_lat_t[hs], p_lat.astype(BF16), preferred_element_type=F32)) / denom
        outs += [o_t[:, g * BLOCK:(g + 1) * BLOCK] for g in range(GROUP)]
    o_ref[...] = jnp.concatenate(outs, axis=0).T.astype(BF16)


def _lat_attention(q, kv, cache_k4, cache_v4, sink_flat, layer):
    nb = DEC_SEQ // BLOCK
    base = N_PROMPT // BLOCK
    kvc = 2 * N_KV_HEADS * HEAD_DIM

    def row(b, i):
        return base + b * nb + i

    kern = functools.partial(_lat_attn_kernel, layer=layer)
    return pl.pallas_call(
        kern,
        grid_spec=pltpu.PrefetchScalarGridSpec(
            num_scalar_prefetch=1,
            grid=(DEC_BATCH, nb),
            in_specs=[pl.BlockSpec((BLOCK, D_MODEL), lambda b, i, s: (row(b, i), 0)),
                      pl.BlockSpec((BLOCK, kvc), lambda b, i, s: (row(b, jnp.maximum(i - 1, 0)), 0)),
                      pl.BlockSpec((BLOCK, kvc), lambda b, i, s: (row(b, i), 0)),
                      pl.BlockSpec((BLOCK, kvc), lambda b, i, s: (row(b, jnp.minimum(i + 1, nb - 1)), 0)),
                      pl.BlockSpec((1, 1, PAST_LEN, N_KV_HEADS * HEAD_DIM), lambda b, i, s: (b, layer, 0, 0)),
                      pl.BlockSpec((1, 1, PAST_LEN, N_KV_HEADS * HEAD_DIM), lambda b, i, s: (b, layer, 0, 0))],
            out_specs=pl.BlockSpec((BLOCK, D_MODEL), lambda b, i, s: (b * nb + i, 0)),
            scratch_shapes=[pltpu.VMEM((PAST_LEN, N_KV_HEADS * HEAD_DIM), BF16),
                            pltpu.VMEM((N_KV_HEADS * HEAD_DIM, PAST_LEN), BF16)]),
        out_shape=jax.ShapeDtypeStruct((N_SAMPLE, D_MODEL), BF16),
        compiler_params=_cparams(("arbitrary", "arbitrary")),
        name="lat_attention",
    )(sink_flat, q, kv, kv, kv, cache_k4, cache_v4)


RET_C = 256
RET_SUB = 1
RET_ROWS = RET_SUB * RET_C
RET_STEPS = N_TOK // RET_ROWS
PROMPT_STEPS = N_PROMPT // RET_ROWS
STEPS_PER_SAMPLE = DEC_SEQ // RET_ROWS
assert SEQ == RET_ROWS


def _ret_kernel(lg_ref, q_ref, k_ref, v_ref, gate_ref, s0_ref, *rest, backward, layer):
    part_ref = rest[0] if backward else None
    o_ref, sout_ref, state_ref = rest[-3:]
    step = pl.program_id(0)
    blk = (RET_STEPS - 1 - step) if backward else step
    in_prompt = blk < PROMPT_STEPS
    first_of_sample = (blk - PROMPT_STEPS) % STEPS_PER_SAMPLE == (STEPS_PER_SAMPLE - 1 if backward else 0)

    @pl.when(in_prompt)
    def _():
        state_ref[...] = jnp.zeros_like(state_ref)

    @pl.when(jnp.logical_and(jnp.logical_not(in_prompt), first_of_sample))
    def _():
        state_ref[...] = s0_ref[0, 0]

    ii = lax.broadcasted_iota(I32, (RET_C, RET_C), 0).astype(F32)
    jj = lax.broadcasted_iota(I32, (RET_C, RET_C), 1).astype(F32)
    dist = (jj - ii) if backward else (ii - jj)
    ti = lax.broadcasted_iota(I32, (RET_C, 1), 0).astype(F32)
    q_pow = (RET_C - ti) if backward else (ti + 1.0)
    k_pow = ti if backward else (RET_C - 1.0 - ti)

    for h in range(RET_HEADS):
        lg = lg_ref[(layer * 2 + (1 if backward else 0)) * RET_HEADS + h]
        intra = jnp.where(dist >= 0, jnp.exp(lg * jnp.maximum(dist, 0.0)), 0.0)
        q_dec = jnp.exp(lg * q_pow)
        k_dec = jnp.exp(lg * k_pow)
        c_dec = jnp.exp(lg * RET_C)
        for sub in (reversed(range(RET_SUB)) if backward else range(RET_SUB)):
            rows = pl.ds(sub * RET_C, RET_C)
            q = q_ref[rows, h * RET_DK:(h + 1) * RET_DK]
            k = k_ref[rows, h * RET_DK:(h + 1) * RET_DK]
            v = v_ref[rows, h * RET_DV:(h + 1) * RET_DV]
            s = state_ref[h]
            a = lax.dot_general(q, k, (((1,), (1,)), ((), ())), preferred_element_type=F32) * intra
            o = (jnp.dot(a.astype(BF16), v, preferred_element_type=F32)
                 + q_dec * jnp.dot(q, s.astype(BF16), preferred_element_type=F32))
            kd = (k.astype(F32) * k_dec).astype(BF16)
            state_ref[h] = c_dec * s + lax.dot_general(kd, v, (((0,), (0,)), ((), ())), preferred_element_type=F32)
            gate = gate_ref[rows, h * RET_DV:(h + 1) * RET_DV].astype(F32)
            on = o * lax.rsqrt(jnp.mean(o * o, axis=-1, keepdims=True) + EPS)
            res = on * gate
            if backward:
                res = res + part_ref[rows, h * RET_DV:(h + 1) * RET_DV].astype(F32)
            o_ref[rows, h * RET_DV:(h + 1) * RET_DV] = res.astype(BF16)

    @pl.when(in_prompt)
    def _():
        if sout_ref.shape[1] == 1:
            sout_ref[0, 0] = state_ref[...]
        else:
            for l in range(sout_ref.shape[1]):
                sout_ref[0, l] = state_ref[...] if l == layer else jnp.zeros_like(state_ref)


def _retention_pass(proj, s0, log_gamma_flat, layer, backward, partial=None, states=None):
    def blk(i):
        return (RET_STEPS - 1 - i) if backward else i

    def s0_map(i, lg):
        return (jnp.clip((blk(i) - PROMPT_STEPS) // STEPS_PER_SAMPLE, 0, DEC_BATCH - 1), layer, 0, 0, 0)

    sout_layers = N_ODD if states is None else 1

    def sout_map(i, lg):
        return (jnp.minimum(blk(i), BATCH - 1), 0 if states is None else layer, 0, 0, 0)

    gate_block = 3 if backward else 2
    in_specs = [pl.BlockSpec((RET_ROWS, RET_HK), lambda i, lg: (blk(i), 0)),
                pl.BlockSpec((RET_ROWS, RET_HK), lambda i, lg: (blk(i), 1)),
                pl.BlockSpec((RET_ROWS, RET_HV), lambda i, lg: (blk(i), 1)),
                pl.BlockSpec((RET_ROWS, RET_HV), lambda i, lg: (blk(i), gate_block)),
                pl.BlockSpec((1, 1, RET_HEADS, RET_DK, RET_DV), s0_map)]
    args = [proj, proj, proj, proj, s0]
    if backward:
        in_specs.append(pl.BlockSpec((RET_ROWS, RET_HV), lambda i, lg: (blk(i), 0)))
        args.append(partial)
    aliases = {}
    if states is not None:
        in_specs.append(pl.BlockSpec(memory_space=pl.ANY))
        args.append(states)
        aliases = {len(args): 1}
    kern = functools.partial(_ret_kernel, backward=backward, layer=layer)
    return pl.pallas_call(
        kern,
        grid_spec=pltpu.PrefetchScalarGridSpec(
            num_scalar_prefetch=1,
            grid=(RET_STEPS,),
            in_specs=in_specs,
            out_specs=[pl.BlockSpec((RET_ROWS, RET_HV), lambda i, lg: (blk(i), 0)),
                       pl.BlockSpec((1, sout_layers, RET_HEADS, RET_DK, RET_DV), sout_map)],
            scratch_shapes=[pltpu.VMEM((RET_HEADS, RET_DK, RET_DV), F32)]),
        out_shape=[jax.ShapeDtypeStruct((N_TOK, RET_HV), BF16),
                   jax.ShapeDtypeStruct((BATCH, N_ODD, RET_HEADS, RET_DK, RET_DV), F32)],
        input_output_aliases=aliases,
        compiler_params=_cparams(("arbitrary",)),
        name="retention_bwd" if backward else "retention_fwd",
    )(log_gamma_flat, *args)


def _dispatch_kernel(slot_ref, pend_ref, h_ref, o_hbm, zbuf, sem):
    i = pl.program_id(0)

    @pl.when(i == 0)
    def _():
        zbuf[...] = jnp.zeros_like(zbuf)

        def tile_fill(start):
            return pltpu.make_async_copy(zbuf, o_hbm.at[pl.ds(pl.multiple_of(start, MOE_TILE), MOE_TILE)], sem)

        def nonempty(e):
            return pend_ref[e] > (pend_ref[e - 1] if e else 0)

        def start_unused(t, carry):
            tile_fill(t * MOE_TILE).start()
            return carry

        def wait_unused(t, carry):
            tile_fill(t * MOE_TILE).wait()
            return carry

        first_unused = pend_ref[N_EXPERTS - 1] // MOE_TILE
        for e in range(N_EXPERTS):
            pl.when(nonempty(e))(lambda e=e: tile_fill(pend_ref[e] - MOE_TILE).start())
        lax.fori_loop(first_unused, MOE_NT, start_unused, 0)
        for e in range(N_EXPERTS):
            pl.when(nonempty(e))(lambda e=e: tile_fill(pend_ref[e] - MOE_TILE).wait())
        lax.fori_loop(first_unused, MOE_NT, wait_unused, 0)

    base = i * DISPATCH_ROWS * TOP_K

    def start(r, carry):
        for k in range(TOP_K):
            pltpu.make_async_copy(h_ref.at[pl.ds(r, 1)], o_hbm.at[pl.ds(slot_ref[base + r * TOP_K + k], 1)],
                                  sem).start()
        return carry

    lax.fori_loop(0, DISPATCH_ROWS, start, 0, unroll=8)
    for k in range(TOP_K):
        pltpu.make_async_copy(h_ref, o_hbm.at[pl.ds(0, DISPATCH_ROWS)], sem).wait()


def _dispatch_rows(slot, pend, h):
    return pl.pallas_call(
        _dispatch_kernel,
        grid_spec=pltpu.PrefetchScalarGridSpec(
            num_scalar_prefetch=2,
            grid=(N_TOK // DISPATCH_ROWS,),
            in_specs=[pl.BlockSpec((DISPATCH_ROWS, D_MODEL), lambda i, s, p: (i, 0))],
            out_specs=pl.BlockSpec(memory_space=pl.ANY),
            scratch_shapes=[pltpu.VMEM((MOE_TILE, D_MODEL), F32), pltpu.SemaphoreType.DMA(())]),
        out_shape=jax.ShapeDtypeStruct((MOE_SLOTS, D_MODEL), F32),
        compiler_params=_cparams(("arbitrary",)),
        name="moe_dispatch",
    )(slot, pend, h)


def _for_occupied_chunks(n_rows, o_ref, chunk):
    n_chunks = MOE_TILE // MOE_CHUNK
    occupied = (n_rows + MOE_CHUNK - 1) // MOE_CHUNK

    def path(k):
        for c in range(n_chunks):
            rows = pl.ds(c * MOE_CHUNK, MOE_CHUNK)
            if c < k:
                chunk(rows)
            else:
                o_ref[rows, :] = jnp.zeros((MOE_CHUNK, o_ref.shape[1]), o_ref.dtype)

    for k in range(n_chunks + 1):
        pl.when(occupied == k)(functools.partial(path, k))


def _moe_up_kernel(te_ref, nv_ref, tr_ref, h_ref, wg_ref, wu_ref, o_ref, wgb_ref, wub_ref):
    m = pl.program_id(1)
    new_w = jnp.logical_or(m == 0, te_ref[m] != te_ref[jnp.maximum(m - 1, 0)])

    @pl.when(new_w)
    def _():
        wgb_ref[...] = wg_ref[0, 0].astype(BF16)
        wub_ref[...] = wu_ref[0, 0].astype(BF16)

    def chunk(rows):
        h = h_ref[rows, :].astype(BF16)
        g = jnp.dot(h, wgb_ref[...], preferred_element_type=F32)
        u = jnp.dot(h, wub_ref[...], preferred_element_type=F32)
        o_ref[rows, :] = (g * jax.nn.sigmoid(g) * u).astype(BF16)

    _for_occupied_chunks(tr_ref[m], o_ref, chunk)


def _moe_up(tile_expert, n_valid, tile_rows, hs, w_gu, layer):
    tf = 1792
    nf = D_FF_EXPERT // tf
    return pl.pallas_call(
        _moe_up_kernel,
        grid_spec=pltpu.PrefetchScalarGridSpec(
            num_scalar_prefetch=3,
            grid=(nf, MOE_NT),
            in_specs=[pl.BlockSpec((MOE_TILE, D_MODEL), lambda f, m, te, nv, tr: (jnp.minimum(m, nv[0] - 1), 0)),
                      pl.BlockSpec((1, 1, D_MODEL, tf), lambda f, m, te, nv, tr: (layer, te[m], 0, f)),
                      pl.BlockSpec((1, 1, D_MODEL, tf), lambda f, m, te, nv, tr: (layer, te[m], 0, nf + f))],
            out_specs=pl.BlockSpec((MOE_TILE, tf), lambda f, m, te, nv, tr: (m, f)),
            scratch_shapes=[pltpu.VMEM((D_MODEL, tf), BF16), pltpu.VMEM((D_MODEL, tf), BF16)]),
        out_shape=jax.ShapeDtypeStruct((MOE_SLOTS, D_FF_EXPERT), BF16),
        compiler_params=_cparams(("arbitrary", "arbitrary")),
        name="moe_up",
    )(tile_expert, n_valid, tile_rows, hs, w_gu, w_gu)


def _moe_down_kernel(te_ref, nv_ref, tr_ref, a_ref, w_ref, o_ref, wb_ref):
    m = pl.program_id(1)
    new_w = jnp.logical_or(m == 0, te_ref[m] != te_ref[jnp.maximum(m - 1, 0)])

    @pl.when(new_w)
    def _():
        wb_ref[...] = w_ref[0, 0].astype(BF16)

    def chunk(rows):
        o_ref[rows, :] = jnp.dot(a_ref[rows, :], wb_ref[...], preferred_element_type=F32)

    _for_occupied_chunks(tr_ref[m], o_ref, chunk)


def _moe_down(tile_expert, n_valid, tile_rows, act, w_down, layer):
    tn = D_MODEL
    return pl.pallas_call(
        _moe_down_kernel,
        grid_spec=pltpu.PrefetchScalarGridSpec(
            num_scalar_prefetch=3,
            grid=(D_MODEL // tn, MOE_NT),
            in_specs=[pl.BlockSpec((MOE_TILE, D_FF_EXPERT), lambda n, m, te, nv, tr: (jnp.minimum(m, nv[0] - 1), 0)),
                      pl.BlockSpec((1, 1, D_FF_EXPERT, tn), lambda n, m, te, nv, tr: (layer, te[m], 0, n))],
            out_specs=pl.BlockSpec((MOE_TILE, tn), lambda n, m, te, nv, tr: (m, n)),
            scratch_shapes=[pltpu.VMEM((D_FF_EXPERT, tn), BF16)]),
        out_shape=jax.ShapeDtypeStruct((MOE_SLOTS, D_MODEL), F32),
        compiler_params=_cparams(("arbitrary", "arbitrary")),
        name="moe_down",
    )(tile_expert, n_valid, tile_rows, act, w_down)


def _combine_kernel(slot_ref, y_hbm, x_ref, rt_ref, pk_ref, pkn_ref, fin_ref, *rest, tok0, final):
    if final:
        out_ref, ybuf, sems = rest
    else:
        xo_ref, h_ref, ybuf, sems = rest
    tc = TC_COMBINE
    i = pl.program_id(0)
    n = pl.num_programs(0)
    cur = i % 2
    nxt = 1 - cur

    def issue_rows(tile, buf, r0):
        base = (tok0 + tile * tc) * TOP_K
        for r in range(COMBINE_GROUP):
            for k in range(TOP_K):
                pltpu.make_async_copy(y_hbm.at[pl.ds(slot_ref[base + (r0 + r) * TOP_K + k], 1)],
                                      ybuf.at[buf, k, pl.ds(r0 + r, 1)], sems.at[buf]).start()

    def wait_tile(buf):
        for k in range(TOP_K):
            pltpu.make_async_copy(y_hbm.at[pl.ds(0, tc)], ybuf.at[buf, k], sems.at[buf]).wait()

    def group_rows(g):
        return pl.multiple_of(g * COMBINE_GROUP, COMBINE_GROUP)

    @pl.when(i == 0)
    def _():
        def first(g, carry):
            issue_rows(0, 0, group_rows(g))
            return carry

        lax.fori_loop(0, tc // COMBINE_GROUP, first, 0)

    wait_tile(cur)
    nxt_tile = jnp.minimum(i + 1, n - 1)
    pk = pk_ref[0, 0]
    pkn = pkn_ref[0, 0]

    def group(g, carry):
        r0 = group_rows(g)
        issue_rows(nxt_tile, nxt, r0)
        rows = pl.ds(r0, COMBINE_GROUP)
        rt = rt_ref[rows, :]
        moe = rt[:, 2:3] * ybuf[cur, 0, rows, :] + rt[:, 3:4] * ybuf[cur, 1, rows, :]
        x_new = x_ref[rows, :] + pk[R_GATE_FFN:R_GATE_FFN + 1] * moe
        if final:
            out_ref[rows, :] = _epilogue(x_new, None, None, fin_ref, True)
        else:
            xo_ref[rows, :] = x_new
            h_ref[rows, :] = _epilogue(x_new, pkn, (R_G_MIX, R_SCALE_MIX, R_SHIFT_MIX), None, False).astype(BF16)
        return carry

    lax.fori_loop(0, tc // COMBINE_GROUP, group, 0, unroll=True)

    @pl.when(i == n - 1)
    def _():
        wait_tile(nxt)


def _combine(slot, y, x, route, pack, layer, next_layer, final_norm, tok0, n_rows, final):
    tc = TC_COMBINE
    b0 = tok0 // tc

    def tmap(m, s):
        return (b0 + m, 0)

    def pmap(l):
        return lambda m, s: (l, (tok0 + m * tc) // GROUP_TOKENS, 0, 0)

    if final:
        out_specs = pl.BlockSpec((tc, D_MODEL), lambda m, s: (m, 0))
        out_shape = jax.ShapeDtypeStruct((n_rows, D_MODEL), F32)
    else:
        out_specs = [pl.BlockSpec((tc, D_MODEL), lambda m, s: (m, 0)),
                     pl.BlockSpec((tc, D_MODEL), lambda m, s: (m, 0))]
        out_shape = [jax.ShapeDtypeStruct((n_rows, D_MODEL), F32),
                     jax.ShapeDtypeStruct((n_rows, D_MODEL), BF16)]
    kern = functools.partial(_combine_kernel, tok0=tok0, final=final)
    return pl.pallas_call(
        kern,
        grid_spec=pltpu.PrefetchScalarGridSpec(
            num_scalar_prefetch=1,
            grid=(n_rows // tc,),
            in_specs=[pl.BlockSpec(memory_space=pl.ANY),
                      pl.BlockSpec((tc, D_MODEL), tmap),
                      pl.BlockSpec((tc, LANES), tmap),
                      pl.BlockSpec((1, 1, 8, D_MODEL), pmap(layer)),
                      pl.BlockSpec((1, 1, 8, D_MODEL), pmap(next_layer)),
                      pl.BlockSpec((1, D_MODEL), lambda m, s: (0, 0))],
            out_specs=out_specs,
            scratch_shapes=[pltpu.VMEM((2, TOP_K, tc, D_MODEL), F32), pltpu.SemaphoreType.DMA((2,))]),
        out_shape=out_shape,
        compiler_params=_cparams(("arbitrary",)),
        name="moe_combine",
    )(slot, y, x, route, pack, pack, final_norm.reshape(1, D_MODEL))


def _routing_tables(route):
    e_flat = route[:, :TOP_K].astype(I32).reshape(-1)
    onehot = (e_flat[:, None] == jnp.arange(N_EXPERTS, dtype=I32)[None, :]).astype(I32)
    csum = jnp.cumsum(onehot, axis=0)
    rank = jnp.sum(onehot * (csum - 1), axis=1)
    count = csum[-1]
    padded = ((count + MOE_TILE - 1) // MOE_TILE) * MOE_TILE
    pend = jnp.cumsum(padded)
    poff = pend - padded
    slot = jnp.sum(onehot * poff[None, :], axis=1) + rank
    n_valid = pend[-1] // MOE_TILE
    tile_start = jnp.arange(MOE_NT, dtype=I32) * MOE_TILE
    te_raw = jnp.minimum(jnp.sum((tile_start[:, None] >= pend[None, :]).astype(I32), axis=1), N_EXPERTS - 1)
    last_e = jnp.max(jnp.where(tile_start < pend[-1], te_raw, 0))
    tile_expert = jnp.minimum(te_raw, last_e)
    group_rows_end = jnp.sum((te_raw[:, None] == jnp.arange(N_EXPERTS, dtype=I32)[None, :]) * (poff + count)[None, :],
                             axis=1)
    tile_rows = jnp.where(tile_start < pend[-1], jnp.clip(group_rows_end - tile_start, 0, MOE_TILE), 0)
    return (slot.astype(I32), pend.astype(I32), tile_expert.astype(I32), n_valid.astype(I32).reshape(1),
            tile_rows.astype(I32))


def _rope_tables(dh):
    nf = dh // 4
    t = jnp.arange(DEC_SEQ)
    freqs = ROPE_BASE ** (-jnp.arange(nf, dtype=F32) / nf)
    row = (t // GRID_W).astype(F32)
    col = (t % GRID_W).astype(F32)
    ang = jnp.concatenate([row[:, None] * freqs, col[:, None] * freqs], axis=-1)
    return jnp.cos(ang), jnp.sin(ang)


def kernel(x_prompt, x_sample, cache_k, cache_v, state_fwd, state_bwd, c, c_ctx, norm_mix, norm_ffn, w_mod, b_mod,
           w_qkv, w_attn_o, attn_sink, w_ret_in, ret_decay, w_ret_out, w_ffn_gu, w_ffn_down, w_router, w_exp_gu,
           w_exp_down, final_norm):
    x = (x_prompt.reshape(N_PROMPT, D_MODEL), x_sample.reshape(N_SAMPLE, D_MODEL))

    cond8 = jnp.concatenate([c_ctx[None, :], c, jnp.zeros((8 - 1 - DEC_BATCH, D_MODEL), F32)], axis=0)
    mods = _modulations(cond8, w_mod, b_mod)
    m3 = mods[:, :N_GROUPS].reshape(DEPTH, N_GROUPS, 6, D_MODEL)
    pack = jnp.concatenate([
        m3,
        jnp.broadcast_to(norm_mix[:, None, None, :], (DEPTH, N_GROUPS, 1, D_MODEL)),
        jnp.broadcast_to(norm_ffn[:, None, None, :], (DEPTH, N_GROUPS, 1, D_MODEL))], axis=2)

    cos_a, sin_a = _rope_tables(HEAD_DIM)
    cos_attn = jnp.tile(cos_a, (1, LANES // (HEAD_DIM // 2)))
    sin_attn = jnp.tile(jnp.concatenate([-sin_a, sin_a], axis=1), (1, LANES // HEAD_DIM))
    cos_ret, sin_ret = _rope_tables(RET_DK)

    kvw = N_KV_HEADS * HEAD_DIM
    cache_k4 = cache_k.reshape(DEC_BATCH, N_EVEN, PAST_LEN, kvw)
    cache_v4 = cache_v.reshape(DEC_BATCH, N_EVEN, PAST_LEN, kvw)
    sink_flat = attn_sink.reshape(-1)
    log_gamma = jax.nn.log_sigmoid(ret_decay.astype(F32)).reshape(-1)
    w_router_pad = jnp.pad(w_router, ((0, 0), (0, 0), (0, LANES - N_EXPERTS)))
    w_router_hi = w_router_pad.astype(BF16)
    w_router_lo = (w_router_pad - w_router_hi.astype(F32)).astype(BF16)

    mix_rows = (R_G_MIX, R_SCALE_MIX, R_SHIFT_MIX)
    ffn_rows = (R_G_FFN, R_SCALE_FFN, R_SHIFT_FFN)

    new_k, new_v = [], []
    new_sf = new_sb = None
    h = _pre(x, pack)
    y_prompt = y_sample = None
    for i in range(DEPTH):
        j = i // 2
        if i % 2 == 0:
            q = _attn_proj(h, w_qkv, j, cos_attn, sin_attn, col0=0, ncols=N_HEADS * HEAD_DIM,
                           rope_cols=N_HEADS * HEAD_DIM, scale=HEAD_DIM ** -0.5 * LOG2E, out_dtype=BF16)
            kv = _attn_proj(h, w_qkv, j, cos_attn, sin_attn, col0=N_HEADS * HEAD_DIM, ncols=2 * kvw,
                            rope_cols=kvw, scale=1.0, out_dtype=F32)
            new_k.append(kv[:N_PROMPT, :kvw].reshape(BATCH, SEQ, N_KV_HEADS, HEAD_DIM))
            new_v.append(kv[:N_PROMPT, kvw:].reshape(BATCH, SEQ, N_KV_HEADS, HEAD_DIM))
            o_ctx = _ctx_attention(q, kv, sink_flat, j)
            o_lat = _lat_attention(q, kv, cache_k4, cache_v4, sink_flat, j)
            x, h = _down([o_ctx, o_lat], w_attn_o, j, x, pack, i, R_GATE_MIX, i, ffn_rows)
            act = _swiglu_up(h, w_ffn_gu, j)
            x, h = _down([act], w_ffn_down, j, x, pack, i, R_GATE_FFN, i + 1, mix_rows)
        else:
            proj = _ret_proj(h, w_ret_in, j, cos_ret, sin_ret)
            part, new_sf = _retention_pass(proj, state_fwd, log_gamma, j, False, states=new_sf)
            o_ret, new_sb = _retention_pass(proj, state_bwd, log_gamma, j, True, partial=part, states=new_sb)
            x, h, hf, route = _down([o_ret], w_ret_out, j, x, pack, i, R_GATE_MIX, i, ffn_rows,
                                    w_router=(w_router_hi, w_router_lo), router_layer=j)
            slot, pend, tile_expert, n_valid, tile_rows = _routing_tables(route)
            hs = _dispatch_rows(slot, pend, hf)
            act = _moe_up(tile_expert, n_valid, tile_rows, hs, w_exp_gu, j)
            ys = _moe_down(tile_expert, n_valid, tile_rows, act, w_exp_down, j)
            if i == DEPTH - 1:
                y_prompt = _combine(slot, ys, x, route, pack, i, i, final_norm, 0, N_PROMPT, True)
                y_sample = _combine(slot, ys, x, route, pack, i, i, final_norm, N_PROMPT, N_SAMPLE, True)
            else:
                x, h = _combine(slot, ys, x, route, pack, i, i + 1, final_norm, 0, N_TOK, False)

    return (y_prompt.reshape(BATCH, SEQ, D_MODEL), y_sample.reshape(DEC_BATCH, DEC_SEQ, D_MODEL),
            jnp.stack(new_k, axis=1), jnp.stack(new_v, axis=1), new_sf, new_sb)
```

```python
import functools

import jax
import jax.numpy as jnp
from jax import lax
from jax.experimental import pallas as pl
from jax.experimental.pallas import tpu as pltpu

F32 = jnp.float32
BF16 = jnp.bfloat16
I32 = jnp.int32

D_MODEL = 1024
BATCH = 16
SEQ = 256
DEPTH = 4
DEC_BATCH = 2
DEC_SEQ = 4096
PAST_LEN = 256
GRID_W = 64
BLOCK = 128
EPS = 1e-6
N_HEADS = 16
N_KV_HEADS = 4
HEAD_DIM = 64
GROUP = N_HEADS // N_KV_HEADS
WINDOW = 128
ROPE_BASE = 10000.0
RET_HEADS = 4
RET_DK = 256
RET_DV = 512
RET_HK = RET_HEADS * RET_DK
RET_HV = RET_HEADS * RET_DV
D_FF = 2816
N_EXPERTS = 8
TOP_K = 2
D_FF_EXPERT = 3584
N_EVEN = 2
N_ODD = 2
NEG = -1e30
LOG2E = 1.4426950408889634

GROUP_TOKENS = 4096
N_PROMPT = BATCH * SEQ
N_SAMPLE = DEC_BATCH * DEC_SEQ
N_TOK = N_PROMPT + N_SAMPLE
N_GROUPS = N_TOK // GROUP_TOKENS

R_SHIFT_MIX, R_SCALE_MIX, R_GATE_MIX, R_SHIFT_FFN, R_SCALE_FFN, R_GATE_FFN, R_G_MIX, R_G_FFN = range(8)

VMEM_LIMIT_BYTES = 56 * 1024 * 1024
LANES = 128

TM = 2048
UP_CHUNK = 512
TM_DOWN = 512
DOWN_CHUNK = 128
MOE_TILE = 512
MOE_CHUNK = 256
MOE_SLOTS = TOP_K * N_TOK + N_EXPERTS * MOE_TILE
MOE_NT = MOE_SLOTS // MOE_TILE
TC_COMBINE = 256
COMBINE_GROUP = 16
DISPATCH_ROWS = 256


def _cparams(sem):
    return pltpu.CompilerParams(dimension_semantics=sem, vmem_limit_bytes=VMEM_LIMIT_BYTES)


def _normmod(x, g, scale, shift):
    ms = jnp.mean(x * x, axis=-1, keepdims=True)
    return (x * lax.rsqrt(ms + EPS) * g) * (1.0 + scale) + shift


def _mod_kernel(cond_ref, w_ref, b_ref, o_ref):
    c = cond_ref[...]
    s = c * jax.nn.sigmoid(c)
    o_ref[0] = jnp.dot(s.astype(BF16), w_ref[0].astype(BF16), preferred_element_type=F32) + b_ref[0]


def _modulations(cond8, w_mod, b_mod):
    tn = 2048
    n6 = 6 * D_MODEL
    return pl.pallas_call(
        _mod_kernel,
        grid=(DEPTH, n6 // tn),
        in_specs=[pl.BlockSpec((8, D_MODEL), lambda l, n: (0, 0)),
                  pl.BlockSpec((1, D_MODEL, tn), lambda l, n: (l, 0, n)),
                  pl.BlockSpec((1, 1, tn), lambda l, n: (l, 0, n))],
        out_specs=pl.BlockSpec((1, 8, tn), lambda l, n: (l, 0, n)),
        out_shape=jax.ShapeDtypeStruct((DEPTH, 8, n6), F32),
        compiler_params=_cparams(("arbitrary", "arbitrary")),
        name="modulations",
    )(cond8, w_mod, b_mod.reshape(DEPTH, 1, n6))


def _split_specs(tm, width):
    fst = N_PROMPT // tm
    return [pl.BlockSpec((tm, width), lambda m: (jnp.minimum(m, fst - 1), 0)),
            pl.BlockSpec((tm, width), lambda m: (jnp.maximum(m - fst, 0), 0))]


def _pre_kernel(xp_ref, xs_ref, pk_ref, h_ref, *, first_sample_tile):
    def norm(x_ref):
        pk = pk_ref[0, 0]
        h = _normmod(x_ref[...], pk[R_G_MIX:R_G_MIX + 1], pk[R_SCALE_MIX:R_SCALE_MIX + 1],
                     pk[R_SHIFT_MIX:R_SHIFT_MIX + 1])
        h_ref[...] = h.astype(BF16)

    m = pl.program_id(0)
    pl.when(m < first_sample_tile)(lambda: norm(xp_ref))
    pl.when(m >= first_sample_tile)(lambda: norm(xs_ref))


def _pre(x_pair, pack):
    tm = TM
    return pl.pallas_call(
        functools.partial(_pre_kernel, first_sample_tile=N_PROMPT // tm),
        grid=(N_TOK // tm,),
        in_specs=_split_specs(tm, D_MODEL) + [
            pl.BlockSpec((1, 1, 8, D_MODEL), lambda m: (0, (m * tm) // GROUP_TOKENS, 0, 0))],
        out_specs=pl.BlockSpec((tm, D_MODEL), lambda m: (m, 0)),
        out_shape=jax.ShapeDtypeStruct((N_TOK, D_MODEL), BF16),
        compiler_params=_cparams(("arbitrary",)),
        name="pre_norm",
    )(*x_pair, pack)


def _row_chunks(n_rows):
    return [pl.ds(c * UP_CHUNK, UP_CHUNK) for c in range(n_rows // UP_CHUNK)]


def _table_map(tm):
    first_sample_tile = N_PROMPT // tm
    tiles_per_seq = DEC_SEQ // tm
    return lambda m: (jnp.maximum(m - first_sample_tile, 0) % tiles_per_seq, 0)


def _rope64(y, cos, sin_signed):
    width = y.shape[-1]
    lane = lax.broadcasted_iota(I32, y.shape, 1)
    first = (lane % HEAD_DIM) < (HEAD_DIM // 2)
    swapped = jnp.where(first, pltpu.roll(y, width - HEAD_DIM // 2, 1), pltpu.roll(y, HEAD_DIM // 2, 1))
    reps = width // LANES
    c = jnp.concatenate([cos] * reps, axis=1) if reps > 1 else cos
    s = jnp.concatenate([sin_signed] * reps, axis=1) if reps > 1 else sin_signed
    return y * c + swapped * s


def _attn_proj_kernel(h_ref, w_ref, cos_ref, sin_ref, o_ref, wb_ref, *, scale, rope_cols, first_sample_tile):
    m = pl.program_id(0)

    @pl.when(m == 0)
    def _():
        wb_ref[...] = w_ref[0].astype(BF16)

    def project(rope):
        for rows in _row_chunks(h_ref.shape[0]):
            y = jnp.dot(h_ref[rows, :], wb_ref[...], preferred_element_type=F32)
            if scale != 1.0:
                y = y * scale
            if rope:
                roped = _rope64(y[:, :rope_cols], cos_ref[rows, :], sin_ref[rows, :])
                y = roped if rope_cols == y.shape[1] else jnp.concatenate([roped, y[:, rope_cols:]], axis=1)
            o_ref[rows, :] = y.astype(o_ref.dtype)

    pl.when(m >= first_sample_tile)(lambda: project(True))
    pl.when(m < first_sample_tile)(lambda: project(False))


def _attn_proj(h, w_qkv, layer, cos, sin, *, col0, ncols, rope_cols, scale, out_dtype):
    tm = TM
    kern = functools.partial(_attn_proj_kernel, scale=scale, rope_cols=rope_cols, first_sample_tile=N_PROMPT // tm)
    return pl.pallas_call(
        kern,
        grid=(N_TOK // tm,),
        in_specs=[pl.BlockSpec((tm, D_MODEL), lambda m: (m, 0)),
                  pl.BlockSpec((1, D_MODEL, ncols), lambda m: (layer, 0, col0 // ncols)),
                  pl.BlockSpec((tm, LANES), _table_map(tm)),
                  pl.BlockSpec((tm, LANES), _table_map(tm))],
        out_specs=pl.BlockSpec((tm, ncols), lambda m: (m, 0)),
        out_shape=jax.ShapeDtypeStruct((N_TOK, ncols), out_dtype),
        scratch_shapes=[pltpu.VMEM((D_MODEL, ncols), BF16)],
        compiler_params=_cparams(("arbitrary",)),
        name="attn_proj",
    )(h, w_qkv, cos, sin)


RET_TN = RET_HK


def _ret_proj_kernel(h_ref, w_ref, cos_ref, sin_ref, o_ref, wb_ref, *, first_sample_tile):
    n = pl.program_id(0)
    m = pl.program_id(1)

    @pl.when(m == 0)
    def _():
        wb_ref[...] = w_ref[0].astype(BF16)

    k_scale = jnp.where(n == 1, RET_DK ** -0.5, 1.0).astype(F32)
    half = RET_DK // 2

    def project(rope, gate=False):
        for rows in _row_chunks(h_ref.shape[0]):
            y = jnp.dot(h_ref[rows, :], wb_ref[...], preferred_element_type=F32) * k_scale
            if gate:
                y = y * jax.nn.sigmoid(y)
            if rope:
                c = cos_ref[rows, :]
                s = sin_ref[rows, :]
                parts = []
                for hh in range(RET_TN // RET_DK):
                    x1 = y[:, hh * RET_DK:hh * RET_DK + half]
                    x2 = y[:, hh * RET_DK + half:(hh + 1) * RET_DK]
                    parts.append(x1 * c - x2 * s)
                    parts.append(x1 * s + x2 * c)
                y = jnp.concatenate(parts, axis=1)
            o_ref[rows, :] = y.astype(BF16)

    gate_tile0 = (2 * RET_HK + RET_HV) // RET_TN
    do_rope = jnp.logical_and(m >= first_sample_tile, n < 2)
    is_gate = n >= gate_tile0
    pl.when(do_rope)(lambda: project(True))
    pl.when(is_gate)(lambda: project(False, gate=True))
    pl.when(jnp.logical_not(jnp.logical_or(do_rope, is_gate)))(lambda: project(False))


def _ret_proj(h, w_ret_in, layer, cos, sin):
    tm = TM
    ncols = 2 * RET_HK + 3 * RET_HV
    tmap = _table_map(tm)
    kern = functools.partial(_ret_proj_kernel, first_sample_tile=N_PROMPT // tm)
    return pl.pallas_call(
        kern,
        grid=(ncols // RET_TN, N_TOK // tm),
        in_specs=[pl.BlockSpec((tm, D_MODEL), lambda n, m: (m, 0)),
                  pl.BlockSpec((1, D_MODEL, RET_TN), lambda n, m: (layer, 0, n)),
                  pl.BlockSpec((tm, LANES), lambda n, m: tmap(m)),
                  pl.BlockSpec((tm, LANES), lambda n, m: tmap(m))],
        out_specs=pl.BlockSpec((tm, RET_TN), lambda n, m: (m, n)),
        out_shape=jax.ShapeDtypeStruct((N_TOK, ncols), BF16),
        scratch_shapes=[pltpu.VMEM((D_MODEL, RET_TN), BF16)],
        compiler_params=_cparams(("arbitrary", "arbitrary")),
        name="ret_proj",
    )(h, w_ret_in, cos, sin)


def _swiglu_kernel(h_ref, wg_ref, wu_ref, o_ref, wgb_ref, wub_ref):
    m = pl.program_id(1)

    @pl.when(m == 0)
    def _():
        wgb_ref[...] = wg_ref[0].astype(BF16)
        wub_ref[...] = wu_ref[0].astype(BF16)

    for rows in _row_chunks(h_ref.shape[0]):
        h = h_ref[rows, :]
        g = jnp.dot(h, wgb_ref[...], preferred_element_type=F32)
        u = jnp.dot(h, wub_ref[...], preferred_element_type=F32)
        o_ref[rows, :] = (g * jax.nn.sigmoid(g) * u).astype(BF16)


def _swiglu_up(h, w_gu, layer):
    tm, tf = 1024, D_FF // 2
    nf = D_FF // tf
    return pl.pallas_call(
        _swiglu_kernel,
        grid=(nf, N_TOK // tm),
        in_specs=[pl.BlockSpec((tm, D_MODEL), lambda f, m: (m, 0)),
                  pl.BlockSpec((1, D_MODEL, tf), lambda f, m: (layer, 0, f)),
                  pl.BlockSpec((1, D_MODEL, tf), lambda f, m: (layer, 0, nf + f))],
        out_specs=pl.BlockSpec((tm, tf), lambda f, m: (m, f)),
        out_shape=jax.ShapeDtypeStruct((N_TOK, D_FF), BF16),
        scratch_shapes=[pltpu.VMEM((D_MODEL, tf), BF16), pltpu.VMEM((D_MODEL, tf), BF16)],
        compiler_params=_cparams(("arbitrary", "arbitrary")),
        name="swiglu_up",
    )(h, w_gu, w_gu)


def _route(logits):
    lane = lax.broadcasted_iota(I32, logits.shape, 1).astype(F32)
    lg = jnp.where(lane < N_EXPERTS, logits, -jnp.inf)
    m1 = jnp.max(lg, axis=-1, keepdims=True)
    i1 = jnp.min(jnp.where(lg == m1, lane, float(LANES)), axis=-1, keepdims=True)
    lg2 = jnp.where(lane == i1, -jnp.inf, lg)
    m2 = jnp.max(lg2, axis=-1, keepdims=True)
    i2 = jnp.min(jnp.where(lg2 == m2, lane, float(LANES)), axis=-1, keepdims=True)
    e2 = jnp.exp(m2 - m1)
    p1 = 1.0 / (1.0 + e2)
    p2 = e2 / (1.0 + e2)
    out = jnp.where(lane == 0, i1, 0.0)
    out = jnp.where(lane == 1, i2, out)
    out = jnp.where(lane == 2, p1, out)
    out = jnp.where(lane == 3, p2, out)
    return out


def _epilogue(x_new, pk_next, rows, fin_ref, final):
    if final:
        ms = jnp.mean(x_new * x_new, axis=-1, keepdims=True)
        return x_new * lax.rsqrt(ms + EPS) * fin_ref[...]
    g_row, sc_row, sh_row = rows
    return _normmod(x_new, pk_next[g_row:g_row + 1], pk_next[sc_row:sc_row + 1], pk_next[sh_row:sh_row + 1])


def _down_kernel(*refs, n_a, n_x, first_sample_tile, gate_row, next_rows, router):
    a_refs = refs[:n_a]
    w_ref = refs[n_a]
    x_refs = refs[n_a + 1:n_a + 1 + n_x]
    pk_ref, pkn_ref = refs[n_a + 1 + n_x:n_a + 3 + n_x]
    pos = n_a + 3 + n_x
    if router:
        wrh_ref, wrl_ref = refs[pos:pos + 2]
        pos += 2
        xo_ref, h_ref, hf_ref, rt_ref, wb_ref = refs[pos:pos + 5]
    else:
        xo_ref, h_ref, wb_ref = refs[pos:pos + 3]
    m = pl.program_id(0)

    @pl.when(m == 0)
    def _():
        wb_ref[...] = w_ref[0].astype(BF16)

    def finish(a_ref, x_ref):
        pk = pk_ref[0, 0]
        pkn = pkn_ref[0, 0]
        for c in range(a_ref.shape[0] // DOWN_CHUNK):
            rows = pl.ds(c * DOWN_CHUNK, DOWN_CHUNK)
            y = jnp.dot(a_ref[rows, :], wb_ref[...], preferred_element_type=F32)
            x_new = x_ref[rows, :] + pk[gate_row:gate_row + 1] * y
            xo_ref[rows, :] = x_new
            hn = _epilogue(x_new, pkn, next_rows, None, False)
            hi = hn.astype(BF16)
            h_ref[rows, :] = hi
            if router:
                hf_ref[rows, :] = hn
        if router:
            hi = h_ref[...]
            lo = (hf_ref[...] - hi.astype(F32)).astype(BF16)
            logits = (jnp.dot(hi, wrh_ref[0], preferred_element_type=F32)
                      + jnp.dot(lo, wrh_ref[0], preferred_element_type=F32)
                      + jnp.dot(hi, wrl_ref[0], preferred_element_type=F32))
            rt_ref[...] = _route(logits)

    if n_a == 1 and n_x == 1:
        finish(a_refs[0], x_refs[0])
    else:
        pl.when(m < first_sample_tile)(lambda: finish(a_refs[0], x_refs[0]))
        pl.when(m >= first_sample_tile)(lambda: finish(a_refs[-1], x_refs[-1]))


def _down(a_list, w, layer, x, pack, pack_layer, gate_row, next_layer, next_rows, w_router=None, router_layer=0):
    tm = TM_DOWN
    kd = w.shape[1]
    n_a = len(a_list)
    fst = N_PROMPT // tm
    router = w_router is not None
    x_list = list(x) if isinstance(x, (list, tuple)) else [x]
    n_x = len(x_list)
    a_specs = [pl.BlockSpec((tm, kd), lambda m: (m, 0))] if n_a == 1 else _split_specs(tm, kd)
    x_specs = [pl.BlockSpec((tm, D_MODEL), lambda m: (m, 0))] if n_x == 1 else _split_specs(tm, D_MODEL)
    in_specs = a_specs + [pl.BlockSpec((1, kd, D_MODEL), lambda m: (layer, 0, 0))] + x_specs + [
        pl.BlockSpec((1, 1, 8, D_MODEL), lambda m: (pack_layer, (m * tm) // GROUP_TOKENS, 0, 0)),
        pl.BlockSpec((1, 1, 8, D_MODEL), lambda m: (next_layer, (m * tm) // GROUP_TOKENS, 0, 0)),
    ]
    args = list(a_list) + [w] + list(x_list) + [pack, pack]
    out_specs = [pl.BlockSpec((tm, D_MODEL), lambda m: (m, 0)),
                 pl.BlockSpec((tm, D_MODEL), lambda m: (m, 0))]
    out_shape = [jax.ShapeDtypeStruct((N_TOK, D_MODEL), F32),
                 jax.ShapeDtypeStruct((N_TOK, D_MODEL), BF16)]
    if router:
        in_specs += [pl.BlockSpec((1, D_MODEL, LANES), lambda m: (router_layer, 0, 0)),
                     pl.BlockSpec((1, D_MODEL, LANES), lambda m: (router_layer, 0, 0))]
        args += list(w_router)
        out_specs += [pl.BlockSpec((tm, D_MODEL), lambda m: (m, 0)),
                      pl.BlockSpec((tm, LANES), lambda m: (m, 0))]
        out_shape += [jax.ShapeDtypeStruct((N_TOK, D_MODEL), F32),
                      jax.ShapeDtypeStruct((N_TOK, LANES), F32)]
    kern = functools.partial(_down_kernel, n_a=n_a, n_x=n_x, first_sample_tile=fst, gate_row=gate_row,
                             next_rows=next_rows, router=router)
    return pl.pallas_call(
        kern,
        grid=(N_TOK // tm,),
        in_specs=in_specs,
        out_specs=out_specs,
        out_shape=out_shape,
        scratch_shapes=[pltpu.VMEM((kd, D_MODEL), BF16)],
        compiler_params=_cparams(("arbitrary",)),
        name="down_proj",
    )(*args)


def _sink_row(sink_ref, layer, kvh, width):
    return jnp.concatenate(
        [jnp.full((1, width), sink_ref[layer * N_HEADS + kvh * GROUP + g] * LOG2E, F32) for g in range(GROUP)],
        axis=1)


def _group_queries(q_ref, kvh):
    return jnp.concatenate(
        [q_ref[:, (kvh * GROUP + g) * HEAD_DIM:(kvh * GROUP + g + 1) * HEAD_DIM] for g in range(GROUP)], axis=0)


_NT = (((1,), (1,)), ((), ()))


def _ctx_attn_kernel(sink_ref, q_ref, kv_ref, o_ref, *, layer):
    kvw = N_KV_HEADS * HEAD_DIM
    k_all = kv_ref[:, :kvw].astype(BF16)
    v_t = kv_ref[:, kvw:].T.astype(BF16)
    outs = []
    for kvh in range(N_KV_HEADS):
        hs = slice(kvh * HEAD_DIM, (kvh + 1) * HEAD_DIM)
        s_t = lax.dot_general(k_all[:, hs], _group_queries(q_ref, kvh), _NT, preferred_element_type=F32)
        sink = _sink_row(sink_ref, layer, kvh, SEQ)
        m = jnp.maximum(jnp.max(s_t, axis=0, keepdims=True), sink)
        p = jnp.exp2(s_t - m)
        denom = jnp.sum(p, axis=0, keepdims=True) + jnp.exp2(sink - m)
        o_t = jnp.dot(v_t[hs], p.astype(BF16), preferred_element_type=F32) / denom
        outs += [o_t[:, g * SEQ:(g + 1) * SEQ] for g in range(GROUP)]
    o_ref[...] = jnp.concatenate(outs, axis=0).T.astype(BF16)


def _ctx_attention(q, kv, sink_flat, layer):
    kern = functools.partial(_ctx_attn_kernel, layer=layer)
    return pl.pallas_call(
        kern,
        grid_spec=pltpu.PrefetchScalarGridSpec(
            num_scalar_prefetch=1,
            grid=(BATCH,),
            in_specs=[pl.BlockSpec((SEQ, D_MODEL), lambda b, s: (b, 0)),
                      pl.BlockSpec((SEQ, 2 * N_KV_HEADS * HEAD_DIM), lambda b, s: (b, 0))],
            out_specs=pl.BlockSpec((SEQ, D_MODEL), lambda b, s: (b, 0))),
        out_shape=jax.ShapeDtypeStruct((N_PROMPT, D_MODEL), BF16),
        compiler_params=_cparams(("arbitrary",)),
        name="ctx_attention",
    )(sink_flat, q, kv)


def _lat_attn_kernel(sink_ref, q_ref, kvp_ref, kvc_ref, kvn_ref, ck_ref, cv_ref, o_ref, kc_ref, vct_ref, *, layer):
    blk = pl.program_id(1)
    kvw = N_KV_HEADS * HEAD_DIM
    cols = GROUP * BLOCK
    n_lat = 3 * BLOCK

    @pl.when(blk == 0)
    def _():
        kc_ref[...] = ck_ref[0, 0].astype(BF16)
        vct_ref[...] = cv_ref[0, 0].T.astype(BF16)

    j = lax.broadcasted_iota(I32, (n_lat, cols), 0)
    r = lax.broadcasted_iota(I32, (n_lat, cols), 1) % BLOCK
    lo = jnp.maximum(r, BLOCK - BLOCK * blk)
    hi = jnp.minimum(r + 2 * WINDOW, DEC_SEQ + BLOCK - 1 - BLOCK * blk)
    lat_cap = jnp.where(jnp.logical_and(j >= lo, j <= hi), jnp.inf, NEG).astype(F32)
    k_lat = jnp.concatenate([kvp_ref[:, :kvw], kvc_ref[:, :kvw], kvn_ref[:, :kvw]], axis=0).astype(BF16)
    v_lat_t = jnp.concatenate([kvp_ref[:, kvw:].T, kvc_ref[:, kvw:].T, kvn_ref[:, kvw:].T], axis=1).astype(BF16)
    outs = []
    for kvh in range(N_KV_HEADS):
        hs = slice(kvh * HEAD_DIM, (kvh + 1) * HEAD_DIM)
        q4 = _group_queries(q_ref, kvh)
        s_ctx = lax.dot_general(kc_ref[:, hs], q4, _NT, preferred_element_type=F32)
        s_lat = lax.dot_general(k_lat[:, hs], q4, _NT, preferred_element_type=F32)
        s_lat = jnp.minimum(s_lat, lat_cap)
        sink = _sink_row(sink_ref, layer, kvh, BLOCK)
        m = jnp.maximum(jnp.maximum(jnp.max(s_ctx, axis=0, keepdims=True),
                                    jnp.max(s_lat, axis=0, keepdims=True)), sink)
        p_ctx = jnp.exp2(s_ctx - m)
        p_lat = jnp.exp2(s_lat - m)
        denom = (jnp.sum(p_ctx, axis=0, keepdims=True) + jnp.sum(p_lat, axis=0, keepdims=True)
                 + jnp.exp2(sink - m))
        o_t = (jnp.dot(vct_ref[hs, :], p_ctx.astype(BF16), preferred_element_type=F32)
               + jnp.dot(v_lat_t[hs], p_lat.astype(BF16), preferred_element_type=F32)) / denom
        outs += [o_t[:, g * BLOCK:(g + 1) * BLOCK] for g in range(GROUP)]
    o_ref[...] = jnp.concatenate(outs, axis=0).T.astype(BF16)


def _lat_attention(q, kv, cache_k4, cache_v4, sink_flat, layer):
    nb = DEC_SEQ // BLOCK
    base = N_PROMPT // BLOCK
    kvc = 2 * N_KV_HEADS * HEAD_DIM

    def row(b, i):
        return base + b * nb + i

    kern = functools.partial(_lat_attn_kernel, layer=layer)
    return pl.pallas_call(
        kern,
        grid_spec=pltpu.PrefetchScalarGridSpec(
            num_scalar_prefetch=1,
            grid=(DEC_BATCH, nb),
            in_specs=[pl.BlockSpec((BLOCK, D_MODEL), lambda b, i, s: (row(b, i), 0)),
                      pl.BlockSpec((BLOCK, kvc), lambda b, i, s: (row(b, jnp.maximum(i - 1, 0)), 0)),
                      pl.BlockSpec((BLOCK, kvc), lambda b, i, s: (row(b, i), 0)),
                      pl.BlockSpec((BLOCK, kvc), lambda b, i, s: (row(b, jnp.minimum(i + 1, nb - 1)), 0)),
                      pl.BlockSpec((1, 1, PAST_LEN, N_KV_HEADS * HEAD_DIM), lambda b, i, s: (b, layer, 0, 0)),
                      pl.BlockSpec((1, 1, PAST_LEN, N_KV_HEADS * HEAD_DIM), lambda b, i, s: (b, layer, 0, 0))],
            out_specs=pl.BlockSpec((BLOCK, D_MODEL), lambda b, i, s: (b * nb + i, 0)),
            scratch_shapes=[pltpu.VMEM((PAST_LEN, N_KV_HEADS * HEAD_DIM), BF16),
                            pltpu.VMEM((N_KV_HEADS * HEAD_DIM, PAST_LEN), BF16)]),
        out_shape=jax.ShapeDtypeStruct((N_SAMPLE, D_MODEL), BF16),
        compiler_params=_cparams(("arbitrary", "arbitrary")),
        name="lat_attention",
    )(sink_flat, q, kv, kv, kv, cache_k4, cache_v4)


RET_C = 256
RET_SUB = 1
RET_ROWS = RET_SUB * RET_C
RET_STEPS = N_TOK // RET_ROWS
PROMPT_STEPS = N_PROMPT // RET_ROWS
STEPS_PER_SAMPLE = DEC_SEQ // RET_ROWS
assert SEQ == RET_ROWS


def _ret_kernel(lg_ref, q_ref, k_ref, v_ref, gate_ref, s0_ref, *rest, backward, layer):
    part_ref = rest[0] if backward else None
    o_ref, sout_ref, state_ref = rest[-3:]
    step = pl.program_id(0)
    blk = (RET_STEPS - 1 - step) if backward else step
    in_prompt = blk < PROMPT_STEPS
    first_of_sample = (blk - PROMPT_STEPS) % STEPS_PER_SAMPLE == (STEPS_PER_SAMPLE - 1 if backward else 0)

    @pl.when(in_prompt)
    def _():
        state_ref[...] = jnp.zeros_like(state_ref)

    @pl.when(jnp.logical_and(jnp.logical_not(in_prompt), first_of_sample))
    def _():
        state_ref[...] = s0_ref[0, 0]

    ii = lax.broadcasted_iota(I32, (RET_C, RET_C), 0).astype(F32)
    jj = lax.broadcasted_iota(I32, (RET_C, RET_C), 1).astype(F32)
    dist = (jj - ii) if backward else (ii - jj)
    ti = lax.broadcasted_iota(I32, (RET_C, 1), 0).astype(F32)
    q_pow = (RET_C - ti) if backward else (ti + 1.0)
    k_pow = ti if backward else (RET_C - 1.0 - ti)

    for h in range(RET_HEADS):
        lg = lg_ref[(layer * 2 + (1 if backward else 0)) * RET_HEADS + h]
        intra = jnp.where(dist >= 0, jnp.exp(lg * jnp.maximum(dist, 0.0)), 0.0)
        q_dec = jnp.exp(lg * q_pow)
        k_dec = jnp.exp(lg * k_pow)
        c_dec = jnp.exp(lg * RET_C)
        for sub in (reversed(range(RET_SUB)) if backward else range(RET_SUB)):
            rows = pl.ds(sub * RET_C, RET_C)
            q = q_ref[rows, h * RET_DK:(h + 1) * RET_DK]
            k = k_ref[rows, h * RET_DK:(h + 1) * RET_DK]
            v = v_ref[rows, h * RET_DV:(h + 1) * RET_DV]
            s = state_ref[h]
            a = lax.dot_general(q, k, (((1,), (1,)), ((), ())), preferred_element_type=F32) * intra
            o = (jnp.dot(a.astype(BF16), v, preferred_element_type=F32)
                 + q_dec * jnp.dot(q, s.astype(BF16), preferred_element_type=F32))
            kd = (k.astype(F32) * k_dec).astype(BF16)
            state_ref[h] = c_dec * s + lax.dot_general(kd, v, (((0,), (0,)), ((), ())), preferred_element_type=F32)
            gate = gate_ref[rows, h * RET_DV:(h + 1) * RET_DV].astype(F32)
            on = o * lax.rsqrt(jnp.mean(o * o, axis=-1, keepdims=True) + EPS)
            res = on * gate
            if backward:
                res = res + part_ref[rows, h * RET_DV:(h + 1) * RET_DV].astype(F32)
            o_ref[rows, h * RET_DV:(h + 1) * RET_DV] = res.astype(BF16)

    @pl.when(in_prompt)
    def _():
        if sout_ref.shape[1] == 1:
            sout_ref[0, 0] = state_ref[...]
        else:
            for l in range(sout_ref.shape[1]):
                sout_ref[0, l] = state_ref[...] if l == layer else jnp.zeros_like(state_ref)


def _retention_pass(proj, s0, log_gamma_flat, layer, backward, partial=None, states=None):
    def blk(i):
        return (RET_STEPS - 1 - i) if backward else i

    def s0_map(i, lg):
        return (jnp.clip((blk(i) - PROMPT_STEPS) // STEPS_PER_SAMPLE, 0, DEC_BATCH - 1), layer, 0, 0, 0)

    sout_layers = N_ODD if states is None else 1

    def sout_map(i, lg):
        return (jnp.minimum(blk(i), BATCH - 1), 0 if states is None else layer, 0, 0, 0)

    gate_block = 3 if backward else 2
    in_specs = [pl.BlockSpec((RET_ROWS, RET_HK), lambda i, lg: (blk(i), 0)),
                pl.BlockSpec((RET_ROWS, RET_HK), lambda i, lg: (blk(i), 1)),
                pl.BlockSpec((RET_ROWS, RET_HV), lambda i, lg: (blk(i), 1)),
                pl.BlockSpec((RET_ROWS, RET_HV), lambda i, lg: (blk(i), gate_block)),
                pl.BlockSpec((1, 1, RET_HEADS, RET_DK, RET_DV), s0_map)]
    args = [proj, proj, proj, proj, s0]
    if backward:
        in_specs.append(pl.BlockSpec((RET_ROWS, RET_HV), lambda i, lg: (blk(i), 0)))
        args.append(partial)
    aliases = {}
    if states is not None:
        in_specs.append(pl.BlockSpec(memory_space=pl.ANY))
        args.append(states)
        aliases = {len(args): 1}
    kern = functools.partial(_ret_kernel, backward=backward, layer=layer)
    return pl.pallas_call(
        kern,
        grid_spec=pltpu.PrefetchScalarGridSpec(
            num_scalar_prefetch=1,
            grid=(RET_STEPS,),
            in_specs=in_specs,
            out_specs=[pl.BlockSpec((RET_ROWS, RET_HV), lambda i, lg: (blk(i), 0)),
                       pl.BlockSpec((1, sout_layers, RET_HEADS, RET_DK, RET_DV), sout_map)],
            scratch_shapes=[pltpu.VMEM((RET_HEADS, RET_DK, RET_DV), F32)]),
        out_shape=[jax.ShapeDtypeStruct((N_TOK, RET_HV), BF16),
                   jax.ShapeDtypeStruct((BATCH, N_ODD, RET_HEADS, RET_DK, RET_DV), F32)],
        input_output_aliases=aliases,
        compiler_params=_cparams(("arbitrary",)),
        name="retention_bwd" if backward else "retention_fwd",
    )(log_gamma_flat, *args)


def _dispatch_kernel(slot_ref, pend_ref, h_ref, o_hbm, zbuf, sem):
    i = pl.program_id(0)

    @pl.when(i == 0)
    def _():
        zbuf[...] = jnp.zeros_like(zbuf)

        def tile_fill(start):
            return pltpu.make_async_copy(zbuf, o_hbm.at[pl.ds(pl.multiple_of(start, MOE_TILE), MOE_TILE)], sem)

        def nonempty(e):
            return pend_ref[e] > (pend_ref[e - 1] if e else 0)

        def start_unused(t, carry):
            tile_fill(t * MOE_TILE).start()
            return carry

        def wait_unused(t, carry):
            tile_fill(t * MOE_TILE).wait()
            return carry

        first_unused = pend_ref[N_EXPERTS - 1] // MOE_TILE
        for e in range(N_EXPERTS):
            pl.when(nonempty(e))(lambda e=e: tile_fill(pend_ref[e] - MOE_TILE).start())
        lax.fori_loop(first_unused, MOE_NT, start_unused, 0)
        for e in range(N_EXPERTS):
            pl.when(nonempty(e))(lambda e=e: tile_fill(pend_ref[e] - MOE_TILE).wait())
        lax.fori_loop(first_unused, MOE_NT, wait_unused, 0)

    base = i * DISPATCH_ROWS * TOP_K

    def start(r, carry):
        for k in range(TOP_K):
            pltpu.make_async_copy(h_ref.at[pl.ds(r, 1)], o_hbm.at[pl.ds(slot_ref[base + r * TOP_K + k], 1)],
                                  sem).start()
        return carry

    lax.fori_loop(0, DISPATCH_ROWS, start, 0, unroll=8)
    for k in range(TOP_K):
        pltpu.make_async_copy(h_ref, o_hbm.at[pl.ds(0, DISPATCH_ROWS)], sem).wait()


def _dispatch_rows(slot, pend, h):
    return pl.pallas_call(
        _dispatch_kernel,
        grid_spec=pltpu.PrefetchScalarGridSpec(
            num_scalar_prefetch=2,
            grid=(N_TOK // DISPATCH_ROWS,),
            in_specs=[pl.BlockSpec((DISPATCH_ROWS, D_MODEL), lambda i, s, p: (i, 0))],
            out_specs=pl.BlockSpec(memory_space=pl.ANY),
            scratch_shapes=[pltpu.VMEM((MOE_TILE, D_MODEL), F32), pltpu.SemaphoreType.DMA(())]),
        out_shape=jax.ShapeDtypeStruct((MOE_SLOTS, D_MODEL), F32),
        compiler_params=_cparams(("arbitrary",)),
        name="moe_dispatch",
    )(slot, pend, h)


def _for_occupied_chunks(n_rows, o_ref, chunk):
    n_chunks = MOE_TILE // MOE_CHUNK
    occupied = (n_rows + MOE_CHUNK - 1) // MOE_CHUNK

    def path(k):
        for c in range(n_chunks):
            rows = pl.ds(c * MOE_CHUNK, MOE_CHUNK)
            if c < k:
                chunk(rows)
            else:
                o_ref[rows, :] = jnp.zeros((MOE_CHUNK, o_ref.shape[1]), o_ref.dtype)

    for k in range(n_chunks + 1):
        pl.when(occupied == k)(functools.partial(path, k))


def _moe_up_kernel(te_ref, nv_ref, tr_ref, nx_ref, h_ref, w_hbm, o_ref, wbuf, wgb_ref, wub_ref, par_ref, sems, *,
                   layer, tf, nf):
    f = pl.program_id(0)
    m = pl.program_id(1)
    new_w = jnp.logical_or(m == 0, te_ref[m] != te_ref[jnp.maximum(m - 1, 0)])

    def fetch(col_tile, e, slot):
        col = pl.multiple_of(col_tile * tf, LANES)
        return [pltpu.make_async_copy(w_hbm.at[layer, e, :, pl.ds(half * D_FF_EXPERT + col, tf)],
                                      wbuf.at[slot, half], sems.at[slot]) for half in range(2)]

    @pl.when(jnp.logical_and(f == 0, m == 0))
    def _():
        par_ref[0] = 0
        for cp in fetch(0, te_ref[0], 0):
            cp.start()

    @pl.when(new_w)
    def _():
        slot = par_ref[0]
        for cp in fetch(f, te_ref[m], slot):
            cp.wait()
        nxt = nx_ref[m]

        @pl.when(nxt >= 0)
        def _():
            for cp in fetch(f, nxt, 1 - slot):
                cp.start()

        @pl.when(jnp.logical_and(nxt < 0, f + 1 < nf))
        def _():
            for cp in fetch(f + 1, te_ref[0], 1 - slot):
                cp.start()

        wgb_ref[...] = wbuf[slot, 0].astype(BF16)
        wub_ref[...] = wbuf[slot, 1].astype(BF16)
        par_ref[0] = 1 - slot

    def chunk(rows):
        h = h_ref[rows, :].astype(BF16)
        g = jnp.dot(h, wgb_ref[...], preferred_element_type=F32)
        u = jnp.dot(h, wub_ref[...], preferred_element_type=F32)
        o_ref[rows, :] = (g * jax.nn.sigmoid(g) * u).astype(BF16)

    _for_occupied_chunks(tr_ref[m], o_ref, chunk)


def _moe_up(tile_expert, n_valid, tile_rows, next_expert, hs, w_gu, layer):
    tf = 1792
    nf = D_FF_EXPERT // tf
    return pl.pallas_call(
        functools.partial(_moe_up_kernel, layer=layer, tf=tf, nf=nf),
        grid_spec=pltpu.PrefetchScalarGridSpec(
            num_scalar_prefetch=4,
            grid=(nf, MOE_NT),
            in_specs=[pl.BlockSpec((MOE_TILE, D_MODEL),
                                   lambda f, m, te, nv, tr, nx: (jnp.minimum(m, nv[0] - 1), 0)),
                      pl.BlockSpec(memory_space=pl.ANY)],
            out_specs=pl.BlockSpec((MOE_TILE, tf), lambda f, m, te, nv, tr, nx: (m, f)),
            scratch_shapes=[pltpu.VMEM((2, 2, D_MODEL, tf), F32),
                            pltpu.VMEM((D_MODEL, tf), BF16), pltpu.VMEM((D_MODEL, tf), BF16),
                            pltpu.SMEM((1,), I32), pltpu.SemaphoreType.DMA((2,))]),
        out_shape=jax.ShapeDtypeStruct((MOE_SLOTS, D_FF_EXPERT), BF16),
        compiler_params=_cparams(("arbitrary", "arbitrary")),
        name="moe_up",
    )(tile_expert, n_valid, tile_rows, next_expert, hs, w_gu)


def _moe_down_kernel(te_ref, nv_ref, tr_ref, nx_ref, a_ref, w_hbm, o_ref, wbuf, wb_ref, par_ref, sems, *, layer):
    m = pl.program_id(0)
    new_w = jnp.logical_or(m == 0, te_ref[m] != te_ref[jnp.maximum(m - 1, 0)])

    def fetch(e, slot):
        return pltpu.make_async_copy(w_hbm.at[layer, e], wbuf.at[slot], sems.at[slot])

    @pl.when(m == 0)
    def _():
        par_ref[0] = 0
        fetch(te_ref[0], 0).start()

    @pl.when(new_w)
    def _():
        slot = par_ref[0]
        fetch(te_ref[m], slot).wait()
        nxt = nx_ref[m]
        pl.when(nxt >= 0)(lambda: fetch(nxt, 1 - slot).start())
        wb_ref[...] = wbuf[slot].astype(BF16)
        par_ref[0] = 1 - slot

    def chunk(rows):
        o_ref[rows, :] = jnp.dot(a_ref[rows, :], wb_ref[...], preferred_element_type=F32)

    _for_occupied_chunks(tr_ref[m], o_ref, chunk)


def _moe_down(tile_expert, n_valid, tile_rows, next_expert, act, w_down, layer):
    return pl.pallas_call(
        functools.partial(_moe_down_kernel, layer=layer),
        grid_spec=pltpu.PrefetchScalarGridSpec(
            num_scalar_prefetch=4,
            grid=(MOE_NT,),
            in_specs=[pl.BlockSpec((MOE_TILE, D_FF_EXPERT),
                                   lambda m, te, nv, tr, nx: (jnp.minimum(m, nv[0] - 1), 0)),
                      pl.BlockSpec(memory_space=pl.ANY)],
            out_specs=pl.BlockSpec((MOE_TILE, D_MODEL), lambda m, te, nv, tr, nx: (m, 0)),
            scratch_shapes=[pltpu.VMEM((2, D_FF_EXPERT, D_MODEL), F32), pltpu.VMEM((D_FF_EXPERT, D_MODEL), BF16),
                            pltpu.SMEM((1,), I32), pltpu.SemaphoreType.DMA((2,))]),
        out_shape=jax.ShapeDtypeStruct((MOE_SLOTS, D_MODEL), F32),
        compiler_params=_cparams(("arbitrary",)),
        name="moe_down",
    )(tile_expert, n_valid, tile_rows, next_expert, act, w_down)


def _combine_kernel(slot_ref, y_hbm, x_ref, rt_ref, pk_ref, pkn_ref, fin_ref, *rest, tok0, final):
    if final:
        out_ref, ybuf, sems = rest
    else:
        xo_ref, h_ref, ybuf, sems = rest
    tc = TC_COMBINE
    i = pl.program_id(0)
    n = pl.num_programs(0)
    cur = i % 2
    nxt = 1 - cur

    def issue_rows(tile, buf, r0):
        base = (tok0 + tile * tc) * TOP_K
        for r in range(COMBINE_GROUP):
            for k in range(TOP_K):
                pltpu.make_async_copy(y_hbm.at[pl.ds(slot_ref[base + (r0 + r) * TOP_K + k], 1)],
                                      ybuf.at[buf, k, pl.ds(r0 + r, 1)], sems.at[buf]).start()

    def wait_tile(buf):
        for k in range(TOP_K):
            pltpu.make_async_copy(y_hbm.at[pl.ds(0, tc)], ybuf.at[buf, k], sems.at[buf]).wait()

    def group_rows(g):
        return pl.multiple_of(g * COMBINE_GROUP, COMBINE_GROUP)

    @pl.when(i == 0)
    def _():
        def first(g, carry):
            issue_rows(0, 0, group_rows(g))
            return carry

        lax.fori_loop(0, tc // COMBINE_GROUP, first, 0)

    wait_tile(cur)
    nxt_tile = jnp.minimum(i + 1, n - 1)
    pk = pk_ref[0, 0]
    pkn = pkn_ref[0, 0]

    def group(g, carry):
        r0 = group_rows(g)
        issue_rows(nxt_tile, nxt, r0)
        rows = pl.ds(r0, COMBINE_GROUP)
        rt = rt_ref[rows, :]
        moe = rt[:, 2:3] * ybuf[cur, 0, rows, :] + rt[:, 3:4] * ybuf[cur, 1, rows, :]
        x_new = x_ref[rows, :] + pk[R_GATE_FFN:R_GATE_FFN + 1] * moe
        if final:
            out_ref[rows, :] = _epilogue(x_new, None, None, fin_ref, True)
        else:
            xo_ref[rows, :] = x_new
            h_ref[rows, :] = _epilogue(x_new, pkn, (R_G_MIX, R_SCALE_MIX, R_SHIFT_MIX), None, False).astype(BF16)
        return carry

    lax.fori_loop(0, tc // COMBINE_GROUP, group, 0, unroll=True)

    @pl.when(i == n - 1)
    def _():
        wait_tile(nxt)


def _combine(slot, y, x, route, pack, layer, next_layer, final_norm, tok0, n_rows, final):
    tc = TC_COMBINE
    b0 = tok0 // tc

    def tmap(m, s):
        return (b0 + m, 0)

    def pmap(l):
        return lambda m, s: (l, (tok0 + m * tc) // GROUP_TOKENS, 0, 0)

    if final:
        out_specs = pl.BlockSpec((tc, D_MODEL), lambda m, s: (m, 0))
        out_shape = jax.ShapeDtypeStruct((n_rows, D_MODEL), F32)
    else:
        out_specs = [pl.BlockSpec((tc, D_MODEL), lambda m, s: (m, 0)),
                     pl.BlockSpec((tc, D_MODEL), lambda m, s: (m, 0))]
        out_shape = [jax.ShapeDtypeStruct((n_rows, D_MODEL), F32),
                     jax.ShapeDtypeStruct((n_rows, D_MODEL), BF16)]
    kern = functools.partial(_combine_kernel, tok0=tok0, final=final)
    return pl.pallas_call(
        kern,
        grid_spec=pltpu.PrefetchScalarGridSpec(
            num_scalar_prefetch=1,
            grid=(n_rows // tc,),
            in_specs=[pl.BlockSpec(memory_space=pl.ANY),
                      pl.BlockSpec((tc, D_MODEL), tmap),
                      pl.BlockSpec((tc, LANES), tmap),
                      pl.BlockSpec((1, 1, 8, D_MODEL), pmap(layer)),
                      pl.BlockSpec((1, 1, 8, D_MODEL), pmap(next_layer)),
                      pl.BlockSpec((1, D_MODEL), lambda m, s: (0, 0))],
            out_specs=out_specs,
            scratch_shapes=[pltpu.VMEM((2, TOP_K, tc, D_MODEL), F32), pltpu.SemaphoreType.DMA((2,))]),
        out_shape=out_shape,
        compiler_params=_cparams(("arbitrary",)),
        name="moe_combine",
    )(slot, y, x, route, pack, pack, final_norm.reshape(1, D_MODEL))


def _routing_tables(route):
    e_flat = route[:, :TOP_K].astype(I32).reshape(-1)
    onehot = (e_flat[:, None] == jnp.arange(N_EXPERTS, dtype=I32)[None, :]).astype(I32)
    csum = jnp.cumsum(onehot, axis=0)
    rank = jnp.sum(onehot * (csum - 1), axis=1)
    count = csum[-1]
    padded = ((count + MOE_TILE - 1) // MOE_TILE) * MOE_TILE
    pend = jnp.cumsum(padded)
    poff = pend - padded
    slot = jnp.sum(onehot * poff[None, :], axis=1) + rank
    n_valid = pend[-1] // MOE_TILE
    tile_start = jnp.arange(MOE_NT, dtype=I32) * MOE_TILE
    te_raw = jnp.minimum(jnp.sum((tile_start[:, None] >= pend[None, :]).astype(I32), axis=1), N_EXPERTS - 1)
    last_e = jnp.max(jnp.where(tile_start < pend[-1], te_raw, 0))
    tile_expert = jnp.minimum(te_raw, last_e)
    group_rows_end = jnp.sum((te_raw[:, None] == jnp.arange(N_EXPERTS, dtype=I32)[None, :]) * (poff + count)[None, :],
                             axis=1)
    used = tile_start < pend[-1]
    tile_rows = jnp.where(used, jnp.clip(group_rows_end - tile_start, 0, MOE_TILE), 0)
    later_other = jnp.logical_and(jnp.logical_and(tile_start[None, :] > tile_start[:, None], used[None, :]),
                                  tile_expert[None, :] != tile_expert[:, None])
    first_later = jnp.min(jnp.where(later_other, jnp.arange(MOE_NT, dtype=I32)[None, :], MOE_NT), axis=1)
    next_expert = jnp.where(first_later < MOE_NT, tile_expert[jnp.minimum(first_later, MOE_NT - 1)], -1)
    return (slot.astype(I32), pend.astype(I32), tile_expert.astype(I32), n_valid.astype(I32).reshape(1),
            tile_rows.astype(I32), next_expert.astype(I32))


def _rope_tables(dh):
    nf = dh // 4
    t = jnp.arange(DEC_SEQ)
    freqs = ROPE_BASE ** (-jnp.arange(nf, dtype=F32) / nf)
    row = (t // GRID_W).astype(F32)
    col = (t % GRID_W).astype(F32)
    ang = jnp.concatenate([row[:, None] * freqs, col[:, None] * freqs], axis=-1)
    return jnp.cos(ang), jnp.sin(ang)


def kernel(x_prompt, x_sample, cache_k, cache_v, state_fwd, state_bwd, c, c_ctx, norm_mix, norm_ffn, w_mod, b_mod,
           w_qkv, w_attn_o, attn_sink, w_ret_in, ret_decay, w_ret_out, w_ffn_gu, w_ffn_down, w_router, w_exp_gu,
           w_exp_down, final_norm):
    x = (x_prompt.reshape(N_PROMPT, D_MODEL), x_sample.reshape(N_SAMPLE, D_MODEL))

    cond8 = jnp.concatenate([c_ctx[None, :], c, jnp.zeros((8 - 1 - DEC_BATCH, D_MODEL), F32)], axis=0)
    mods = _modulations(cond8, w_mod, b_mod)
    m3 = mods[:, :N_GROUPS].reshape(DEPTH, N_GROUPS, 6, D_MODEL)
    pack = jnp.concatenate([
        m3,
        jnp.broadcast_to(norm_mix[:, None, None, :], (DEPTH, N_GROUPS, 1, D_MODEL)),
        jnp.broadcast_to(norm_ffn[:, None, None, :], (DEPTH, N_GROUPS, 1, D_MODEL))], axis=2)

    cos_a, sin_a = _rope_tables(HEAD_DIM)
    cos_attn = jnp.tile(cos_a, (1, LANES // (HEAD_DIM // 2)))
    sin_attn = jnp.tile(jnp.concatenate([-sin_a, sin_a], axis=1), (1, LANES // HEAD_DIM))
    cos_ret, sin_ret = _rope_tables(RET_DK)

    kvw = N_KV_HEADS * HEAD_DIM
    cache_k4 = cache_k.reshape(DEC_BATCH, N_EVEN, PAST_LEN, kvw)
    cache_v4 = cache_v.reshape(DEC_BATCH, N_EVEN, PAST_LEN, kvw)
    sink_flat = attn_sink.reshape(-1)
    log_gamma = jax.nn.log_sigmoid(ret_decay.astype(F32)).reshape(-1)
    w_router_pad = jnp.pad(w_router, ((0, 0), (0, 0), (0, LANES - N_EXPERTS)))
    w_router_hi = w_router_pad.astype(BF16)
    w_router_lo = (w_router_pad - w_router_hi.astype(F32)).astype(BF16)

    mix_rows = (R_G_MIX, R_SCALE_MIX, R_SHIFT_MIX)
    ffn_rows = (R_G_FFN, R_SCALE_FFN, R_SHIFT_FFN)

    new_k, new_v = [], []
    new_sf = new_sb = None
    h = _pre(x, pack)
    y_prompt = y_sample = None
    for i in range(DEPTH):
        j = i // 2
        if i % 2 == 0:
            q = _attn_proj(h, w_qkv, j, cos_attn, sin_attn, col0=0, ncols=N_HEADS * HEAD_DIM,
                           rope_cols=N_HEADS * HEAD_DIM, scale=HEAD_DIM ** -0.5 * LOG2E, out_dtype=BF16)
            kv = _attn_proj(h, w_qkv, j, cos_attn, sin_attn, col0=N_HEADS * HEAD_DIM, ncols=2 * kvw,
                            rope_cols=kvw, scale=1.0, out_dtype=F32)
            new_k.append(kv[:N_PROMPT, :kvw].reshape(BATCH, SEQ, N_KV_HEADS, HEAD_DIM))
            new_v.append(kv[:N_PROMPT, kvw:].reshape(BATCH, SEQ, N_KV_HEADS, HEAD_DIM))
            o_ctx = _ctx_attention(q, kv, sink_flat, j)
            o_lat = _lat_attention(q, kv, cache_k4, cache_v4, sink_flat, j)
            x, h = _down([o_ctx, o_lat], w_attn_o, j, x, pack, i, R_GATE_MIX, i, ffn_rows)
            act = _swiglu_up(h, w_ffn_gu, j)
            x, h = _down([act], w_ffn_down, j, x, pack, i, R_GATE_FFN, i + 1, mix_rows)
        else:
            proj = _ret_proj(h, w_ret_in, j, cos_ret, sin_ret)
            part, new_sf = _retention_pass(proj, state_fwd, log_gamma, j, False, states=new_sf)
            o_ret, new_sb = _retention_pass(proj, state_bwd, log_gamma, j, True, partial=part, states=new_sb)
            x, h, hf, route = _down([o_ret], w_ret_out, j, x, pack, i, R_GATE_MIX, i, ffn_rows,
                                    w_router=(w_router_hi, w_router_lo), router_layer=j)
            slot, pend, tile_expert, n_valid, tile_rows, next_expert = _routing_tables(route)
            hs = _dispatch_rows(slot, pend, hf)
            act = _moe_up(tile_expert, n_valid, tile_rows, next_expert, hs, w_exp_gu, j)
            ys = _moe_down(tile_expert, n_valid, tile_rows, next_expert, act, w_exp_down, j)
            if i == DEPTH - 1:
                y_prompt = _combine(slot, ys, x, route, pack, i, i, final_norm, 0, N_PROMPT, True)
                y_sample = _combine(slot, ys, x, route, pack, i, i, final_norm, N_PROMPT, N_SAMPLE, True)
            else:
                x, h = _combine(slot, ys, x, route, pack, i, i + 1, final_norm, 0, N_TOK, False)

    return (y_prompt.reshape(BATCH, SEQ, D_MODEL), y_sample.reshape(DEC_BATCH, DEC_SEQ, D_MODEL),
            jnp.stack(new_k, axis=1), jnp.stack(new_v, axis=1), new_sf, new_sb)
```

```python
import functools

import jax
import jax.numpy as jnp
from jax import lax
from jax.experimental import pallas as pl
from jax.experimental.pallas import tpu as pltpu

F32 = jnp.float32
BF16 = jnp.bfloat16
I32 = jnp.int32

D_MODEL = 1024
BATCH = 16
SEQ = 256
DEPTH = 4
DEC_BATCH = 2
DEC_SEQ = 4096
PAST_LEN = 256
GRID_W = 64
BLOCK = 128
EPS = 1e-6
N_HEADS = 16
N_KV_HEADS = 4
HEAD_DIM = 64
GROUP = N_HEADS // N_KV_HEADS
WINDOW = 128
ROPE_BASE = 10000.0
RET_HEADS = 4
RET_DK = 256
RET_DV = 512
RET_HK = RET_HEADS * RET_DK
RET_HV = RET_HEADS * RET_DV
D_FF = 2816
N_EXPERTS = 8
TOP_K = 2
D_FF_EXPERT = 3584
N_EVEN = 2
N_ODD = 2
NEG = -1e30
LOG2E = 1.4426950408889634

GROUP_TOKENS = 4096
N_PROMPT = BATCH * SEQ
N_SAMPLE = DEC_BATCH * DEC_SEQ
N_TOK = N_PROMPT + N_SAMPLE
N_GROUPS = N_TOK // GROUP_TOKENS

R_SHIFT_MIX, R_SCALE_MIX, R_GATE_MIX, R_SHIFT_FFN, R_SCALE_FFN, R_GATE_FFN, R_G_MIX, R_G_FFN = range(8)

VMEM_LIMIT_BYTES = 56 * 1024 * 1024
LANES = 128

TM = 2048
UP_CHUNK = 512
TM_DOWN = 512
DOWN_CHUNK = 128
MOE_TILE = 512
MOE_CHUNK = 256
MOE_SLOTS = TOP_K * N_TOK + N_EXPERTS * MOE_TILE
MOE_NT = MOE_SLOTS // MOE_TILE
TC_COMBINE = 256
COMBINE_GROUP = 16
DISPATCH_ROWS = 256


def _cparams(sem):
    return pltpu.CompilerParams(dimension_semantics=sem, vmem_limit_bytes=VMEM_LIMIT_BYTES)


def _normmod(x, g, scale, shift):
    ms = jnp.mean(x * x, axis=-1, keepdims=True)
    return (x * lax.rsqrt(ms + EPS) * g) * (1.0 + scale) + shift


def _mod_kernel(cond_ref, w_ref, b_ref, o_ref):
    c = cond_ref[...]
    s = c * jax.nn.sigmoid(c)
    o_ref[0] = jnp.dot(s.astype(BF16), w_ref[0].astype(BF16), preferred_element_type=F32) + b_ref[0]


def _modulations(cond8, w_mod, b_mod):
    tn = 2048
    n6 = 6 * D_MODEL
    return pl.pallas_call(
        _mod_kernel,
        grid=(DEPTH, n6 // tn),
        in_specs=[pl.BlockSpec((8, D_MODEL), lambda l, n: (0, 0)),
                  pl.BlockSpec((1, D_MODEL, tn), lambda l, n: (l, 0, n)),
                  pl.BlockSpec((1, 1, tn), lambda l, n: (l, 0, n))],
        out_specs=pl.BlockSpec((1, 8, tn), lambda l, n: (l, 0, n)),
        out_shape=jax.ShapeDtypeStruct((DEPTH, 8, n6), F32),
        compiler_params=_cparams(("arbitrary", "arbitrary")),
        name="modulations",
    )(cond8, w_mod, b_mod.reshape(DEPTH, 1, n6))


def _split_specs(tm, width):
    fst = N_PROMPT // tm
    return [pl.BlockSpec((tm, width), lambda m: (jnp.minimum(m, fst - 1), 0)),
            pl.BlockSpec((tm, width), lambda m: (jnp.maximum(m - fst, 0), 0))]


def _pre_kernel(xp_ref, xs_ref, pk_ref, h_ref, *, first_sample_tile):
    def norm(x_ref):
        pk = pk_ref[0, 0]
        h = _normmod(x_ref[...], pk[R_G_MIX:R_G_MIX + 1], pk[R_SCALE_MIX:R_SCALE_MIX + 1],
                     pk[R_SHIFT_MIX:R_SHIFT_MIX + 1])
        h_ref[...] = h.astype(BF16)

    m = pl.program_id(0)
    pl.when(m < first_sample_tile)(lambda: norm(xp_ref))
    pl.when(m >= first_sample_tile)(lambda: norm(xs_ref))


def _pre(x_pair, pack):
    tm = TM
    return pl.pallas_call(
        functools.partial(_pre_kernel, first_sample_tile=N_PROMPT // tm),
        grid=(N_TOK // tm,),
        in_specs=_split_specs(tm, D_MODEL) + [
            pl.BlockSpec((1, 1, 8, D_MODEL), lambda m: (0, (m * tm) // GROUP_TOKENS, 0, 0))],
        out_specs=pl.BlockSpec((tm, D_MODEL), lambda m: (m, 0)),
        out_shape=jax.ShapeDtypeStruct((N_TOK, D_MODEL), BF16),
        compiler_params=_cparams(("arbitrary",)),
        name="pre_norm",
    )(*x_pair, pack)


def _row_chunks(n_rows):
    return [pl.ds(c * UP_CHUNK, UP_CHUNK) for c in range(n_rows // UP_CHUNK)]


def _table_map(tm):
    first_sample_tile = N_PROMPT // tm
    tiles_per_seq = DEC_SEQ // tm
    return lambda m: (jnp.maximum(m - first_sample_tile, 0) % tiles_per_seq, 0)


def _rope64(y, cos, sin_signed):
    width = y.shape[-1]
    lane = lax.broadcasted_iota(I32, y.shape, 1)
    first = (lane % HEAD_DIM) < (HEAD_DIM // 2)
    swapped = jnp.where(first, pltpu.roll(y, width - HEAD_DIM // 2, 1), pltpu.roll(y, HEAD_DIM // 2, 1))
    reps = width // LANES
    c = jnp.concatenate([cos] * reps, axis=1) if reps > 1 else cos
    s = jnp.concatenate([sin_signed] * reps, axis=1) if reps > 1 else sin_signed
    return y * c + swapped * s


def _attn_proj_kernel(h_ref, w_ref, cos_ref, sin_ref, o_ref, wb_ref, *, scale, rope_cols, first_sample_tile):
    m = pl.program_id(0)

    @pl.when(m == 0)
    def _():
        wb_ref[...] = w_ref[0].astype(BF16)

    def project(rope):
        for rows in _row_chunks(h_ref.shape[0]):
            y = jnp.dot(h_ref[rows, :], wb_ref[...], preferred_element_type=F32)
            if scale != 1.0:
                y = y * scale
            if rope:
                roped = _rope64(y[:, :rope_cols], cos_ref[rows, :], sin_ref[rows, :])
                y = roped if rope_cols == y.shape[1] else jnp.concatenate([roped, y[:, rope_cols:]], axis=1)
            o_ref[rows, :] = y.astype(o_ref.dtype)

    pl.when(m >= first_sample_tile)(lambda: project(True))
    pl.when(m < first_sample_tile)(lambda: project(False))


def _attn_proj(h, w_qkv, layer, cos, sin, *, col0, ncols, rope_cols, scale, out_dtype):
    tm = TM
    kern = functools.partial(_attn_proj_kernel, scale=scale, rope_cols=rope_cols, first_sample_tile=N_PROMPT // tm)
    return pl.pallas_call(
        kern,
        grid=(N_TOK // tm,),
        in_specs=[pl.BlockSpec((tm, D_MODEL), lambda m: (m, 0)),
                  pl.BlockSpec((1, D_MODEL, ncols), lambda m: (layer, 0, col0 // ncols)),
                  pl.BlockSpec((tm, LANES), _table_map(tm)),
                  pl.BlockSpec((tm, LANES), _table_map(tm))],
        out_specs=pl.BlockSpec((tm, ncols), lambda m: (m, 0)),
        out_shape=jax.ShapeDtypeStruct((N_TOK, ncols), out_dtype),
        scratch_shapes=[pltpu.VMEM((D_MODEL, ncols), BF16)],
        compiler_params=_cparams(("arbitrary",)),
        name="attn_proj",
    )(h, w_qkv, cos, sin)


RET_TN = RET_HK


def _ret_proj_kernel(h_ref, w_ref, cos_ref, sin_ref, o_ref, wb_ref, *, first_sample_tile):
    n = pl.program_id(0)
    m = pl.program_id(1)

    @pl.when(m == 0)
    def _():
        wb_ref[...] = w_ref[0].astype(BF16)

    k_scale = jnp.where(n == 1, RET_DK ** -0.5, 1.0).astype(F32)
    half = RET_DK // 2

    def project(rope, gate=False):
        for rows in _row_chunks(h_ref.shape[0]):
            y = jnp.dot(h_ref[rows, :], wb_ref[...], preferred_element_type=F32) * k_scale
            if gate:
                y = y * jax.nn.sigmoid(y)
            if rope:
                c = cos_ref[rows, :]
                s = sin_ref[rows, :]
                parts = []
                for hh in range(RET_TN // RET_DK):
                    x1 = y[:, hh * RET_DK:hh * RET_DK + half]
                    x2 = y[:, hh * RET_DK + half:(hh + 1) * RET_DK]
                    parts.append(x1 * c - x2 * s)
                    parts.append(x1 * s + x2 * c)
                y = jnp.concatenate(parts, axis=1)
            o_ref[0, rows, :] = y.astype(BF16)

    gate_tile0 = (2 * RET_HK + RET_HV) // RET_TN
    do_rope = jnp.logical_and(m >= first_sample_tile, n < 2)
    is_gate = n >= gate_tile0
    pl.when(do_rope)(lambda: project(True))
    pl.when(is_gate)(lambda: project(False, gate=True))
    pl.when(jnp.logical_not(jnp.logical_or(do_rope, is_gate)))(lambda: project(False))


def _ret_proj(h, w_ret_in, layer, cos, sin):
    tm = TM
    ncols = 2 * RET_HK + 3 * RET_HV
    tmap = _table_map(tm)
    kern = functools.partial(_ret_proj_kernel, first_sample_tile=N_PROMPT // tm)
    return pl.pallas_call(
        kern,
        grid=(ncols // RET_TN, N_TOK // tm),
        in_specs=[pl.BlockSpec((tm, D_MODEL), lambda n, m: (m, 0)),
                  pl.BlockSpec((1, D_MODEL, RET_TN), lambda n, m: (layer, 0, n)),
                  pl.BlockSpec((tm, LANES), lambda n, m: tmap(m)),
                  pl.BlockSpec((tm, LANES), lambda n, m: tmap(m))],
        out_specs=pl.BlockSpec((1, tm, RET_TN), lambda n, m: (n, m, 0)),
        out_shape=jax.ShapeDtypeStruct((ncols // RET_TN, N_TOK, RET_TN), BF16),
        scratch_shapes=[pltpu.VMEM((D_MODEL, RET_TN), BF16)],
        compiler_params=_cparams(("arbitrary", "arbitrary")),
        name="ret_proj",
    )(h, w_ret_in, cos, sin)


def _swiglu_kernel(h_ref, wg_ref, wu_ref, o_ref, wgb_ref, wub_ref):
    m = pl.program_id(1)

    @pl.when(m == 0)
    def _():
        wgb_ref[...] = wg_ref[0].astype(BF16)
        wub_ref[...] = wu_ref[0].astype(BF16)

    for rows in _row_chunks(h_ref.shape[0]):
        h = h_ref[rows, :]
        g = jnp.dot(h, wgb_ref[...], preferred_element_type=F32)
        u = jnp.dot(h, wub_ref[...], preferred_element_type=F32)
        o_ref[rows, :] = (g * jax.nn.sigmoid(g) * u).astype(BF16)


def _swiglu_up(h, w_gu, layer):
    tm, tf = 1024, D_FF // 2
    nf = D_FF // tf
    return pl.pallas_call(
        _swiglu_kernel,
        grid=(nf, N_TOK // tm),
        in_specs=[pl.BlockSpec((tm, D_MODEL), lambda f, m: (m, 0)),
                  pl.BlockSpec((1, D_MODEL, tf), lambda f, m: (layer, 0, f)),
                  pl.BlockSpec((1, D_MODEL, tf), lambda f, m: (layer, 0, nf + f))],
        out_specs=pl.BlockSpec((tm, tf), lambda f, m: (m, f)),
        out_shape=jax.ShapeDtypeStruct((N_TOK, D_FF), BF16),
        scratch_shapes=[pltpu.VMEM((D_MODEL, tf), BF16), pltpu.VMEM((D_MODEL, tf), BF16)],
        compiler_params=_cparams(("arbitrary", "arbitrary")),
        name="swiglu_up",
    )(h, w_gu, w_gu)


def _route(logits):
    lane = lax.broadcasted_iota(I32, logits.shape, 1).astype(F32)
    lg = jnp.where(lane < N_EXPERTS, logits, -jnp.inf)
    m1 = jnp.max(lg, axis=-1, keepdims=True)
    i1 = jnp.min(jnp.where(lg == m1, lane, float(LANES)), axis=-1, keepdims=True)
    lg2 = jnp.where(lane == i1, -jnp.inf, lg)
    m2 = jnp.max(lg2, axis=-1, keepdims=True)
    i2 = jnp.min(jnp.where(lg2 == m2, lane, float(LANES)), axis=-1, keepdims=True)
    e2 = jnp.exp(m2 - m1)
    p1 = 1.0 / (1.0 + e2)
    p2 = e2 / (1.0 + e2)
    out = jnp.where(lane == 0, i1, 0.0)
    out = jnp.where(lane == 1, i2, out)
    out = jnp.where(lane == 2, p1, out)
    out = jnp.where(lane == 3, p2, out)
    return out


def _epilogue(x_new, pk_next, rows, fin_ref, final):
    if final:
        ms = jnp.mean(x_new * x_new, axis=-1, keepdims=True)
        return x_new * lax.rsqrt(ms + EPS) * fin_ref[...]
    g_row, sc_row, sh_row = rows
    return _normmod(x_new, pk_next[g_row:g_row + 1], pk_next[sc_row:sc_row + 1], pk_next[sh_row:sh_row + 1])


def _down_kernel(*refs, n_a, n_x, first_sample_tile, gate_row, next_rows, router):
    a_refs = refs[:n_a]
    w_ref = refs[n_a]
    x_refs = refs[n_a + 1:n_a + 1 + n_x]
    pk_ref, pkn_ref = refs[n_a + 1 + n_x:n_a + 3 + n_x]
    pos = n_a + 3 + n_x
    if router:
        wrh_ref, wrl_ref = refs[pos:pos + 2]
        pos += 2
        xo_ref, h_ref, hf_ref, rt_ref, wb_ref = refs[pos:pos + 5]
    else:
        xo_ref, h_ref, wb_ref = refs[pos:pos + 3]
    m = pl.program_id(0)

    @pl.when(m == 0)
    def _():
        wb_ref[...] = w_ref[0].astype(BF16)

    def finish(a_ref, x_ref):
        pk = pk_ref[0, 0]
        pkn = pkn_ref[0, 0]
        for c in range(a_ref.shape[0] // DOWN_CHUNK):
            rows = pl.ds(c * DOWN_CHUNK, DOWN_CHUNK)
            y = jnp.dot(a_ref[rows, :], wb_ref[...], preferred_element_type=F32)
            x_new = x_ref[rows, :] + pk[gate_row:gate_row + 1] * y
            xo_ref[rows, :] = x_new
            hn = _epilogue(x_new, pkn, next_rows, None, False)
            hi = hn.astype(BF16)
            h_ref[rows, :] = hi
            if router:
                hf_ref[rows, :] = hn
        if router:
            hi = h_ref[...]
            lo = (hf_ref[...] - hi.astype(F32)).astype(BF16)
            logits = (jnp.dot(hi, wrh_ref[0], preferred_element_type=F32)
                      + jnp.dot(lo, wrh_ref[0], preferred_element_type=F32)
                      + jnp.dot(hi, wrl_ref[0], preferred_element_type=F32))
            rt_ref[...] = _route(logits)

    if n_a == 1 and n_x == 1:
        finish(a_refs[0], x_refs[0])
    else:
        pl.when(m < first_sample_tile)(lambda: finish(a_refs[0], x_refs[0]))
        pl.when(m >= first_sample_tile)(lambda: finish(a_refs[-1], x_refs[-1]))


def _down(a_list, w, layer, x, pack, pack_layer, gate_row, next_layer, next_rows, w_router=None, router_layer=0):
    tm = TM_DOWN
    kd = w.shape[1]
    n_a = len(a_list)
    fst = N_PROMPT // tm
    router = w_router is not None
    x_list = list(x) if isinstance(x, (list, tuple)) else [x]
    n_x = len(x_list)
    a_specs = [pl.BlockSpec((tm, kd), lambda m: (m, 0))] if n_a == 1 else _split_specs(tm, kd)
    x_specs = [pl.BlockSpec((tm, D_MODEL), lambda m: (m, 0))] if n_x == 1 else _split_specs(tm, D_MODEL)
    in_specs = a_specs + [pl.BlockSpec((1, kd, D_MODEL), lambda m: (layer, 0, 0))] + x_specs + [
        pl.BlockSpec((1, 1, 8, D_MODEL), lambda m: (pack_layer, (m * tm) // GROUP_TOKENS, 0, 0)),
        pl.BlockSpec((1, 1, 8, D_MODEL), lambda m: (next_layer, (m * tm) // GROUP_TOKENS, 0, 0)),
    ]
    args = list(a_list) + [w] + list(x_list) + [pack, pack]
    out_specs = [pl.BlockSpec((tm, D_MODEL), lambda m: (m, 0)),
                 pl.BlockSpec((tm, D_MODEL), lambda m: (m, 0))]
    out_shape = [jax.ShapeDtypeStruct((N_TOK, D_MODEL), F32),
                 jax.ShapeDtypeStruct((N_TOK, D_MODEL), BF16)]
    if router:
        in_specs += [pl.BlockSpec((1, D_MODEL, LANES), lambda m: (router_layer, 0, 0)),
                     pl.BlockSpec((1, D_MODEL, LANES), lambda m: (router_layer, 0, 0))]
        args += list(w_router)
        out_specs += [pl.BlockSpec((tm, D_MODEL), lambda m: (m, 0)),
                      pl.BlockSpec((tm, LANES), lambda m: (m, 0))]
        out_shape += [jax.ShapeDtypeStruct((N_TOK, D_MODEL), F32),
                      jax.ShapeDtypeStruct((N_TOK, LANES), F32)]
    kern = functools.partial(_down_kernel, n_a=n_a, n_x=n_x, first_sample_tile=fst, gate_row=gate_row,
                             next_rows=next_rows, router=router)
    return pl.pallas_call(
        kern,
        grid=(N_TOK // tm,),
        in_specs=in_specs,
        out_specs=out_specs,
        out_shape=out_shape,
        scratch_shapes=[pltpu.VMEM((kd, D_MODEL), BF16)],
        compiler_params=_cparams(("arbitrary",)),
        name="down_proj",
    )(*args)


def _sink_row(sink_ref, layer, kvh, width):
    return jnp.concatenate(
        [jnp.full((1, width), sink_ref[layer * N_HEADS + kvh * GROUP + g] * LOG2E, F32) for g in range(GROUP)],
        axis=1)


def _group_queries(q_ref, kvh):
    return jnp.concatenate(
        [q_ref[:, (kvh * GROUP + g) * HEAD_DIM:(kvh * GROUP + g + 1) * HEAD_DIM] for g in range(GROUP)], axis=0)


_NT = (((1,), (1,)), ((), ()))


def _ctx_attn_kernel(sink_ref, q_ref, kv_ref, o_ref, *, layer):
    kvw = N_KV_HEADS * HEAD_DIM
    k_all = kv_ref[:, :kvw].astype(BF16)
    v_t = kv_ref[:, kvw:].T.astype(BF16)
    outs = []
    for kvh in range(N_KV_HEADS):
        hs = slice(kvh * HEAD_DIM, (kvh + 1) * HEAD_DIM)
        s_t = lax.dot_general(k_all[:, hs], _group_queries(q_ref, kvh), _NT, preferred_element_type=F32)
        sink = _sink_row(sink_ref, layer, kvh, SEQ)
        m = jnp.maximum(jnp.max(s_t, axis=0, keepdims=True), sink)
        p = jnp.exp2(s_t - m)
        denom = jnp.sum(p, axis=0, keepdims=True) + jnp.exp2(sink - m)
        o_t = jnp.dot(v_t[hs], p.astype(BF16), preferred_element_type=F32) / denom
        outs += [o_t[:, g * SEQ:(g + 1) * SEQ] for g in range(GROUP)]
    o_ref[...] = jnp.concatenate(outs, axis=0).T.astype(BF16)


def _ctx_attention(q, kv, sink_flat, layer):
    kern = functools.partial(_ctx_attn_kernel, layer=layer)
    return pl.pallas_call(
        kern,
        grid_spec=pltpu.PrefetchScalarGridSpec(
            num_scalar_prefetch=1,
            grid=(BATCH,),
            in_specs=[pl.BlockSpec((SEQ, D_MODEL), lambda b, s: (b, 0)),
                      pl.BlockSpec((SEQ, 2 * N_KV_HEADS * HEAD_DIM), lambda b, s: (b, 0))],
            out_specs=pl.BlockSpec((SEQ, D_MODEL), lambda b, s: (b, 0))),
        out_shape=jax.ShapeDtypeStruct((N_PROMPT, D_MODEL), BF16),
        compiler_params=_cparams(("arbitrary",)),
        name="ctx_attention",
    )(sink_flat, q, kv)


def _lat_attn_kernel(sink_ref, q_ref, kvp_ref, kvc_ref, kvn_ref, ck_ref, cv_ref, o_ref, kc_ref, vct_ref, *, layer):
    blk = pl.program_id(1)
    kvw = N_KV_HEADS * HEAD_DIM
    cols = GROUP * BLOCK
    n_lat = 3 * BLOCK

    @pl.when(blk == 0)
    def _():
        kc_ref[...] = ck_ref[0, 0].astype(BF16)
        vct_ref[...] = cv_ref[0, 0].T.astype(BF16)

    j = lax.broadcasted_iota(I32, (n_lat, cols), 0)
    r = lax.broadcasted_iota(I32, (n_lat, cols), 1) % BLOCK
    lo = jnp.maximum(r, BLOCK - BLOCK * blk)
    hi = jnp.minimum(r + 2 * WINDOW, DEC_SEQ + BLOCK - 1 - BLOCK * blk)
    lat_cap = jnp.where(jnp.logical_and(j >= lo, j <= hi), jnp.inf, NEG).astype(F32)
    k_lat = jnp.concatenate([kvp_ref[:, :kvw], kvc_ref[:, :kvw], kvn_ref[:, :kvw]], axis=0).astype(BF16)
    v_lat_t = jnp.concatenate([kvp_ref[:, kvw:].T, kvc_ref[:, kvw:].T, kvn_ref[:, kvw:].T], axis=1).astype(BF16)
    outs = []
    for kvh in range(N_KV_HEADS):
        hs = slice(kvh * HEAD_DIM, (kvh + 1) * HEAD_DIM)
        q4 = _group_queries(q_ref, kvh)
        s_ctx = lax.dot_general(kc_ref[:, hs], q4, _NT, preferred_element_type=F32)
        s_lat = lax.dot_general(k_lat[:, hs], q4, _NT, preferred_element_type=F32)
        s_lat = jnp.minimum(s_lat, lat_cap)
        sink = _sink_row(sink_ref, layer, kvh, BLOCK)
        m = jnp.maximum(jnp.maximum(jnp.max(s_ctx, axis=0, keepdims=True),
                                    jnp.max(s_lat, axis=0, keepdims=True)), sink)
        p_ctx = jnp.exp2(s_ctx - m)
        p_lat = jnp.exp2(s_lat - m)
        denom = (jnp.sum(p_ctx, axis=0, keepdims=True) + jnp.sum(p_lat, axis=0, keepdims=True)
                 + jnp.exp2(sink - m))
        o_t = (jnp.dot(vct_ref[hs, :], p_ctx.astype(BF16), preferred_element_type=F32)
               + jnp.dot(v_lat_t[hs], p_lat.astype(BF16), preferred_element_type=F32)) / denom
        outs += [o_t[:, g * BLOCK:(g + 1) * BLOCK] for g in range(GROUP)]
    o_ref[...] = jnp.concatenate(outs, axis=0).T.astype(BF16)


def _lat_attention(q, kv, cache_k4, cache_v4, sink_flat, layer):
    nb = DEC_SEQ // BLOCK
    base = N_PROMPT // BLOCK
    kvc = 2 * N_KV_HEADS * HEAD_DIM

    def row(b, i):
        return base + b * nb + i

    kern = functools.partial(_lat_attn_kernel, layer=layer)
    return pl.pallas_call(
        kern,
        grid_spec=pltpu.PrefetchScalarGridSpec(
            num_scalar_prefetch=1,
            grid=(DEC_BATCH, nb),
            in_specs=[pl.BlockSpec((BLOCK, D_MODEL), lambda b, i, s: (row(b, i), 0)),
                      pl.BlockSpec((BLOCK, kvc), lambda b, i, s: (row(b, jnp.maximum(i - 1, 0)), 0)),
                      pl.BlockSpec((BLOCK, kvc), lambda b, i, s: (row(b, i), 0)),
                      pl.BlockSpec((BLOCK, kvc), lambda b, i, s: (row(b, jnp.minimum(i + 1, nb - 1)), 0)),
                      pl.BlockSpec((1, 1, PAST_LEN, N_KV_HEADS * HEAD_DIM), lambda b, i, s: (b, layer, 0, 0)),
                      pl.BlockSpec((1, 1, PAST_LEN, N_KV_HEADS * HEAD_DIM), lambda b, i, s: (b, layer, 0, 0))],
            out_specs=pl.BlockSpec((BLOCK, D_MODEL), lambda b, i, s: (b * nb + i, 0)),
            scratch_shapes=[pltpu.VMEM((PAST_LEN, N_KV_HEADS * HEAD_DIM), BF16),
                            pltpu.VMEM((N_KV_HEADS * HEAD_DIM, PAST_LEN), BF16)]),
        out_shape=jax.ShapeDtypeStruct((N_SAMPLE, D_MODEL), BF16),
        compiler_params=_cparams(("arbitrary", "arbitrary")),
        name="lat_attention",
    )(sink_flat, q, kv, kv, kv, cache_k4, cache_v4)


RET_C = 256
RET_SUB = 1
RET_ROWS = RET_SUB * RET_C
RET_STEPS = N_TOK // RET_ROWS
PROMPT_STEPS = N_PROMPT // RET_ROWS
STEPS_PER_SAMPLE = DEC_SEQ // RET_ROWS
assert SEQ == RET_ROWS


def _ret_kernel(lg_ref, q_ref, k_ref, v0_ref, v1_ref, g0_ref, g1_ref, s0_ref, *rest, backward, layer):
    part_ref = rest[0] if backward else None
    o_ref, sout_ref, state_ref = rest[-3:]
    step = pl.program_id(0)
    blk = (RET_STEPS - 1 - step) if backward else step
    in_prompt = blk < PROMPT_STEPS
    first_of_sample = (blk - PROMPT_STEPS) % STEPS_PER_SAMPLE == (STEPS_PER_SAMPLE - 1 if backward else 0)

    @pl.when(in_prompt)
    def _():
        state_ref[...] = jnp.zeros_like(state_ref)

    @pl.when(jnp.logical_and(jnp.logical_not(in_prompt), first_of_sample))
    def _():
        state_ref[...] = s0_ref[0, 0]

    ii = lax.broadcasted_iota(I32, (RET_C, RET_C), 0).astype(F32)
    jj = lax.broadcasted_iota(I32, (RET_C, RET_C), 1).astype(F32)
    dist = (jj - ii) if backward else (ii - jj)
    ti = lax.broadcasted_iota(I32, (RET_C, 1), 0).astype(F32)
    q_pow = (RET_C - ti) if backward else (ti + 1.0)
    k_pow = ti if backward else (RET_C - 1.0 - ti)

    for h in range(RET_HEADS):
        lg = lg_ref[(layer * 2 + (1 if backward else 0)) * RET_HEADS + h]
        intra = jnp.where(dist >= 0, jnp.exp(lg * jnp.maximum(dist, 0.0)), 0.0)
        q_dec = jnp.exp(lg * q_pow)
        k_dec = jnp.exp(lg * k_pow)
        c_dec = jnp.exp(lg * RET_C)
        for sub in (reversed(range(RET_SUB)) if backward else range(RET_SUB)):
            rows = pl.ds(sub * RET_C, RET_C)
            vcols = slice((h % 2) * RET_DV, (h % 2 + 1) * RET_DV)
            q = q_ref[0, rows, h * RET_DK:(h + 1) * RET_DK]
            k = k_ref[0, rows, h * RET_DK:(h + 1) * RET_DK]
            v = (v0_ref, v1_ref)[h // 2][0, rows, vcols]
            s = state_ref[h]
            a = lax.dot_general(q, k, (((1,), (1,)), ((), ())), preferred_element_type=F32) * intra
            o = (jnp.dot(a.astype(BF16), v, preferred_element_type=F32)
                 + q_dec * jnp.dot(q, s.astype(BF16), preferred_element_type=F32))
            kd = (k.astype(F32) * k_dec).astype(BF16)
            state_ref[h] = c_dec * s + lax.dot_general(kd, v, (((0,), (0,)), ((), ())), preferred_element_type=F32)
            gate = (g0_ref, g1_ref)[h // 2][0, rows, vcols].astype(F32)
            on = o * lax.rsqrt(jnp.mean(o * o, axis=-1, keepdims=True) + EPS)
            res = on * gate
            if backward:
                res = res + part_ref[rows, h * RET_DV:(h + 1) * RET_DV].astype(F32)
            o_ref[rows, h * RET_DV:(h + 1) * RET_DV] = res.astype(BF16)

    @pl.when(in_prompt)
    def _():
        if sout_ref.shape[1] == 1:
            sout_ref[0, 0] = state_ref[...]
        else:
            for l in range(sout_ref.shape[1]):
                sout_ref[0, l] = state_ref[...] if l == layer else jnp.zeros_like(state_ref)


def _retention_pass(proj, s0, log_gamma_flat, layer, backward, partial=None, states=None):
    def blk(i):
        return (RET_STEPS - 1 - i) if backward else i

    def s0_map(i, lg):
        return (jnp.clip((blk(i) - PROMPT_STEPS) // STEPS_PER_SAMPLE, 0, DEC_BATCH - 1), layer, 0, 0, 0)

    sout_layers = N_ODD if states is None else 1

    def sout_map(i, lg):
        return (jnp.minimum(blk(i), BATCH - 1), 0 if states is None else layer, 0, 0, 0)

    gate_slab = 6 if backward else 4

    def slab(j):
        return pl.BlockSpec((1, RET_ROWS, RET_TN), lambda i, lg: (j, blk(i), 0))

    in_specs = [slab(0), slab(1), slab(2), slab(3), slab(gate_slab), slab(gate_slab + 1),
                pl.BlockSpec((1, 1, RET_HEADS, RET_DK, RET_DV), s0_map)]
    args = [proj] * 6 + [s0]
    if backward:
        in_specs.append(pl.BlockSpec((RET_ROWS, RET_HV), lambda i, lg: (blk(i), 0)))
        args.append(partial)
    aliases = {}
    if states is not None:
        in_specs.append(pl.BlockSpec(memory_space=pl.ANY))
        args.append(states)
        aliases = {len(args): 1}
    kern = functools.partial(_ret_kernel, backward=backward, layer=layer)
    return pl.pallas_call(
        kern,
        grid_spec=pltpu.PrefetchScalarGridSpec(
            num_scalar_prefetch=1,
            grid=(RET_STEPS,),
            in_specs=in_specs,
            out_specs=[pl.BlockSpec((RET_ROWS, RET_HV), lambda i, lg: (blk(i), 0)),
                       pl.BlockSpec((1, sout_layers, RET_HEADS, RET_DK, RET_DV), sout_map)],
            scratch_shapes=[pltpu.VMEM((RET_HEADS, RET_DK, RET_DV), F32)]),
        out_shape=[jax.ShapeDtypeStruct((N_TOK, RET_HV), BF16),
                   jax.ShapeDtypeStruct((BATCH, N_ODD, RET_HEADS, RET_DK, RET_DV), F32)],
        input_output_aliases=aliases,
        compiler_params=_cparams(("arbitrary",)),
        name="retention_bwd" if backward else "retention_fwd",
    )(log_gamma_flat, *args)


def _dispatch_kernel(slot_ref, pend_ref, h_ref, o_hbm, zbuf, sem):
    i = pl.program_id(0)

    @pl.when(i == 0)
    def _():
        zbuf[...] = jnp.zeros_like(zbuf)

        def tile_fill(start):
            return pltpu.make_async_copy(zbuf, o_hbm.at[pl.ds(pl.multiple_of(start, MOE_TILE), MOE_TILE)], sem)

        def nonempty(e):
            return pend_ref[e] > (pend_ref[e - 1] if e else 0)

        def start_unused(t, carry):
            tile_fill(t * MOE_TILE).start()
            return carry

        def wait_unused(t, carry):
            tile_fill(t * MOE_TILE).wait()
            return carry

        first_unused = pend_ref[N_EXPERTS - 1] // MOE_TILE
        for e in range(N_EXPERTS):
            pl.when(nonempty(e))(lambda e=e: tile_fill(pend_ref[e] - MOE_TILE).start())
        lax.fori_loop(first_unused, MOE_NT, start_unused, 0)
        for e in range(N_EXPERTS):
            pl.when(nonempty(e))(lambda e=e: tile_fill(pend_ref[e] - MOE_TILE).wait())
        lax.fori_loop(first_unused, MOE_NT, wait_unused, 0)

    base = i * DISPATCH_ROWS * TOP_K

    def start(r, carry):
        for k in range(TOP_K):
            pltpu.make_async_copy(h_ref.at[pl.ds(r, 1)], o_hbm.at[pl.ds(slot_ref[base + r * TOP_K + k], 1)],
                                  sem).start()
        return carry

    lax.fori_loop(0, DISPATCH_ROWS, start, 0, unroll=8)
    for k in range(TOP_K):
        pltpu.make_async_copy(h_ref, o_hbm.at[pl.ds(0, DISPATCH_ROWS)], sem).wait()


def _dispatch_rows(slot, pend, h):
    return pl.pallas_call(
        _dispatch_kernel,
        grid_spec=pltpu.PrefetchScalarGridSpec(
            num_scalar_prefetch=2,
            grid=(N_TOK // DISPATCH_ROWS,),
            in_specs=[pl.BlockSpec((DISPATCH_ROWS, D_MODEL), lambda i, s, p: (i, 0))],
            out_specs=pl.BlockSpec(memory_space=pl.ANY),
            scratch_shapes=[pltpu.VMEM((MOE_TILE, D_MODEL), F32), pltpu.SemaphoreType.DMA(())]),
        out_shape=jax.ShapeDtypeStruct((MOE_SLOTS, D_MODEL), F32),
        compiler_params=_cparams(("arbitrary",)),
        name="moe_dispatch",
    )(slot, pend, h)


def _for_occupied_chunks(n_rows, o_ref, chunk):
    n_chunks = MOE_TILE // MOE_CHUNK
    occupied = (n_rows + MOE_CHUNK - 1) // MOE_CHUNK

    def path(k):
        for c in range(n_chunks):
            rows = pl.ds(c * MOE_CHUNK, MOE_CHUNK)
            if c < k:
                chunk(rows)
            else:
                o_ref[rows, :] = jnp.zeros((MOE_CHUNK, o_ref.shape[1]), o_ref.dtype)

    for k in range(n_chunks + 1):
        pl.when(occupied == k)(functools.partial(path, k))


def _moe_up_kernel(te_ref, nv_ref, tr_ref, nx_ref, h_ref, w_hbm, o_ref, wbuf, wgb_ref, wub_ref, par_ref, sems, *,
                   layer, tf, nf):
    f = pl.program_id(0)
    m = pl.program_id(1)
    new_w = jnp.logical_or(m == 0, te_ref[m] != te_ref[jnp.maximum(m - 1, 0)])

    def fetch(col_tile, e, slot):
        col = pl.multiple_of(col_tile * tf, LANES)
        return [pltpu.make_async_copy(w_hbm.at[layer, e, :, pl.ds(half * D_FF_EXPERT + col, tf)],
                                      wbuf.at[slot, half], sems.at[slot]) for half in range(2)]

    @pl.when(jnp.logical_and(f == 0, m == 0))
    def _():
        par_ref[0] = 0
        for cp in fetch(0, te_ref[0], 0):
            cp.start()

    @pl.when(new_w)
    def _():
        slot = par_ref[0]
        for cp in fetch(f, te_ref[m], slot):
            cp.wait()
        nxt = nx_ref[m]

        @pl.when(nxt >= 0)
        def _():
            for cp in fetch(f, nxt, 1 - slot):
                cp.start()

        @pl.when(jnp.logical_and(nxt < 0, f + 1 < nf))
        def _():
            for cp in fetch(f + 1, te_ref[0], 1 - slot):
                cp.start()

        wgb_ref[...] = wbuf[slot, 0].astype(BF16)
        wub_ref[...] = wbuf[slot, 1].astype(BF16)
        par_ref[0] = 1 - slot

    def chunk(rows):
        h = h_ref[rows, :].astype(BF16)
        g = jnp.dot(h, wgb_ref[...], preferred_element_type=F32)
        u = jnp.dot(h, wub_ref[...], preferred_element_type=F32)
        o_ref[rows, :] = (g * jax.nn.sigmoid(g) * u).astype(BF16)

    _for_occupied_chunks(tr_ref[m], o_ref, chunk)


def _moe_up(tile_expert, n_valid, tile_rows, next_expert, hs, w_gu, layer):
    tf = 1792
    nf = D_FF_EXPERT // tf
    return pl.pallas_call(
        functools.partial(_moe_up_kernel, layer=layer, tf=tf, nf=nf),
        grid_spec=pltpu.PrefetchScalarGridSpec(
            num_scalar_prefetch=4,
            grid=(nf, MOE_NT),
            in_specs=[pl.BlockSpec((MOE_TILE, D_MODEL),
                                   lambda f, m, te, nv, tr, nx: (jnp.minimum(m, nv[0] - 1), 0)),
                      pl.BlockSpec(memory_space=pl.ANY)],
            out_specs=pl.BlockSpec((MOE_TILE, tf), lambda f, m, te, nv, tr, nx: (m, f)),
            scratch_shapes=[pltpu.VMEM((2, 2, D_MODEL, tf), F32),
                            pltpu.VMEM((D_MODEL, tf), BF16), pltpu.VMEM((D_MODEL, tf), BF16),
                            pltpu.SMEM((1,), I32), pltpu.SemaphoreType.DMA((2,))]),
        out_shape=jax.ShapeDtypeStruct((MOE_SLOTS, D_FF_EXPERT), BF16),
        compiler_params=_cparams(("arbitrary", "arbitrary")),
        name="moe_up",
    )(tile_expert, n_valid, tile_rows, next_expert, hs, w_gu)


def _moe_down_kernel(te_ref, nv_ref, tr_ref, nx_ref, a_ref, w_hbm, o_ref, wbuf, wb_ref, par_ref, sems, *, layer):
    m = pl.program_id(0)
    new_w = jnp.logical_or(m == 0, te_ref[m] != te_ref[jnp.maximum(m - 1, 0)])

    def fetch(e, slot):
        return pltpu.make_async_copy(w_hbm.at[layer, e], wbuf.at[slot], sems.at[slot])

    @pl.when(m == 0)
    def _():
        par_ref[0] = 0
        fetch(te_ref[0], 0).start()

    @pl.when(new_w)
    def _():
        slot = par_ref[0]
        fetch(te_ref[m], slot).wait()
        nxt = nx_ref[m]
        pl.when(nxt >= 0)(lambda: fetch(nxt, 1 - slot).start())
        wb_ref[...] = wbuf[slot].astype(BF16)
        par_ref[0] = 1 - slot

    def chunk(rows):
        o_ref[rows, :] = jnp.dot(a_ref[rows, :], wb_ref[...], preferred_element_type=F32)

    _for_occupied_chunks(tr_ref[m], o_ref, chunk)


def _moe_down(tile_expert, n_valid, tile_rows, next_expert, act, w_down, layer):
    return pl.pallas_call(
        functools.partial(_moe_down_kernel, layer=layer),
        grid_spec=pltpu.PrefetchScalarGridSpec(
            num_scalar_prefetch=4,
            grid=(MOE_NT,),
            in_specs=[pl.BlockSpec((MOE_TILE, D_FF_EXPERT),
                                   lambda m, te, nv, tr, nx: (jnp.minimum(m, nv[0] - 1), 0)),
                      pl.BlockSpec(memory_space=pl.ANY)],
            out_specs=pl.BlockSpec((MOE_TILE, D_MODEL), lambda m, te, nv, tr, nx: (m, 0)),
            scratch_shapes=[pltpu.VMEM((2, D_FF_EXPERT, D_MODEL), F32), pltpu.VMEM((D_FF_EXPERT, D_MODEL), BF16),
                            pltpu.SMEM((1,), I32), pltpu.SemaphoreType.DMA((2,))]),
        out_shape=jax.ShapeDtypeStruct((MOE_SLOTS, D_MODEL), F32),
        compiler_params=_cparams(("arbitrary",)),
        name="moe_down",
    )(tile_expert, n_valid, tile_rows, next_expert, act, w_down)


def _combine_kernel(slot_ref, y_hbm, x_ref, rt_ref, pk_ref, pkn_ref, fin_ref, *rest, tok0, final):
    if final:
        out_ref, ybuf, sems = rest
    else:
        xo_ref, h_ref, ybuf, sems = rest
    tc = TC_COMBINE
    i = pl.program_id(0)
    n = pl.num_programs(0)
    cur = i % 2
    nxt = 1 - cur

    def issue_rows(tile, buf, r0):
        base = (tok0 + tile * tc) * TOP_K
        for r in range(COMBINE_GROUP):
            for k in range(TOP_K):
                pltpu.make_async_copy(y_hbm.at[pl.ds(slot_ref[base + (r0 + r) * TOP_K + k], 1)],
                                      ybuf.at[buf, k, pl.ds(r0 + r, 1)], sems.at[buf]).start()

    def wait_tile(buf):
        for k in range(TOP_K):
            pltpu.make_async_copy(y_hbm.at[pl.ds(0, tc)], ybuf.at[buf, k], sems.at[buf]).wait()

    def group_rows(g):
        return pl.multiple_of(g * COMBINE_GROUP, COMBINE_GROUP)

    @pl.when(i == 0)
    def _():
        def first(g, carry):
            issue_rows(0, 0, group_rows(g))
            return carry

        lax.fori_loop(0, tc // COMBINE_GROUP, first, 0)

    wait_tile(cur)
    nxt_tile = jnp.minimum(i + 1, n - 1)
    pk = pk_ref[0, 0]
    pkn = pkn_ref[0, 0]

    def group(g, carry):
        r0 = group_rows(g)
        issue_rows(nxt_tile, nxt, r0)
        rows = pl.ds(r0, COMBINE_GROUP)
        rt = rt_ref[rows, :]
        moe = rt[:, 2:3] * ybuf[cur, 0, rows, :] + rt[:, 3:4] * ybuf[cur, 1, rows, :]
        x_new = x_ref[rows, :] + pk[R_GATE_FFN:R_GATE_FFN + 1] * moe
        if final:
            out_ref[rows, :] = _epilogue(x_new, None, None, fin_ref, True)
        else:
            xo_ref[rows, :] = x_new
            h_ref[rows, :] = _epilogue(x_new, pkn, (R_G_MIX, R_SCALE_MIX, R_SHIFT_MIX), None, False).astype(BF16)
        return carry

    lax.fori_loop(0, tc // COMBINE_GROUP, group, 0, unroll=True)

    @pl.when(i == n - 1)
    def _():
        wait_tile(nxt)


def _combine(slot, y, x, route, pack, layer, next_layer, final_norm, tok0, n_rows, final):
    tc = TC_COMBINE
    b0 = tok0 // tc

    def tmap(m, s):
        return (b0 + m, 0)

    def pmap(l):
        return lambda m, s: (l, (tok0 + m * tc) // GROUP_TOKENS, 0, 0)

    if final:
        out_specs = pl.BlockSpec((tc, D_MODEL), lambda m, s: (m, 0))
        out_shape = jax.ShapeDtypeStruct((n_rows, D_MODEL), F32)
    else:
        out_specs = [pl.BlockSpec((tc, D_MODEL), lambda m, s: (m, 0)),
                     pl.BlockSpec((tc, D_MODEL), lambda m, s: (m, 0))]
        out_shape = [jax.ShapeDtypeStruct((n_rows, D_MODEL), F32),
                     jax.ShapeDtypeStruct((n_rows, D_MODEL), BF16)]
    kern = functools.partial(_combine_kernel, tok0=tok0, final=final)
    return pl.pallas_call(
        kern,
        grid_spec=pltpu.PrefetchScalarGridSpec(
            num_scalar_prefetch=1,
            grid=(n_rows // tc,),
            in_specs=[pl.BlockSpec(memory_space=pl.ANY),
                      pl.BlockSpec((tc, D_MODEL), tmap),
                      pl.BlockSpec((tc, LANES), tmap),
                      pl.BlockSpec((1, 1, 8, D_MODEL), pmap(layer)),
                      pl.BlockSpec((1, 1, 8, D_MODEL), pmap(next_layer)),
                      pl.BlockSpec((1, D_MODEL), lambda m, s: (0, 0))],
            out_specs=out_specs,
            scratch_shapes=[pltpu.VMEM((2, TOP_K, tc, D_MODEL), F32), pltpu.SemaphoreType.DMA((2,))]),
        out_shape=out_shape,
        compiler_params=_cparams(("arbitrary",)),
        name="moe_combine",
    )(slot, y, x, route, pack, pack, final_norm.reshape(1, D_MODEL))


def _routing_tables(route):
    e_flat = route[:, :TOP_K].astype(I32).reshape(-1)
    onehot = (e_flat[:, None] == jnp.arange(N_EXPERTS, dtype=I32)[None, :]).astype(I32)
    csum = jnp.cumsum(onehot, axis=0)
    rank = jnp.sum(onehot * (csum - 1), axis=1)
    count = csum[-1]
    padded = ((count + MOE_TILE - 1) // MOE_TILE) * MOE_TILE
    pend = jnp.cumsum(padded)
    poff = pend - padded
    slot = jnp.sum(onehot * poff[None, :], axis=1) + rank
    n_valid = pend[-1] // MOE_TILE
    tile_start = jnp.arange(MOE_NT, dtype=I32) * MOE_TILE
    te_raw = jnp.minimum(jnp.sum((tile_start[:, None] >= pend[None, :]).astype(I32), axis=1), N_EXPERTS - 1)
    last_e = jnp.max(jnp.where(tile_start < pend[-1], te_raw, 0))
    tile_expert = jnp.minimum(te_raw, last_e)
    group_rows_end = jnp.sum((te_raw[:, None] == jnp.arange(N_EXPERTS, dtype=I32)[None, :]) * (poff + count)[None, :],
                             axis=1)
    used = tile_start < pend[-1]
    tile_rows = jnp.where(used, jnp.clip(group_rows_end - tile_start, 0, MOE_TILE), 0)
    later_other = jnp.logical_and(jnp.logical_and(tile_start[None, :] > tile_start[:, None], used[None, :]),
                                  tile_expert[None, :] != tile_expert[:, None])
    first_later = jnp.min(jnp.where(later_other, jnp.arange(MOE_NT, dtype=I32)[None, :], MOE_NT), axis=1)
    next_expert = jnp.where(first_later < MOE_NT, tile_expert[jnp.minimum(first_later, MOE_NT - 1)], -1)
    return (slot.astype(I32), pend.astype(I32), tile_expert.astype(I32), n_valid.astype(I32).reshape(1),
            tile_rows.astype(I32), next_expert.astype(I32))


def _rope_tables(dh):
    nf = dh // 4
    t = jnp.arange(DEC_SEQ)
    freqs = ROPE_BASE ** (-jnp.arange(nf, dtype=F32) / nf)
    row = (t // GRID_W).astype(F32)
    col = (t % GRID_W).astype(F32)
    ang = jnp.concatenate([row[:, None] * freqs, col[:, None] * freqs], axis=-1)
    return jnp.cos(ang), jnp.sin(ang)


def kernel(x_prompt, x_sample, cache_k, cache_v, state_fwd, state_bwd, c, c_ctx, norm_mix, norm_ffn, w_mod, b_mod,
           w_qkv, w_attn_o, attn_sink, w_ret_in, ret_decay, w_ret_out, w_ffn_gu, w_ffn_down, w_router, w_exp_gu,
           w_exp_down, final_norm):
    x = (x_prompt.reshape(N_PROMPT, D_MODEL), x_sample.reshape(N_SAMPLE, D_MODEL))

    cond8 = jnp.concatenate([c_ctx[None, :], c, jnp.zeros((8 - 1 - DEC_BATCH, D_MODEL), F32)], axis=0)
    mods = _modulations(cond8, w_mod, b_mod)
    m3 = mods[:, :N_GROUPS].reshape(DEPTH, N_GROUPS, 6, D_MODEL)
    pack = jnp.concatenate([
        m3,
        jnp.broadcast_to(norm_mix[:, None, None, :], (DEPTH, N_GROUPS, 1, D_MODEL)),
        jnp.broadcast_to(norm_ffn[:, None, None, :], (DEPTH, N_GROUPS, 1, D_MODEL))], axis=2)

    cos_a, sin_a = _rope_tables(HEAD_DIM)
    cos_attn = jnp.tile(cos_a, (1, LANES // (HEAD_DIM // 2)))
    sin_attn = jnp.tile(jnp.concatenate([-sin_a, sin_a], axis=1), (1, LANES // HEAD_DIM))
    cos_ret, sin_ret = _rope_tables(RET_DK)

    kvw = N_KV_HEADS * HEAD_DIM
    cache_k4 = cache_k.reshape(DEC_BATCH, N_EVEN, PAST_LEN, kvw)
    cache_v4 = cache_v.reshape(DEC_BATCH, N_EVEN, PAST_LEN, kvw)
    sink_flat = attn_sink.reshape(-1)
    log_gamma = jax.nn.log_sigmoid(ret_decay.astype(F32)).reshape(-1)
    w_router_pad = jnp.pad(w_router, ((0, 0), (0, 0), (0, LANES - N_EXPERTS)))
    w_router_hi = w_router_pad.astype(BF16)
    w_router_lo = (w_router_pad - w_router_hi.astype(F32)).astype(BF16)

    mix_rows = (R_G_MIX, R_SCALE_MIX, R_SHIFT_MIX)
    ffn_rows = (R_G_FFN, R_SCALE_FFN, R_SHIFT_FFN)

    new_k, new_v = [], []
    new_sf = new_sb = None
    h = _pre(x, pack)
    y_prompt = y_sample = None
    for i in range(DEPTH):
        j = i // 2
        if i % 2 == 0:
            q = _attn_proj(h, w_qkv, j, cos_attn, sin_attn, col0=0, ncols=N_HEADS * HEAD_DIM,
                           rope_cols=N_HEADS * HEAD_DIM, scale=HEAD_DIM ** -0.5 * LOG2E, out_dtype=BF16)
            kv = _attn_proj(h, w_qkv, j, cos_attn, sin_attn, col0=N_HEADS * HEAD_DIM, ncols=2 * kvw,
                            rope_cols=kvw, scale=1.0, out_dtype=F32)
            new_k.append(kv[:N_PROMPT, :kvw].reshape(BATCH, SEQ, N_KV_HEADS, HEAD_DIM))
            new_v.append(kv[:N_PROMPT, kvw:].reshape(BATCH, SEQ, N_KV_HEADS, HEAD_DIM))
            o_ctx = _ctx_attention(q, kv, sink_flat, j)
            o_lat = _lat_attention(q, kv, cache_k4, cache_v4, sink_flat, j)
            x, h = _down([o_ctx, o_lat], w_attn_o, j, x, pack, i, R_GATE_MIX, i, ffn_rows)
            act = _swiglu_up(h, w_ffn_gu, j)
            x, h = _down([act], w_ffn_down, j, x, pack, i, R_GATE_FFN, i + 1, mix_rows)
        else:
            proj = _ret_proj(h, w_ret_in, j, cos_ret, sin_ret)
            part, new_sf = _retention_pass(proj, state_fwd, log_gamma, j, False, states=new_sf)
            o_ret, new_sb = _retention_pass(proj, state_bwd, log_gamma, j, True, partial=part, states=new_sb)
            x, h, hf, route = _down([o_ret], w_ret_out, j, x, pack, i, R_GATE_MIX, i, ffn_rows,
                                    w_router=(w_router_hi, w_router_lo), router_layer=j)
            slot, pend, tile_expert, n_valid, tile_rows, next_expert = _routing_tables(route)
            hs = _dispatch_rows(slot, pend, hf)
            act = _moe_up(tile_expert, n_valid, tile_rows, next_expert, hs, w_exp_gu, j)
            ys = _moe_down(tile_expert, n_valid, tile_rows, next_expert, act, w_exp_down, j)
            if i == DEPTH - 1:
                y_prompt = _combine(slot, ys, x, route, pack, i, i, final_norm, 0, N_PROMPT, True)
                y_sample = _combine(slot, ys, x, route, pack, i, i, final_norm, N_PROMPT, N_SAMPLE, True)
            else:
                x, h = _combine(slot, ys, x, route, pack, i, i + 1, final_norm, 0, N_TOK, False)

    return (y_prompt.reshape(BATCH, SEQ, D_MODEL), y_sample.reshape(DEC_BATCH, DEC_SEQ, D_MODEL),
            jnp.stack(new_k, axis=1), jnp.stack(new_v, axis=1), new_sf, new_sb)
```
